```python
import functools
import jax, jax.numpy as jnp
from jax import lax
import numpy as np

D_MODEL = 1024
BATCH = 32
SEQ = 256
DEPTH = 4
DEC_BATCH = 2
DEC_SEQ = 1024
PAST_LEN = 256

GRID_W = 64
HEAD_DIM = 64
RWKV_HEADS = 4
NAT_HEADS = 4
HGRN_HEADS = 4
SWA_HEADS = 4
SWA_KV_HEADS = 2
RWKV_W = RWKV_HEADS * HEAD_DIM
NAT_W = NAT_HEADS * HEAD_DIM
HGRN_W = HGRN_HEADS * HEAD_DIM
SWA_W = SWA_HEADS * HEAD_DIM
SWA_KV_W = SWA_KV_HEADS * HEAD_DIM
D_MIX = RWKV_W + NAT_W + HGRN_W + SWA_W
RWKV_LORA_RANK = 64
RWKV_GATE_RANK = 128
RWKV_DECAY_SCALE = 0.6065306597126334
RWKV_LN_EPS = 64e-5
NAT_KH = 8
NAT_KW = 16
NAT_QCOLS = 16
NAT_KCOLS = 32
HGRN_CHUNK = 64
SWA_WINDOW = 128
SWA_BLOCK = 128
CTX_QBLOCK = 128
ROPE_BASE = 10000.0
FFN_HIDDEN = 4 * D_MODEL
NORM_EPS = 1e-6
MASK_VALUE = -1e30
N_MOD = 6
ATTN_SCALE = HEAD_DIM ** -0.5
IN_SPLITS = (RWKV_W, RWKV_W, RWKV_W, RWKV_GATE_RANK, RWKV_LORA_RANK, RWKV_LORA_RANK, RWKV_LORA_RANK, RWKV_LORA_RANK,
             NAT_W, NAT_W, NAT_W,
             HGRN_W, HGRN_W, HGRN_W, HGRN_W, HGRN_W,
             SWA_W, SWA_KV_W, SWA_KV_W)
D_IN = 3 * RWKV_W + RWKV_GATE_RANK + 4 * RWKV_LORA_RANK + 3 * NAT_W + 5 * HGRN_W + SWA_W + 2 * SWA_KV_W

kernel_name = 'hybrid_rwkv7_nat_hgrn2_swa_dit_step'


def _rms(x, gain):
    xf = x.astype(jnp.float32)
    y = xf * lax.rsqrt(jnp.mean(xf * xf, axis=-1, keepdims=True) + NORM_EPS)
    return y.astype(x.dtype) * gain


def _modulation(cond, w, b):
    m = jax.nn.silu(cond) @ w + b
    return m.reshape(cond.shape[0], N_MOD, D_MODEL)


def _split_cols(p):
    cuts, acc = [], 0
    for s in IN_SPLITS[:-1]:
        acc += s
        cuts.append(acc)
    return jnp.split(p, cuts, axis=-1)


def _heads(x, n):
    return x.reshape(x.shape[0], x.shape[1], n, HEAD_DIM)


def _shift_lerp(x, mu):
    prev = jnp.pad(x[:, :-1], ((0, 0), (1, 0), (0, 0)))
    return x + (prev - x) * mu


def _axial_rope(x):
    T = x.shape[1]
    t = jnp.arange(T)
    row = (t // GRID_W).astype(jnp.float32)
    col = (t % GRID_W).astype(jnp.float32)
    half = HEAD_DIM // 2
    nf = half // 2
    inv = 1.0 / (ROPE_BASE ** (jnp.arange(nf, dtype=jnp.float32) / nf))

    def rot(xa, pos):
        ang = pos[:, None] * inv[None, :]
        cs = jnp.cos(ang)[None, :, None, :].astype(x.dtype)
        sn = jnp.sin(ang)[None, :, None, :].astype(x.dtype)
        x1, x2 = xa[..., :nf], xa[..., nf:]
        return jnp.concatenate([x1 * cs - x2 * sn, x2 * cs + x1 * sn], axis=-1)

    return jnp.concatenate([rot(x[..., :half], row), rot(x[..., half:], col)], axis=-1)


def _ctx_attention(q, k, v, sink):
    B, L, H, dh = q.shape
    KV = k.shape[2]
    G = H // KV
    nb = L // CTX_QBLOCK
    qb = jnp.moveaxis((q * ATTN_SCALE).reshape(B, nb, CTX_QBLOCK, KV, G, dh), 1, 0)

    def one_block(qi):
        s = jnp.einsum('bqkgd,blkd->bkgql', qi, k).astype(jnp.float32)
        if sink is not None:
            col = jnp.broadcast_to(sink.reshape(KV, G)[None, :, :, None, None].astype(jnp.float32), s.shape[:-1] + (1,))
            s = jnp.concatenate([s, col], axis=-1)
        p = jax.nn.softmax(s, axis=-1)[..., :L]
        return jnp.einsum('bkgql,blkd->bqkgd', p.astype(v.dtype), v)

    o = lax.map(one_block, qb)
    return jnp.moveaxis(o, 0, 1).reshape(B, L, H * dh)


def _nat_latent(q, k, v, ck, cv, rpb):
    B, T, H, dh = q.shape
    dt = q.dtype
    rows = T // GRID_W
    kh = min(NAT_KH, rows)
    ncb = GRID_W // NAT_QCOLS
    row_start = jnp.clip(jnp.arange(rows) - kh // 2, 0, rows - kh)
    row_idx = row_start[:, None] + jnp.arange(kh)[None, :]
    qcol = jnp.arange(GRID_W).reshape(ncb, NAT_QCOLS)
    win_start = jnp.clip(qcol - NAT_KW // 2, 0, GRID_W - NAT_KW)
    kcol_start = jnp.clip(jnp.arange(ncb) * NAT_QCOLS - NAT_KW // 2, 0, GRID_W - NAT_KCOLS)
    col_idx = kcol_start[:, None] + jnp.arange(NAT_KCOLS)[None, :]
    kc = col_idx[:, None, :]
    col_ok = (kc >= win_start[..., None]) & (kc < win_start[..., None] + NAT_KW)
    roff = row_idx - jnp.arange(rows)[:, None] + NAT_KH - 1
    coff = jnp.clip(kc - qcol[..., None], -(NAT_KW - 1), NAT_KW - 1) + NAT_KW - 1
    bias = rpb[:, roff[:, None, None, :, None], coff[None, :, :, None, :]].astype(jnp.float32)
    kg = k.reshape(B, rows, GRID_W, H, dh)
    vg = v.reshape(B, rows, GRID_W, H, dh)
    ri = row_idx[:, :, None, None]
    ci = col_idx[None, None, :, :]
    kb = kg[:, ri, ci]
    vb = vg[:, ri, ci]
    qg = (q * ATTN_SCALE).reshape(B, rows, ncb, NAT_QCOLS, H, dh)
    s_loc = jnp.einsum('brcqhd,brackhd->bhrcqak', qg, kb).astype(jnp.float32) + bias
    s_loc = jnp.where(col_ok[None, :, :, None, :], s_loc, MASK_VALUE)
    nloc = kh * NAT_KCOLS
    s_loc = s_loc.reshape(B, H, rows, ncb, NAT_QCOLS, nloc)
    s_ctx = jnp.einsum('brcqhd,blhd->bhrcql', qg, ck).astype(jnp.float32)
    p = jax.nn.softmax(jnp.concatenate([s_loc, s_ctx], axis=-1), axis=-1)
    p_loc = p[..., :nloc].reshape(B, H, rows, ncb, NAT_QCOLS, kh, NAT_KCOLS).astype(dt)
    p_ctx = p[..., nloc:].astype(dt)
    o = jnp.einsum('bhrcqak,brackhd->brcqhd', p_loc, vb) + jnp.einsum('bhrcql,blhd->brcqhd', p_ctx, cv)
    return o.reshape(B, T, H * dh)


def _swa_latent(q, k, v, ck, cv, sink):
    B, T, H, dh = q.shape
    dt = q.dtype
    KV = k.shape[2]
    G = H // KV
    L = ck.shape[1]
    nb = T // SWA_BLOCK
    pad = ((0, 0), (SWA_BLOCK, SWA_BLOCK), (0, 0), (0, 0))
    kp, vp = jnp.pad(k, pad), jnp.pad(v, pad)
    idx = jnp.arange(nb)[:, None] * SWA_BLOCK + jnp.arange(3 * SWA_BLOCK)[None, :]
    kb, vb = kp[:, idx], vp[:, idx]
    qb = (q * ATTN_SCALE).reshape(B, nb, SWA_BLOCK, KV, G, dh)
    qpos = jnp.arange(nb)[:, None] * SWA_BLOCK + jnp.arange(SWA_BLOCK)[None, :]
    kpos = (idx - SWA_BLOCK)[:, None, :]
    ok = (jnp.abs(kpos - qpos[:, :, None]) <= SWA_WINDOW) & (kpos >= 0) & (kpos < T)
    s_loc = jnp.einsum('bnqkgd,bnskd->bnkgqs', qb, kb).astype(jnp.float32)
    s_loc = jnp.where(ok[None, :, None, None], s_loc, MASK_VALUE)
    s_ctx = jnp.einsum('bnqkgd,blkd->bnkgql', qb, ck).astype(jnp.float32)
    s_sink = jnp.broadcast_to(sink.reshape(KV, G)[None, None, :, :, None, None].astype(jnp.float32), s_loc.shape[:-1] + (1,))
    p = jax.nn.softmax(jnp.concatenate([s_loc, s_ctx, s_sink], axis=-1), axis=-1)
    nloc = 3 * SWA_BLOCK
    o = (jnp.einsum('bnkgqs,bnskd->bnqkgd', p[..., :nloc].astype(dt), vb)
         + jnp.einsum('bnkgql,blkd->bnqkgd', p[..., nloc:nloc + L].astype(dt), cv))
    return o.reshape(B, T, H * dh)


def _rwkv_direction(r, k, v, wh, ah, mu_rkv, mu_lora, w0, w2, a0, a2, k_k, k_a, r_k, s0):
    B, T, _ = r.shape
    dt = r.dtype
    r = _shift_lerp(r, mu_rkv[0])
    k = _shift_lerp(k, mu_rkv[1])
    v = _shift_lerp(v, mu_rkv[2])
    wh = _shift_lerp(wh, mu_lora[0])
    ah = _shift_lerp(ah, mu_lora[1])
    log_w = -RWKV_DECAY_SCALE * jax.nn.sigmoid((w0 + jnp.tanh(wh) @ w2).astype(jnp.float32))
    a = jax.nn.sigmoid((a0 + ah @ a2).astype(jnp.float32))
    hd = lambda t: t.reshape(B, T, RWKV_HEADS, HEAD_DIM).astype(jnp.float32)
    r, k, v, w, a = hd(r), hd(k), hd(v), hd(jnp.exp(log_w)), hd(a)
    kk = k * k_k.reshape(RWKV_HEADS, HEAD_DIM)
    kk = kk / jnp.maximum(jnp.sqrt(jnp.sum(kk * kk, axis=-1, keepdims=True)), 1e-12)
    k = k * (1.0 + (a - 1.0) * k_a.reshape(RWKV_HEADS, HEAD_DIM))

    def step(S, inp):
        r_t, w_t, k_t, v_t, kk_t, a_t = inp
        sa = jnp.einsum('bhvk,bhk->bhv', S, -kk_t)
        S = S * w_t[:, :, None, :] + sa[..., None] * (kk_t * a_t)[:, :, None, :] + v_t[..., None] * k_t[:, :, None, :]
        return S, jnp.einsum('bhvk,bhk->bhv', S, r_t)

    tm = lambda t: jnp.moveaxis(t, 1, 0)
    s_fin, y = lax.scan(step, s0.astype(jnp.float32), (tm(r), tm(w), tm(k), tm(v), tm(kk), tm(a)))
    y = jnp.moveaxis(y, 0, 1)
    bonus = jnp.sum(r * k * r_k, axis=-1, keepdims=True) * v
    return y, bonus, s_fin.astype(dt)


def _rwkv_mixer(r, k, v, gh, whf, ahf, whb, ahb, lw, s_f, s_b):
    B, T, _ = r.shape
    dt = r.dtype

    def direction(d, r_, k_, v_, wh_, ah_, s0):
        return _rwkv_direction(r_, k_, v_, wh_, ah_, lw['rw_mu_rkv'][d], lw['rw_mu_lora'][d], lw['rw_w0'][d],
                               lw['rw_w2'][d], lw['rw_a0'][d], lw['rw_a2'][d], lw['rw_kk'], lw['rw_ka'],
                               lw['rw_rk'], s0)

    rev = lambda t: t[:, ::-1]
    yf, bf, sf = direction(0, r, k, v, whf, ahf, s_f)
    yb, bb, sb = direction(1, rev(r), rev(k), rev(v), rev(whb), rev(ahb), s_b)
    y = yf + rev(yb)
    mu = jnp.mean(y, axis=-1, keepdims=True)
    var = jnp.mean(jnp.square(y - mu), axis=-1, keepdims=True)
    y = ((y - mu) * lax.rsqrt(var + RWKV_LN_EPS)).reshape(B, T, RWKV_W) * lw['rw_lnx_w'] + lw['rw_lnx_b']
    y = y + (bf + rev(bb)).reshape(B, T, RWKV_W)
    g = jax.nn.sigmoid(gh) @ lw['rw_g2']
    return y.astype(dt) * g, sf, sb


def _hgrn_direction(q, v, f_raw, lb, s0):
    B, T, H, N = q.shape
    dt = q.dtype
    C = HGRN_CHUNK
    nc = T // C
    f_raw = f_raw.astype(jnp.float32)
    f = lb + (1.0 - lb) * jax.nn.sigmoid(f_raw)
    log_f = jnp.log(f)
    k = (1.0 - lb) * jax.nn.sigmoid(-f_raw)
    chunks = lambda t: jnp.moveaxis(t.astype(jnp.float32).reshape(B, nc, C, H, N), 1, 0)
    causal = jnp.tril(jnp.ones((C, C), dtype=bool))[None, :, :, None, None]

    def step(S, inp):
        qc, kc, vc, lfc = inp
        b = jnp.cumsum(lfc, axis=1)
        o_inter = jnp.einsum('bthk,bhkv->bthv', qc * jnp.exp(b), S)
        diff = b[:, :, None] - b[:, None, :]
        dec = jnp.where(causal, jnp.exp(jnp.where(causal, diff, 0.0)), 0.0)
        att = jnp.einsum('bthk,bshk,btshk->bhts', qc, kc, dec)
        o_intra = jnp.einsum('bhts,bshv->bthv', att, vc)
        b_end = b[:, -1]
        S = S * jnp.exp(b_end)[..., None] + jnp.einsum('bshk,bshv->bhkv', kc * jnp.exp(b_end[:, None] - b), vc)
        return S, o_inter + o_intra

    s_fin, o = lax.scan(step, s0.astype(jnp.float32), (chunks(q), chunks(k), chunks(v), chunks(log_f)))
    o = jnp.moveaxis(o, 0, 1).reshape(B, T, H, N)
    return o.astype(dt), s_fin.astype(dt)


def _hgrn_mixer(q, i, g, f_f, f_b, lw, s_f, s_b):
    B, T, _ = q.shape
    dt = q.dtype
    q4 = _heads(jax.nn.silu(q), HGRN_HEADS)
    v4 = _heads(i, HGRN_HEADS)
    lb = lw['hg_lb'].reshape(2, HGRN_HEADS, HEAD_DIM)
    rev = lambda t: t[:, ::-1]
    of, sf = _hgrn_direction(q4, v4, _heads(f_f, HGRN_HEADS), lb[0], s_f)
    ob, sb = _hgrn_direction(rev(q4), rev(v4), rev(_heads(f_b, HGRN_HEADS)), lb[1], s_b)
    o = (of + rev(ob)).astype(jnp.float32)
    o = o * lax.rsqrt(jnp.mean(o * o, axis=-1, keepdims=True) + NORM_EPS)
    o = o.reshape(B, T, HGRN_W).astype(dt) * lw['hg_norm'] * jax.nn.sigmoid(g)
    return o, sf, sb


def _mix_context(h, lw):
    (r, k, v, gh, whf, ahf, whb, ahb, nq, nk, nv, hq, hi, hg, hff, hfb, sq, sk, sv) = _split_cols(h @ lw['w_in'])
    B = h.shape[0]
    z = jnp.zeros((B, RWKV_HEADS, HEAD_DIM, HEAD_DIM), h.dtype)
    ya, rs_f, rs_b = _rwkv_mixer(r, k, v, gh, whf, ahf, whb, ahb, lw, z, z)
    nk4, nv4 = _heads(nk, NAT_HEADS), _heads(nv, NAT_HEADS)
    yb = _ctx_attention(_heads(nq, NAT_HEADS), nk4, nv4, None)
    zh = jnp.zeros((B, HGRN_HEADS, HEAD_DIM, HEAD_DIM), h.dtype)
    yc, hs_f, hs_b = _hgrn_mixer(hq, hi, hg, hff, hfb, lw, zh, zh)
    sk4, sv4 = _heads(sk, SWA_KV_HEADS), _heads(sv, SWA_KV_HEADS)
    yd = _ctx_attention(_heads(sq, SWA_HEADS), sk4, sv4, lw['swa_sink'])
    y = jnp.concatenate([ya, yb, yc, yd], axis=-1)
    new = (jnp.stack([nk4, nv4], axis=1), jnp.stack([sk4, sv4], axis=1),
           jnp.stack([rs_f, rs_b], axis=1), jnp.stack([hs_f, hs_b], axis=1))
    return y, new


def _mix_latent(h, lw, nat_kv, swa_kv, rw_s, hg_s):
    (r, k, v, gh, whf, ahf, whb, ahb, nq, nk, nv, hq, hi, hg, hff, hfb, sq, sk, sv) = _split_cols(h @ lw['w_in'])
    ya, _, _ = _rwkv_mixer(r, k, v, gh, whf, ahf, whb, ahb, lw, rw_s[:, 0], rw_s[:, 1])
    yb = _nat_latent(_heads(nq, NAT_HEADS), _heads(nk, NAT_HEADS), _heads(nv, NAT_HEADS),
                     nat_kv[:, 0], nat_kv[:, 1], lw['nat_rpb'])
    yc, _, _ = _hgrn_mixer(hq, hi, hg, hff, hfb, lw, hg_s[:, 0], hg_s[:, 1])
    yd = _swa_latent(_axial_rope(_heads(sq, SWA_HEADS)), _axial_rope(_heads(sk, SWA_KV_HEADS)),
                     _heads(sv, SWA_KV_HEADS), swa_kv[:, 0], swa_kv[:, 1], lw['swa_sink'])
    return jnp.concatenate([ya, yb, yc, yd], axis=-1), None


def _block(x, mod, lw, mix):
    g = lw['norm_g']
    h = _rms(x, g[0]) * (1 + mod[:, None, 1]) + mod[:, None, 0]
    y, aux = mix(h)
    x = x + mod[:, None, 2] * _rms(y @ lw['w_out'], g[1])
    h = _rms(x, g[2]) * (1 + mod[:, None, 4]) + mod[:, None, 3]
    f = jnp.square(jax.nn.relu(h @ lw['ffn_w1'])) @ lw['ffn_w2']
    x = x + mod[:, None, 5] * _rms(f, g[3])
    return x, aux


def setup_inputs(seed: int = 0) -> dict:
    key = jax.random.key(seed)
    ks = jax.random.split(key, 32)
    nrm = lambda i, shape, s: s * jax.random.normal(ks[i], shape, jnp.float32)
    uni = lambda i, shape: jax.random.uniform(ks[i], shape, jnp.float32)
    R = RWKV_LORA_RANK
    return {
        'x_prompt': nrm(0, (BATCH, SEQ, D_MODEL), 1.0),
        'x_sample': nrm(1, (DEC_BATCH, DEC_SEQ, D_MODEL), 1.0),
        'cache_nat_kv': nrm(2, (DEC_BATCH, DEPTH, 2, PAST_LEN, NAT_HEADS, HEAD_DIM), 1.0),
        'cache_swa_kv': nrm(3, (DEC_BATCH, DEPTH, 2, PAST_LEN, SWA_KV_HEADS, HEAD_DIM), 1.0),
        'state_rwkv': nrm(4, (DEC_BATCH, DEPTH, 2, RWKV_HEADS, HEAD_DIM, HEAD_DIM), 0.3),
        'state_hgrn': nrm(5, (DEC_BATCH, DEPTH, 2, HGRN_HEADS, HEAD_DIM, HEAD_DIM), 0.3),
        'c': nrm(6, (DEC_BATCH, D_MODEL), 1.0),
        'c_ctx': nrm(7, (D_MODEL,), 1.0),
        'norm_g': 1.0 + nrm(8, (DEPTH, 4, D_MODEL), 0.05),
        'mod_w': nrm(9, (DEPTH, D_MODEL, N_MOD * D_MODEL), 0.5 * D_MODEL ** -0.5),
        'mod_b': nrm(10, (DEPTH, N_MOD * D_MODEL), 0.02),
        'w_in': nrm(11, (DEPTH, D_MODEL, D_IN), D_MODEL ** -0.5),
        'w_out': nrm(12, (DEPTH, D_MIX, D_MODEL), D_MIX ** -0.5),
        'rw_mu_rkv': uni(13, (DEPTH, 2, 3, RWKV_W)),
        'rw_mu_lora': uni(14, (DEPTH, 2, 2, R)),
        'rw_w0': nrm(15, (DEPTH, 2, RWKV_W), 0.5),
        'rw_w2': nrm(16, (DEPTH, 2, R, RWKV_W), 0.1),
        'rw_a0': nrm(17, (DEPTH, 2, RWKV_W), 0.1),
        'rw_a2': nrm(18, (DEPTH, 2, R, RWKV_W), 0.5 * R ** -0.5),
        'rw_g2': nrm(19, (DEPTH, RWKV_GATE_RANK, RWKV_W), RWKV_GATE_RANK ** -0.5),
        'rw_kk': 0.85 + nrm(20, (DEPTH, RWKV_W), 0.05),
        'rw_ka': 1.0 + nrm(21, (DEPTH, RWKV_W), 0.05),
        'rw_rk': nrm(22, (DEPTH, RWKV_HEADS, HEAD_DIM), 0.1),
        'rw_lnx_w': 1.0 + nrm(23, (DEPTH, RWKV_W), 0.05),
        'rw_lnx_b': nrm(24, (DEPTH, RWKV_W), 0.02),
        'nat_rpb': nrm(25, (DEPTH, NAT_HEADS, 2 * NAT_KH - 1, 2 * NAT_KW - 1), 0.2),
        'hg_lb_logits': nrm(26, (2, DEPTH, HGRN_W), 0.1),
        'hg_norm': 1.0 + nrm(27, (DEPTH, HGRN_W), 0.05),
        'swa_sink': nrm(28, (DEPTH, SWA_HEADS), 0.5),
        'ffn_w1': nrm(29, (DEPTH, D_MODEL, FFN_HIDDEN), D_MODEL ** -0.5),
        'ffn_w2': nrm(30, (DEPTH, FFN_HIDDEN, D_MODEL), FFN_HIDDEN ** -0.5),
    }


def reference(x_prompt, x_sample, cache_nat_kv, cache_swa_kv, state_rwkv, state_hgrn, c, c_ctx,
              norm_g, mod_w, mod_b, w_in, w_out, rw_mu_rkv, rw_mu_lora, rw_w0, rw_w2, rw_a0, rw_a2,
              rw_g2, rw_kk, rw_ka, rw_rk, rw_lnx_w, rw_lnx_b, nat_rpb, hg_lb_logits, hg_norm,
              swa_sink, ffn_w1, ffn_w2):
    lb_sm = jax.nn.softmax(hg_lb_logits.astype(jnp.float32), axis=1)
    hg_lb = jnp.cumsum(lb_sm, axis=1) - lb_sm[:, :1]
    xp, xs = x_prompt, x_sample
    nat_out, swa_out, rw_out, hg_out = [], [], [], []
    for l in range(DEPTH):
        lw = {
            'norm_g': norm_g[l], 'w_in': w_in[l], 'w_out': w_out[l], 'ffn_w1': ffn_w1[l], 'ffn_w2': ffn_w2[l],
            'rw_mu_rkv': rw_mu_rkv[l], 'rw_mu_lora': rw_mu_lora[l], 'rw_w0': rw_w0[l], 'rw_w2': rw_w2[l],
            'rw_a0': rw_a0[l], 'rw_a2': rw_a2[l], 'rw_g2': rw_g2[l], 'rw_kk': rw_kk[l], 'rw_ka': rw_ka[l],
            'rw_rk': rw_rk[l], 'rw_lnx_w': rw_lnx_w[l], 'rw_lnx_b': rw_lnx_b[l], 'nat_rpb': nat_rpb[l],
            'hg_lb': hg_lb[:, l], 'hg_norm': hg_norm[l], 'swa_sink': swa_sink[l],
        }
        mod_p = _modulation(c_ctx[None, :], mod_w[l], mod_b[l])
        xp, (nkv, skv, rws, hgs) = _block(xp, mod_p, lw, functools.partial(_mix_context, lw=lw))
        nat_out.append(nkv)
        swa_out.append(skv)
        rw_out.append(rws)
        hg_out.append(hgs)
        mod_s = _modulation(c, mod_w[l], mod_b[l])
        xs, _ = _block(xs, mod_s, lw, functools.partial(
            _mix_latent, lw=lw, nat_kv=cache_nat_kv[:, l], swa_kv=cache_swa_kv[:, l],
            rw_s=state_rwkv[:, l], hg_s=state_hgrn[:, l]))
    new_cache_nat_kv = jnp.stack(nat_out, axis=1)
    new_cache_swa_kv = jnp.stack(swa_out, axis=1)
    new_state_rwkv = jnp.stack(rw_out, axis=1)
    new_state_hgrn = jnp.stack(hg_out, axis=1)
    return (xp, xs, new_cache_nat_kv, new_cache_swa_kv, new_state_rwkv, new_state_hgrn)
```

```python
import functools

import numpy as np
import jax
import jax.numpy as jnp
from jax import lax
from jax.experimental import pallas as pl
from jax.experimental.pallas import tpu as pltpu

F32 = jnp.float32
BF16 = jnp.bfloat16
HIGHEST = lax.Precision.HIGHEST

D_MODEL = 1024
DEPTH = 4
GRID_W = 64
HEAD_DIM = 64
N_HEADS = 4
MIX_W = N_HEADS * HEAD_DIM
SWA_KV_HEADS = 2
LORA_RANK = 64
GATE_RANK = 128
RWKV_DECAY_SCALE = 0.6065306597126334
RWKV_LN_EPS = 64e-5
NAT_KH = 8
NAT_KW = 16
SWA_WINDOW = 128
SWA_BLOCK = 128
ROPE_BASE = 10000.0
FFN_HIDDEN = 4 * D_MODEL
NORM_EPS = 1e-6
MASK_VALUE = -1e30
N_MOD = 6
ATTN_SCALE = HEAD_DIM ** -0.5

RW_COLS = 3 * MIX_W + GATE_RANK + 4 * LORA_RANK
NAT_COLS = 3 * MIX_W
HG_COLS = 5 * MIX_W
SWA_COLS = MIX_W + 2 * SWA_KV_HEADS * HEAD_DIM
SLABS = (RW_COLS, NAT_COLS, HG_COLS, SWA_COLS)
D_IN = sum(SLABS)

LANES = 128
SUBLANES = 8
ROW_TILE = 256
MOD_ROWS = 8


def _sigmoid(x):
    return 1.0 / (1.0 + jnp.exp(-x))


def _rms_rows(x, g):
    ms = jnp.mean(x * x, axis=-1, keepdims=True)
    return x * lax.rsqrt(ms + NORM_EPS) * g


def _bdot(a, b):
    return jnp.dot(a.astype(BF16), b.astype(BF16), preferred_element_type=F32)


def _bdot_nt(a, b):
    return lax.dot_general(a.astype(BF16), b.astype(BF16), (((1,), (1,)), ((), ())),
                           preferred_element_type=F32)


def _seg_sum(x, seg):
    return jnp.dot(x, seg, preferred_element_type=F32, precision=HIGHEST)


def _params(*sem):
    return pltpu.CompilerParams(dimension_semantics=sem)


def _resident(shape, index_map):
    return pl.BlockSpec(shape, index_map, pipeline_mode=pl.Buffered(1))


def _mod_kernel(cond_ref, w_ref, b_ref, o_ref):
    cnd = cond_ref[...]
    s = cnd * _sigmoid(cnd)
    o_ref[...] = _bdot(s, w_ref[...]) + b_ref[...]


def _modulation_all(cond, mod_w, mod_b):
    tn = 1536
    n = N_MOD * D_MODEL
    return pl.pallas_call(
        _mod_kernel,
        grid=(DEPTH, n // tn),
        in_specs=[pl.BlockSpec((MOD_ROWS, D_MODEL), lambda l, j: (0, 0)),
                  pl.BlockSpec((None, D_MODEL, tn), lambda l, j: (l, 0, j)),
                  pl.BlockSpec((None, 1, tn), lambda l, j: (l, 0, j))],
        out_specs=pl.BlockSpec((None, MOD_ROWS, tn), lambda l, j: (l, 0, j)),
        out_shape=jax.ShapeDtypeStruct((DEPTH, MOD_ROWS, n), F32),
        compiler_params=_params("parallel", "parallel"),
        name="modulation",
    )(cond, mod_w, mod_b.reshape(DEPTH, 1, n))


def _mod_row(seq_len, first_row):
    if first_row == 0:
        return lambda i: 0
    return lambda i: first_row + (i * ROW_TILE) // seq_len


def _inproj_kernel(x_ref, g_ref, mod_ref, w_ref, oa_ref, ob_ref, oc_ref, od_ref, *, row_fn):
    row = row_fn(pl.program_id(0))
    shift = mod_ref[pl.ds(row, 1), 0:D_MODEL]
    scale = mod_ref[pl.ds(row, 1), D_MODEL:2 * D_MODEL]
    h = _rms_rows(x_ref[...], g_ref[0:1, :]) * (1.0 + scale) + shift
    hb = h.astype(BF16)
    c0 = 0
    for o_ref, width in zip((oa_ref, ob_ref, oc_ref, od_ref), SLABS):
        o_ref[...] = jnp.dot(hb, w_ref[:, c0:c0 + width], preferred_element_type=F32)
        c0 += width


def _in_projection(x, norm_g, mod, w_in_bf, layer, row_fn):
    n = x.shape[0]
    return pl.pallas_call(
        functools.partial(_inproj_kernel, row_fn=row_fn),
        grid=(n // ROW_TILE,),
        in_specs=[pl.BlockSpec((ROW_TILE, D_MODEL), lambda i: (i, 0)),
                  pl.BlockSpec((None, 4, D_MODEL), lambda i: (layer, 0, 0)),
                  pl.BlockSpec((None, MOD_ROWS, N_MOD * D_MODEL), lambda i: (layer, 0, 0)),
                  _resident((None, D_MODEL, D_IN), lambda i: (layer, 0, 0))],
        out_specs=[pl.BlockSpec((ROW_TILE, w), lambda i: (i, 0)) for w in SLABS],
        out_shape=[jax.ShapeDtypeStruct((n, w), F32) for w in SLABS],
        compiler_params=_params("parallel"),
        name="in_projection",
    )(x, norm_g, mod, w_in_bf)


def _outproj_ffn_kernel(x_ref, ya_ref, yb_ref, yc_ref, yd_ref, g_ref, mod_ref, wo_ref, w1_ref, w2_ref,
                        o_ref, *, row_fn):
    row = row_fn(pl.program_id(0))
    mod = lambda j: mod_ref[pl.ds(row, 1), j * D_MODEL:(j + 1) * D_MODEL]
    x = x_ref[...]
    acc = None
    for j, y_ref in enumerate((ya_ref, yb_ref, yc_ref, yd_ref)):
        part = jnp.dot(y_ref[...].astype(BF16), wo_ref[j * MIX_W:(j + 1) * MIX_W, :],
                       preferred_element_type=F32)
        acc = part if acc is None else acc + part
    x = x + mod(2) * _rms_rows(acc, g_ref[1:2, :])
    h = (_rms_rows(x, g_ref[2:3, :]) * (1.0 + mod(4)) + mod(3)).astype(BF16)
    f = None
    for j in range(FFN_HIDDEN // D_MODEL):
        u = jnp.dot(h, w1_ref[:, j * D_MODEL:(j + 1) * D_MODEL], preferred_element_type=F32)
        u = jnp.square(jnp.maximum(u, 0.0)).astype(BF16)
        part = jnp.dot(u, w2_ref[j * D_MODEL:(j + 1) * D_MODEL, :], preferred_element_type=F32)
        f = part if f is None else f + part
    o_ref[...] = x + mod(5) * _rms_rows(f, g_ref[3:4, :])


def _out_projection_ffn(x, ys, norm_g, mod, w_out_bf, w1_bf, w2_bf, layer, row_fn):
    n = x.shape[0]
    tile = lambda w: pl.BlockSpec((ROW_TILE, w), lambda i: (i, 0))
    return pl.pallas_call(
        functools.partial(_outproj_ffn_kernel, row_fn=row_fn),
        grid=(n // ROW_TILE,),
        in_specs=[tile(D_MODEL)] + [tile(MIX_W)] * 4 + [
            pl.BlockSpec((None, 4, D_MODEL), lambda i: (layer, 0, 0)),
            pl.BlockSpec((None, MOD_ROWS, N_MOD * D_MODEL), lambda i: (layer, 0, 0)),
            _resident((None, D_MODEL, D_MODEL), lambda i: (layer, 0, 0)),
            _resident((None, D_MODEL, FFN_HIDDEN), lambda i: (layer, 0, 0)),
            _resident((None, FFN_HIDDEN, D_MODEL), lambda i: (layer, 0, 0))],
        out_specs=tile(D_MODEL),
        out_shape=jax.ShapeDtypeStruct((n, D_MODEL), F32),
        compiler_params=_params("parallel"),
        name="out_projection_ffn",
    )(x, *ys, norm_g, mod, w_out_bf, w1_bf, w2_bf)


def _scan_kernel(*refs, n_rows, delta, steps, v_tiles, expand):
    if expand:
        rows_ref, v_ref, s0_ref, e_ref, y_ref, sfin_ref, s_scr, x_scr = refs
    else:
        rows_ref, v_ref, s0_ref, y_ref, sfin_ref, s_scr = refs
        x_scr = rows_ref
    tb = pl.program_id(1)

    @pl.when(tb == 0)
    def _():
        s_scr[...] = s0_ref[...]

    if expand:
        for j in range(n_rows):
            flat = rows_ref[j].reshape(steps * HEAD_DIM, rows_ref.shape[-1])
            x_scr[j] = jnp.dot(flat, e_ref[...], preferred_element_type=F32,
                               precision=HIGHEST).reshape(steps, HEAD_DIM, LANES)

    n_part = max(1, 4 // v_tiles)
    tiles = [slice(i * SUBLANES, (i + 1) * SUBLANES) for i in range(v_tiles)]
    if delta:
        i_nkk, i_w, i_kka, i_kv, i_r = range(5)
    else:
        i_w, i_kv, i_r = range(3)

    def tree(parts):
        while len(parts) > 1:
            parts = [a + b for a, b in zip(parts[0::2], parts[1::2])]
        return parts[0]

    def step(t, carry):
        row = lambda j, k: x_scr[j, t, k:k + 1, :]
        vt = [v_ref[t, tl, :] for tl in tiles]
        if delta:
            parts = [[None] * n_part for _ in tiles]
            for k in range(HEAD_DIM):
                nk = row(i_nkk, k)
                for i, tl in enumerate(tiles):
                    term = s_scr[k, tl, :] * nk
                    p = parts[i][k % n_part]
                    parts[i][k % n_part] = term if p is None else p + term
            sa = [tree(p) for p in parts]
        yparts = [[None] * n_part for _ in tiles]
        for k in range(HEAD_DIM):
            wk, kvk, rk = row(i_w, k), row(i_kv, k), row(i_r, k)
            if delta:
                kkak = row(i_kka, k)
            for i, tl in enumerate(tiles):
                s = s_scr[k, tl, :] * wk + vt[i] * kvk
                if delta:
                    s = s + sa[i] * kkak
                s_scr[k, tl, :] = s
                term = s * rk
                p = yparts[i][k % n_part]
                yparts[i][k % n_part] = term if p is None else p + term
        for i, tl in enumerate(tiles):
            y_ref[t, tl, :] = tree(yparts[i])
        return carry

    lax.fori_loop(0, steps, step, 0)

    @pl.when(tb == pl.num_programs(1) - 1)
    def _():
        sfin_ref[...] = s_scr[...]


def _scan(rows, v, s0, *, delta, steps):
    groups, n_rows, t_len, _, n_chain = rows.shape
    vrows = v.shape[2]
    v_tiles = vrows // SUBLANES
    expand = n_chain != LANES
    in_specs = [pl.BlockSpec((None, n_rows, steps, HEAD_DIM, n_chain), lambda g, t: (g, 0, t, 0, 0)),
                pl.BlockSpec((None, steps, vrows, LANES), lambda g, t: (g, t, 0, 0)),
                pl.BlockSpec((None, HEAD_DIM, vrows, LANES), lambda g, t: (g, 0, 0, 0))]
    args = [rows, v, s0]
    scratch = [pltpu.VMEM((HEAD_DIM, vrows, LANES), F32)]
    if expand:
        lane = np.arange(LANES)
        e = (lane[None, :] % n_chain == np.arange(n_chain)[:, None]).astype(np.float32)
        in_specs.append(pl.BlockSpec((n_chain, LANES), lambda g, t: (0, 0)))
        args.append(jnp.asarray(e))
        scratch.append(pltpu.VMEM((n_rows, steps, HEAD_DIM, LANES), F32))
    return pl.pallas_call(
        functools.partial(_scan_kernel, n_rows=n_rows, delta=delta, steps=steps, v_tiles=v_tiles,
                          expand=expand),
        grid=(groups, t_len // steps),
        in_specs=in_specs,
        out_specs=[pl.BlockSpec((None, steps, vrows, LANES), lambda g, t: (g, t, 0, 0)),
                   pl.BlockSpec((None, HEAD_DIM, vrows, LANES), lambda g, t: (g, 0, 0, 0))],
        out_shape=[jax.ShapeDtypeStruct((groups, t_len, vrows, LANES), F32),
                   jax.ShapeDtypeStruct((groups, HEAD_DIM, vrows, LANES), F32)],
        scratch_shapes=scratch,
        compiler_params=_params("parallel", "arbitrary"),
        name="rwkv_scan" if delta else "hgrn_scan",
    )(*args)


def _to_scan_layout(a, n_seq, t_len, per_dir_groups):
    n_arr = a.shape[1]
    a = a.reshape(2, n_arr, n_seq, t_len, N_HEADS, HEAD_DIM)
    a = jnp.stack([a[0], a[1][:, :, ::-1]])
    if per_dir_groups:
        r = a.transpose(0, 1, 3, 5, 2, 4).reshape(2, n_arr, t_len, HEAD_DIM, n_seq * N_HEADS)
        return r[:, :n_arr - 1], r[:, n_arr - 1]
    n_chain = 2 * n_seq * N_HEADS
    vg = LANES // n_chain
    rows = a[:, :n_arr - 1].transpose(1, 3, 5, 0, 2, 4).reshape(1, n_arr - 1, t_len, HEAD_DIM, n_chain)
    v = a[:, n_arr - 1].reshape(2, n_seq, t_len, N_HEADS, vg, HEAD_DIM // vg)
    v = v.transpose(2, 5, 4, 0, 1, 3).reshape(1, t_len, HEAD_DIM // vg, LANES)
    return rows, v


def _from_scan_layout(y, n_seq, t_len, per_dir_groups):
    if per_dir_groups:
        y = y.reshape(2, t_len, HEAD_DIM, n_seq, N_HEADS)
        y = y[0] + y[1][::-1]
        return y.transpose(2, 0, 3, 1).reshape(n_seq * t_len, MIX_W)
    n_chain = 2 * n_seq * N_HEADS
    vg = LANES // n_chain
    y = y.reshape(t_len, HEAD_DIM // vg, vg, 2, n_seq, N_HEADS)
    y = y[:, :, :, 0] + y[::-1, :, :, 1]
    return y.transpose(3, 0, 4, 2, 1).reshape(n_seq * t_len, MIX_W)


def _state_to_scan(s, key_last):
    n_seq = s.shape[0]
    vg = LANES // (2 * n_seq * N_HEADS)
    if key_last:
        s = s.reshape(n_seq, 2, N_HEADS, vg, HEAD_DIM // vg, HEAD_DIM).transpose(5, 4, 3, 1, 0, 2)
    else:
        s = s.reshape(n_seq, 2, N_HEADS, HEAD_DIM, vg, HEAD_DIM // vg).transpose(3, 5, 4, 1, 0, 2)
    return s.reshape(1, HEAD_DIM, HEAD_DIM // vg, LANES)


def _state_from_scan(s, n_seq, key_last):
    s = s.reshape(2, HEAD_DIM, HEAD_DIM, n_seq, N_HEADS)
    return s.transpose(3, 0, 4, 2, 1) if key_last else s.transpose(3, 0, 4, 1, 2)


def _shifted(x, d, t_len):
    row = lax.broadcasted_iota(jnp.int32, (t_len, 1), 0)
    prev_f = jnp.where(row == 0, 0.0, pltpu.roll(x, 1, 0))
    prev_b = jnp.where(row == t_len - 1, 0.0, pltpu.roll(x, t_len - 1, 0))
    return jnp.where(d == 0, prev_f, prev_b)


def _rwkv_pre_kernel(rkv_ref, lora_ref, mu_rkv_ref, mu_lo_ref, wl_ref, b0_ref, kk_ref, ka_ref, rk_ref,
                     seg_ref, scan_ref, bonus_ref, *, t_len):
    d = pl.program_id(1)
    x = rkv_ref[...]
    x = x + (_shifted(x, d, t_len) - x) * mu_rkv_ref[...]
    r, k, v = x[:, 0:MIX_W], x[:, MIX_W:2 * MIX_W], x[:, 2 * MIX_W:3 * MIX_W]
    lo = lora_ref[...]
    lo = lo + (_shifted(lo, d, t_len) - lo) * mu_lo_ref[...]
    lane = lax.broadcasted_iota(jnp.int32, lo.shape, 1)
    lo = jnp.where(lane < LORA_RANK, jnp.tanh(lo), lo)
    pre = _bdot(lo, wl_ref[...]) + b0_ref[...]
    w = jnp.exp(-RWKV_DECAY_SCALE * _sigmoid(pre[:, 0:MIX_W]))
    a = _sigmoid(pre[:, MIX_W:2 * MIX_W])
    seg = seg_ref[...]
    kk = k * kk_ref[...]
    kk = kk / jnp.maximum(jnp.sqrt(_seg_sum(kk * kk, seg)), 1e-12)
    k = k * (1.0 + (a - 1.0) * ka_ref[...])
    scan_ref[0] = -kk
    scan_ref[1] = w
    scan_ref[2] = kk * a
    scan_ref[3] = k
    scan_ref[4] = r
    scan_ref[5] = v
    bonus_ref[...] = _seg_sum(r * k * rk_ref[...], seg) * v


def _rwkv_pre(p_rw, lw, seg, n_seq, t_len, seq0):
    n = n_seq * t_len
    lora_blk = (3 * MIX_W + GATE_RANK) // (2 * LORA_RANK)
    vec = lambda w: pl.BlockSpec((None, 1, w), lambda i, d: (d, 0, 0))
    shared = lambda w: pl.BlockSpec((1, w), lambda i, d: (0, 0))
    return pl.pallas_call(
        functools.partial(_rwkv_pre_kernel, t_len=t_len),
        grid=(n_seq, 2),
        in_specs=[pl.BlockSpec((t_len, 3 * MIX_W), lambda i, d: (seq0 + i, 0)),
                  pl.BlockSpec((t_len, 2 * LORA_RANK), lambda i, d: (seq0 + i, lora_blk + d)),
                  vec(3 * MIX_W), vec(2 * LORA_RANK),
                  pl.BlockSpec((None, 2 * LORA_RANK, 2 * MIX_W), lambda i, d: (d, 0, 0)),
                  vec(2 * MIX_W), shared(MIX_W), shared(MIX_W), shared(MIX_W),
                  pl.BlockSpec((MIX_W, MIX_W), lambda i, d: (0, 0))],
        out_specs=[pl.BlockSpec((None, 6, t_len, MIX_W), lambda i, d: (d, 0, i, 0)),
                   pl.BlockSpec((None, t_len, MIX_W), lambda i, d: (d, i, 0))],
        out_shape=[jax.ShapeDtypeStruct((2, 6, n, MIX_W), F32),
                   jax.ShapeDtypeStruct((2, n, MIX_W), F32)],
        compiler_params=_params("parallel", "parallel"),
        name="rwkv_pre",
    )(p_rw, p_rw, lw["mu_rkv"], lw["mu_lo"], lw["w_lora"], lw["b_lora"], lw["rw_kk"], lw["rw_ka"],
      lw["rw_rk"], seg)


def _rwkv_post_kernel(y_ref, bonus_ref, gh_ref, g2_ref, lnw_ref, lnb_ref, seg_ref, o_ref):
    seg = seg_ref[...]
    y = y_ref[...]
    mu = _seg_sum(y, seg) * (1.0 / HEAD_DIM)
    yc = y - mu
    var = _seg_sum(yc * yc, seg) * (1.0 / HEAD_DIM)
    y = yc * lax.rsqrt(var + RWKV_LN_EPS) * lnw_ref[...] + lnb_ref[...]
    y = y + bonus_ref[0] + bonus_ref[1]
    o_ref[...] = y * _bdot(_sigmoid(gh_ref[...]), g2_ref[...])


def _rwkv_post(y, bonus, p_rw, lw, seg, tile0):
    n = y.shape[0]
    gh_blk = 3 * MIX_W // GATE_RANK
    shared = lambda w: pl.BlockSpec((1, w), lambda i: (0, 0))
    return pl.pallas_call(
        _rwkv_post_kernel,
        grid=(n // ROW_TILE,),
        in_specs=[pl.BlockSpec((ROW_TILE, MIX_W), lambda i: (i, 0)),
                  pl.BlockSpec((2, ROW_TILE, MIX_W), lambda i: (0, i, 0)),
                  pl.BlockSpec((ROW_TILE, GATE_RANK), lambda i: (tile0 + i, gh_blk)),
                  pl.BlockSpec((GATE_RANK, MIX_W), lambda i: (0, 0)),
                  shared(MIX_W), shared(MIX_W),
                  pl.BlockSpec((MIX_W, MIX_W), lambda i: (0, 0))],
        out_specs=pl.BlockSpec((ROW_TILE, MIX_W), lambda i: (i, 0)),
        out_shape=jax.ShapeDtypeStruct((n, MIX_W), F32),
        compiler_params=_params("parallel"),
        name="rwkv_post",
    )(y, bonus, p_rw, lw["rw_g2"], lw["rw_lnx_w"], lw["rw_lnx_b"], seg)


def _hgrn_pre_kernel(qv_ref, f_ref, lb_ref, scan_ref):
    q = qv_ref[:, 0:MIX_W]
    f_raw = f_ref[...]
    lb = lb_ref[...]
    scan_ref[0] = lb + (1.0 - lb) * _sigmoid(f_raw)
    scan_ref[1] = (1.0 - lb) * _sigmoid(-f_raw)
    scan_ref[2] = q * _sigmoid(q)
    scan_ref[3] = qv_ref[:, MIX_W:2 * MIX_W]


def _hgrn_pre(p_hg, lb, n_tok, tile0):
    return pl.pallas_call(
        _hgrn_pre_kernel,
        grid=(n_tok // ROW_TILE, 2),
        in_specs=[pl.BlockSpec((ROW_TILE, 2 * MIX_W), lambda i, d: (tile0 + i, 0)),
                  pl.BlockSpec((ROW_TILE, MIX_W), lambda i, d: (tile0 + i, 3 + d)),
                  pl.BlockSpec((None, 1, MIX_W), lambda i, d: (d, 0, 0))],
        out_specs=pl.BlockSpec((None, 4, ROW_TILE, MIX_W), lambda i, d: (d, 0, i, 0)),
        out_shape=jax.ShapeDtypeStruct((2, 4, n_tok, MIX_W), F32),
        compiler_params=_params("parallel", "parallel"),
        name="hgrn_pre",
    )(p_hg, p_hg, lb)


def _hgrn_post_kernel(o_ref, g_ref, nw_ref, seg_ref, out_ref):
    o = o_ref[...]
    ms = _seg_sum(o * o, seg_ref[...]) * (1.0 / HEAD_DIM)
    out_ref[...] = o * lax.rsqrt(ms + NORM_EPS) * nw_ref[...] * _sigmoid(g_ref[...])


def _hgrn_post(o, p_hg, hg_norm, seg, tile0):
    n = o.shape[0]
    return pl.pallas_call(
        _hgrn_post_kernel,
        grid=(n // ROW_TILE,),
        in_specs=[pl.BlockSpec((ROW_TILE, MIX_W), lambda i: (i, 0)),
                  pl.BlockSpec((ROW_TILE, MIX_W), lambda i: (tile0 + i, 2)),
                  pl.BlockSpec((1, MIX_W), lambda i: (0, 0)),
                  pl.BlockSpec((MIX_W, MIX_W), lambda i: (0, 0))],
        out_specs=pl.BlockSpec((ROW_TILE, MIX_W), lambda i: (i, 0)),
        out_shape=jax.ShapeDtypeStruct((n, MIX_W), F32),
        compiler_params=_params("parallel"),
        name="hgrn_post",
    )(o, p_hg, hg_norm, seg)


def _head(x, h):
    return x[:, h * HEAD_DIM:(h + 1) * HEAD_DIM]


def _softmax_av(scores, values, sink=None):
    m = None
    for s in scores:
        mi = jnp.max(s, axis=-1, keepdims=True)
        m = mi if m is None else jnp.maximum(m, mi)
    if sink is not None:
        m = jnp.maximum(m, sink)
    den = None if sink is None else jnp.exp(sink - m)
    out = None
    for s, v in zip(scores, values):
        e = jnp.exp(s - m)
        di = jnp.sum(e, axis=-1, keepdims=True)
        den = di if den is None else den + di
        oi = _bdot(e, v)
        out = oi if out is None else out + oi
    return out / den


def _ctx_attn_kernel(sink_ref, nat_ref, swa_ref, yb_ref, yd_ref, *, t_len):
    q_all = nat_ref[:, 0:MIX_W] * ATTN_SCALE
    k_all = nat_ref[:, MIX_W:2 * MIX_W]
    v_all = nat_ref[:, 2 * MIX_W:3 * MIX_W]
    outs = []
    for h in range(N_HEADS):
        outs.append(_softmax_av([_bdot_nt(_head(q_all, h), _head(k_all, h))], [_head(v_all, h)]))
    yb_ref[...] = jnp.concatenate(outs, axis=-1)
    group = N_HEADS // SWA_KV_HEADS
    kv_w = SWA_KV_HEADS * HEAD_DIM
    q_all = swa_ref[:, 0:MIX_W] * ATTN_SCALE
    k_all = swa_ref[:, MIX_W:MIX_W + kv_w]
    v_all = swa_ref[:, MIX_W + kv_w:MIX_W + 2 * kv_w]
    outs = []
    for kvh in range(SWA_KV_HEADS):
        for g in range(group):
            h = kvh * group + g
            outs.append(_softmax_av([_bdot_nt(_head(q_all, h), _head(k_all, kvh))], [_head(v_all, kvh)],
                                    sink=sink_ref[h]))
    yd_ref[...] = jnp.concatenate(outs, axis=-1)


def _ctx_attention(p_nat, p_swa, sink, n_seq, t_len):
    n = n_seq * t_len
    return pl.pallas_call(
        functools.partial(_ctx_attn_kernel, t_len=t_len),
        grid=(n_seq,),
        in_specs=[pl.BlockSpec(memory_space=pltpu.SMEM),
                  pl.BlockSpec((t_len, NAT_COLS), lambda i: (i, 0)),
                  pl.BlockSpec((t_len, SWA_COLS), lambda i: (i, 0))],
        out_specs=[pl.BlockSpec((t_len, MIX_W), lambda i: (i, 0))] * 2,
        out_shape=[jax.ShapeDtypeStruct((n, MIX_W), F32)] * 2,
        compiler_params=_params("parallel"),
        name="ctx_attention",
    )(sink, p_nat, p_swa)


def _nat_latent_kernel(p_ref, ckv_ref, bias_ref, o_ref, *, rows):
    kh = min(NAT_KH, rows)
    n_loc = kh * GRID_W
    ck_all = ckv_ref[0]
    cv_all = ckv_ref[1]
    for r in range(rows):
        rs = min(max(r - kh // 2, 0), rows - kh)
        roff0 = rs - r + NAT_KH - 1
        q_all = p_ref[r * GRID_W:(r + 1) * GRID_W, 0:MIX_W] * ATTN_SCALE
        k_all = p_ref[rs * GRID_W:rs * GRID_W + n_loc, MIX_W:2 * MIX_W]
        v_all = p_ref[rs * GRID_W:rs * GRID_W + n_loc, 2 * MIX_W:3 * MIX_W]
        outs = []
        for h in range(N_HEADS):
            q = _head(q_all, h)
            bias = jnp.concatenate([bias_ref[h, roff0 + 2 * j] for j in range(kh // 2)], axis=-1)
            s_loc = _bdot_nt(q, _head(k_all, h)) + bias
            outs.append(_softmax_av([s_loc, _bdot_nt(q, _head(ck_all, h))],
                                    [_head(v_all, h), _head(cv_all, h)]))
        o_ref[r * GRID_W:(r + 1) * GRID_W, :] = jnp.concatenate(outs, axis=-1)


def _nat_bias_table(rpb):
    q = np.arange(GRID_W)[:, None]
    kc = np.arange(GRID_W)[None, :]
    win = np.clip(q - NAT_KW // 2, 0, GRID_W - NAT_KW)
    ok = (kc >= win) & (kc < win + NAT_KW)
    coff = np.clip(kc - q, -(NAT_KW - 1), NAT_KW - 1) + NAT_KW - 1
    t = jnp.where(ok[None, None], rpb[:, :, coff], MASK_VALUE)
    return jnp.concatenate([t[:, :-1], t[:, 1:]], axis=-1)


def _nat_latent(p_nat, cache, bias, layer, n_seq, t_len, tile0):
    rows = t_len // GRID_W
    return pl.pallas_call(
        functools.partial(_nat_latent_kernel, rows=rows),
        grid=(n_seq,),
        in_specs=[pl.BlockSpec((t_len, NAT_COLS), lambda i: (tile0 + i, 0)),
                  pl.BlockSpec((None, None, 2, cache.shape[3], MIX_W), lambda i: (i, layer, 0, 0, 0)),
                  pl.BlockSpec(bias.shape, lambda i: (0, 0, 0, 0))],
        out_specs=pl.BlockSpec((t_len, MIX_W), lambda i: (i, 0)),
        out_shape=jax.ShapeDtypeStruct((n_seq * t_len, MIX_W), F32),
        compiler_params=_params("parallel"),
        name="nat_latent",
    )(p_nat, cache, bias)


def _rope(x, cos, sin_signed):
    width = x.shape[-1]
    lane = lax.broadcasted_iota(jnp.int32, x.shape, 1)
    half = HEAD_DIM // 4
    partner = jnp.where(lane % (2 * half) < half, pltpu.roll(x, width - half, 1), pltpu.roll(x, half, 1))
    return x * cos + partner * sin_signed


def _swa_latent_kernel(sink_ref, p_ref, ckv_ref, cos_ref, sin_ref, o_ref, *, t_len):
    hd = HEAD_DIM
    kv_w = SWA_KV_HEADS * hd
    group = N_HEADS // SWA_KV_HEADS
    blk = SWA_BLOCK
    nb = t_len // blk
    q = _rope(p_ref[:, 0:MIX_W], cos_ref[...], sin_ref[...]) * ATTN_SCALE
    k = _rope(p_ref[:, MIX_W:MIX_W + kv_w], cos_ref[:, 0:kv_w], sin_ref[:, 0:kv_w])
    v = p_ref[:, MIX_W + kv_w:MIX_W + 2 * kv_w]
    row = lax.broadcasted_iota(jnp.int32, (group * blk, 1), 0)
    ck_all = ckv_ref[0]
    cv_all = ckv_ref[1]
    for n in range(nb):
        lo = max(n - 1, 0) * blk
        hi = min(n + 2, nb) * blk
        qpos = n * blk + row % blk
        kpos = lo + lax.broadcasted_iota(jnp.int32, (group * blk, hi - lo), 1)
        ok = jnp.abs(kpos - qpos) <= SWA_WINDOW
        heads = [None] * N_HEADS
        for kvh in range(SWA_KV_HEADS):
            qs = jnp.concatenate([_head(q[n * blk:(n + 1) * blk], kvh * group + g) for g in range(group)],
                                 axis=0)
            sink = jnp.concatenate([jnp.full((blk, 1), sink_ref[kvh * group + g], F32)
                                    for g in range(group)], axis=0)
            s_loc = jnp.where(ok, _bdot_nt(qs, _head(k[lo:hi], kvh)), MASK_VALUE)
            s_ctx = _bdot_nt(qs, _head(ck_all, kvh))
            o = _softmax_av([s_loc, s_ctx], [_head(v[lo:hi], kvh), _head(cv_all, kvh)], sink=sink)
            for g in range(group):
                heads[kvh * group + g] = o[g * blk:(g + 1) * blk]
        o_ref[n * blk:(n + 1) * blk, :] = jnp.concatenate(heads, axis=-1)


def _rope_tables(t_len):
    t = jnp.arange(t_len)
    half = HEAD_DIM // 2
    nf = half // 2
    inv = 1.0 / (ROPE_BASE ** (jnp.arange(nf, dtype=F32) / nf))
    cols = []
    for pos in ((t // GRID_W).astype(F32), (t % GRID_W).astype(F32)):
        ang = pos[:, None] * inv[None, :]
        cols.append((jnp.cos(ang), jnp.sin(ang)))
    cos = jnp.concatenate([cols[0][0], cols[0][0], cols[1][0], cols[1][0]], axis=-1)
    sin = jnp.concatenate([-cols[0][1], cols[0][1], -cols[1][1], cols[1][1]], axis=-1)
    return jnp.tile(cos, (1, N_HEADS)), jnp.tile(sin, (1, N_HEADS))


def _swa_latent(p_swa, cache, sink, rope, layer, n_seq, t_len, tile0):
    kv_w = SWA_KV_HEADS * HEAD_DIM
    return pl.pallas_call(
        functools.partial(_swa_latent_kernel, t_len=t_len),
        grid=(n_seq,),
        in_specs=[pl.BlockSpec(memory_space=pltpu.SMEM),
                  pl.BlockSpec((t_len, SWA_COLS), lambda i: (tile0 + i, 0)),
                  pl.BlockSpec((None, None, 2, cache.shape[3], kv_w), lambda i: (i, layer, 0, 0, 0)),
                  pl.BlockSpec((t_len, MIX_W), lambda i: (0, 0)),
                  pl.BlockSpec((t_len, MIX_W), lambda i: (0, 0))],
        out_specs=pl.BlockSpec((t_len, MIX_W), lambda i: (i, 0)),
        out_shape=jax.ShapeDtypeStruct((n_seq * t_len, MIX_W), F32),
        compiler_params=_params("parallel"),
        name="swa_latent",
    )(sink, p_swa, cache, rope[0], rope[1])


def _recurrent_mixers(p_rw, p_hg, lw, seg, n_seq, t_len, tile0, states):
    ctx = states is None
    seq0 = tile0 * ROW_TILE // t_len
    n_tok = n_seq * t_len
    steps = 32

    scan_in, bonus = _rwkv_pre(p_rw, lw, seg, n_seq, t_len, seq0)
    rows, v = _to_scan_layout(scan_in, n_seq, t_len, ctx)
    s0 = jnp.zeros((2, HEAD_DIM, HEAD_DIM, LANES), F32) if ctx else _state_to_scan(states[0], True)
    y, s_rw = _scan(rows, v, s0, delta=True, steps=steps)
    ya = _rwkv_post(_from_scan_layout(y, n_seq, t_len, ctx), bonus, p_rw, lw, seg, tile0)

    scan_in = _hgrn_pre(p_hg, lw["hg_lb"], n_tok, tile0)
    rows, v = _to_scan_layout(scan_in, n_seq, t_len, ctx)
    s0 = jnp.zeros((2, HEAD_DIM, HEAD_DIM, LANES), F32) if ctx else _state_to_scan(states[1], False)
    o, s_hg = _scan(rows, v, s0, delta=False, steps=steps)
    yc = _hgrn_post(_from_scan_layout(o, n_seq, t_len, ctx), p_hg, lw["hg_norm"], seg, tile0)
    return ya, yc, s_rw, s_hg


def kernel(x_prompt, x_sample, cache_nat_kv, cache_swa_kv, state_rwkv, state_hgrn, c, c_ctx, norm_g, mod_w, mod_b, w_in, w_out, rw_mu_rkv, rw_mu_lora, rw_w0, rw_w2, rw_a0, rw_a2, rw_g2, rw_kk, rw_ka, rw_rk, rw_lnx_w, rw_lnx_b, nat_rpb, hg_lb_logits, hg_norm, swa_sink, ffn_w1, ffn_w2):
    n_ctx, t_ctx, _ = x_prompt.shape
    n_lat, t_lat, _ = x_sample.shape
    past = cache_nat_kv.shape[3]

    cond = jnp.zeros((MOD_ROWS, D_MODEL), F32).at[0].set(c_ctx).at[1:1 + n_lat].set(c)
    mod = _modulation_all(cond, mod_w, mod_b)

    w_in_bf, w_out_bf = w_in.astype(BF16), w_out.astype(BF16)
    w1_bf, w2_bf = ffn_w1.astype(BF16), ffn_w2.astype(BF16)

    lb_sm = jax.nn.softmax(hg_lb_logits.astype(F32), axis=1)
    hg_lb = jnp.cumsum(lb_sm, axis=1) - lb_sm[:, :1]

    head_of = np.arange(MIX_W) // HEAD_DIM
    seg = jnp.asarray((head_of[:, None] == head_of[None, :]).astype(np.float32))
    rope = _rope_tables(t_lat)
    zeros_lora = jnp.zeros((LORA_RANK, MIX_W), F32)
    cache_nat = cache_nat_kv.reshape(n_lat, DEPTH, 2, past, MIX_W)
    cache_swa = cache_swa_kv.reshape(n_lat, DEPTH, 2, past, SWA_KV_HEADS * HEAD_DIM)

    xp = x_prompt.reshape(n_ctx * t_ctx, D_MODEL)
    xs = x_sample.reshape(n_lat * t_lat, D_MODEL)
    ctx_row, lat_row = _mod_row(t_ctx, 0), _mod_row(t_lat, 1)
    nat_out, swa_out, rw_out, hg_out = [], [], [], []
    for l in range(DEPTH):
        lw = {
            "mu_rkv": rw_mu_rkv[l].reshape(2, 1, 3 * MIX_W),
            "mu_lo": rw_mu_lora[l].reshape(2, 1, 2 * LORA_RANK),
            "w_lora": jnp.stack([jnp.concatenate(
                [jnp.concatenate([rw_w2[l, d], zeros_lora], axis=1),
                 jnp.concatenate([zeros_lora, rw_a2[l, d]], axis=1)], axis=0) for d in range(2)]),
            "b_lora": jnp.concatenate([rw_w0[l], rw_a0[l]], axis=-1).reshape(2, 1, 2 * MIX_W),
            "rw_kk": rw_kk[l].reshape(1, MIX_W), "rw_ka": rw_ka[l].reshape(1, MIX_W),
            "rw_rk": rw_rk[l].reshape(1, MIX_W), "rw_g2": rw_g2[l],
            "rw_lnx_w": rw_lnx_w[l].reshape(1, MIX_W), "rw_lnx_b": rw_lnx_b[l].reshape(1, MIX_W),
            "hg_lb": hg_lb[:, l].reshape(2, 1, MIX_W), "hg_norm": hg_norm[l].reshape(1, MIX_W),
        }
        sink = swa_sink[l]

        p_rw, p_nat, p_hg, p_swa = _in_projection(xp, norm_g, mod, w_in_bf, l, ctx_row)
        ya, yc, s_rw, s_hg = _recurrent_mixers(p_rw, p_hg, lw, seg, n_ctx, t_ctx, 0, None)
        yb, yd = _ctx_attention(p_nat, p_swa, sink, n_ctx, t_ctx)
        xp = _out_projection_ffn(xp, (ya, yb, yc, yd), norm_g, mod, w_out_bf, w1_bf, w2_bf, l, ctx_row)
        nat_out.append(p_nat[:, MIX_W:].reshape(n_ctx, t_ctx, 2, N_HEADS, HEAD_DIM).transpose(0, 2, 1, 3, 4))
        swa_out.append(p_swa[:, MIX_W:].reshape(n_ctx, t_ctx, 2, SWA_KV_HEADS, HEAD_DIM).transpose(0, 2, 1, 3, 4))
        rw_out.append(_state_from_scan(s_rw, n_ctx, True))
        hg_out.append(_state_from_scan(s_hg, n_ctx, False))

        p_rw, p_nat, p_hg, p_swa = _in_projection(xs, norm_g, mod, w_in_bf, l, lat_row)
        ya, yc, _, _ = _recurrent_mixers(p_rw, p_hg, lw, seg, n_lat, t_lat, 0,
                                         (state_rwkv[:, l], state_hgrn[:, l]))
        yb = _nat_latent(p_nat, cache_nat, _nat_bias_table(nat_rpb[l]), l, n_lat, t_lat, 0)
        yd = _swa_latent(p_swa, cache_swa, sink, rope, l, n_lat, t_lat, 0)
        xs = _out_projection_ffn(xs, (ya, yb, yc, yd), norm_g, mod, w_out_bf, w1_bf, w2_bf, l, lat_row)

    return (xp.reshape(x_prompt.shape), xs.reshape(x_sample.shape),
            jnp.stack(nat_out, axis=1), jnp.stack(swa_out, axis=1),
            jnp.stack(rw_out, axis=1), jnp.stack(hg_out, axis=1))
```

```python
import functools

import numpy as np
import jax
import jax.numpy as jnp
from jax import lax
from jax.experimental import pallas as pl
from jax.experimental.pallas import tpu as pltpu

F32 = jnp.float32
BF16 = jnp.bfloat16
HIGHEST = lax.Precision.HIGHEST

D_MODEL = 1024
DEPTH = 4
GRID_W = 64
HEAD_DIM = 64
N_HEADS = 4
MIX_W = N_HEADS * HEAD_DIM
SWA_KV_HEADS = 2
LORA_RANK = 64
GATE_RANK = 128
RWKV_DECAY_SCALE = 0.6065306597126334
RWKV_LN_EPS = 64e-5
NAT_KH = 8
NAT_KW = 16
SWA_WINDOW = 128
SWA_BLOCK = 128
ROPE_BASE = 10000.0
FFN_HIDDEN = 4 * D_MODEL
NORM_EPS = 1e-6
MASK_VALUE = -1e30
N_MOD = 6
ATTN_SCALE = HEAD_DIM ** -0.5

RW_COLS = 3 * MIX_W + GATE_RANK + 4 * LORA_RANK
NAT_COLS = 3 * MIX_W
HG_COLS = 5 * MIX_W
SWA_COLS = MIX_W + 2 * SWA_KV_HEADS * HEAD_DIM
SLABS = (RW_COLS, NAT_COLS, HG_COLS, SWA_COLS)
D_IN = sum(SLABS)

LANES = 128
SUBLANES = 8
ROW_TILE = 256
MOD_ROWS = 8


def _sigmoid(x):
    return 1.0 / (1.0 + jnp.exp(-x))


def _rms_rows(x, g):
    ms = jnp.mean(x * x, axis=-1, keepdims=True)
    return x * lax.rsqrt(ms + NORM_EPS) * g


def _bdot(a, b):
    return jnp.dot(a.astype(BF16), b.astype(BF16), preferred_element_type=F32)


def _bdot_nt(a, b):
    return lax.dot_general(a.astype(BF16), b.astype(BF16), (((1,), (1,)), ((), ())),
                           preferred_element_type=F32)


def _seg_sum(x, seg):
    return jnp.dot(x, seg, preferred_element_type=F32, precision=HIGHEST)


def _params(*sem):
    return pltpu.CompilerParams(dimension_semantics=sem)


def _resident(shape, index_map):
    return pl.BlockSpec(shape, index_map, pipeline_mode=pl.Buffered(1))


def _mod_kernel(cond_ref, w_ref, b_ref, o_ref):
    cnd = cond_ref[...]
    s = cnd * _sigmoid(cnd)
    o_ref[...] = _bdot(s, w_ref[...]) + b_ref[...]


def _modulation_all(cond, mod_w, mod_b):
    tn = 1536
    n = N_MOD * D_MODEL
    return pl.pallas_call(
        _mod_kernel,
        grid=(DEPTH, n // tn),
        in_specs=[pl.BlockSpec((MOD_ROWS, D_MODEL), lambda l, j: (0, 0)),
                  pl.BlockSpec((None, D_MODEL, tn), lambda l, j: (l, 0, j)),
                  pl.BlockSpec((None, 1, tn), lambda l, j: (l, 0, j))],
        out_specs=pl.BlockSpec((None, MOD_ROWS, tn), lambda l, j: (l, 0, j)),
        out_shape=jax.ShapeDtypeStruct((DEPTH, MOD_ROWS, n), F32),
        compiler_params=_params("parallel", "parallel"),
        name="modulation",
    )(cond, mod_w, mod_b.reshape(DEPTH, 1, n))


def _mod_row(seq_len, first_row):
    if first_row == 0:
        return lambda i: 0
    return lambda i: first_row + (i * ROW_TILE) // seq_len


def _inproj_kernel(x_ref, g_ref, mod_ref, wa_ref, wb_ref, wc_ref, wd_ref, oa_ref, ob_ref, oc_ref, od_ref,
                   *, row_fn, feature_major):
    row = row_fn(pl.program_id(0))
    shift = mod_ref[pl.ds(row, 1), 0:D_MODEL]
    scale = mod_ref[pl.ds(row, 1), D_MODEL:2 * D_MODEL]
    h = _rms_rows(x_ref[...], g_ref[0:1, :]) * (1.0 + scale) + shift
    hb = h.astype(BF16)
    for w_ref, o_ref, fm in zip((wa_ref, wb_ref, wc_ref, wd_ref), (oa_ref, ob_ref, oc_ref, od_ref),
                                feature_major):
        if fm:
            o_ref[...] = lax.dot_general(w_ref[...], hb, (((1,), (1,)), ((), ())),
                                         preferred_element_type=F32)
        else:
            o_ref[...] = jnp.dot(hb, w_ref[...], preferred_element_type=F32)


def _in_projection(x, norm_g, mod, w_slabs, layer, row_fn, feature_major):
    n = x.shape[0]
    out_specs, out_shape = [], []
    for width, fm in zip(SLABS, feature_major):
        if fm:
            out_specs.append(pl.BlockSpec((width, ROW_TILE), lambda i: (0, i)))
            out_shape.append(jax.ShapeDtypeStruct((width, n), F32))
        else:
            out_specs.append(pl.BlockSpec((ROW_TILE, width), lambda i: (i, 0)))
            out_shape.append(jax.ShapeDtypeStruct((n, width), F32))
    return pl.pallas_call(
        functools.partial(_inproj_kernel, row_fn=row_fn, feature_major=feature_major),
        grid=(n // ROW_TILE,),
        in_specs=[pl.BlockSpec((ROW_TILE, D_MODEL), lambda i: (i, 0)),
                  pl.BlockSpec((None, 4, D_MODEL), lambda i: (layer, 0, 0)),
                  pl.BlockSpec((None, MOD_ROWS, N_MOD * D_MODEL), lambda i: (layer, 0, 0))]
        + [_resident((None,) + w.shape[1:], lambda i: (layer, 0, 0)) for w in w_slabs],
        out_specs=out_specs,
        out_shape=out_shape,
        compiler_params=_params("parallel"),
        name="in_projection",
    )(x, norm_g, mod, *w_slabs)


def _outproj_ffn_kernel(x_ref, ya_ref, yb_ref, yc_ref, yd_ref, g_ref, mod_ref, wo_ref, w1_ref, w2_ref,
                        o_ref, *, row_fn):
    row = row_fn(pl.program_id(0))
    mod = lambda j: mod_ref[pl.ds(row, 1), j * D_MODEL:(j + 1) * D_MODEL]
    x = x_ref[...]
    acc = None
    for j, y_ref in enumerate((ya_ref, yb_ref, yc_ref, yd_ref)):
        part = jnp.dot(y_ref[...].astype(BF16), wo_ref[j * MIX_W:(j + 1) * MIX_W, :],
                       preferred_element_type=F32)
        acc = part if acc is None else acc + part
    x = x + mod(2) * _rms_rows(acc, g_ref[1:2, :])
    h = (_rms_rows(x, g_ref[2:3, :]) * (1.0 + mod(4)) + mod(3)).astype(BF16)
    f = None
    for j in range(FFN_HIDDEN // D_MODEL):
        u = jnp.dot(h, w1_ref[:, j * D_MODEL:(j + 1) * D_MODEL], preferred_element_type=F32)
        u = jnp.square(jnp.maximum(u, 0.0)).astype(BF16)
        part = jnp.dot(u, w2_ref[j * D_MODEL:(j + 1) * D_MODEL, :], preferred_element_type=F32)
        f = part if f is None else f + part
    o_ref[...] = x + mod(5) * _rms_rows(f, g_ref[3:4, :])


def _out_projection_ffn(x, ys, norm_g, mod, w_out_bf, w1_bf, w2_bf, layer, row_fn):
    n = x.shape[0]
    tile = lambda w: pl.BlockSpec((ROW_TILE, w), lambda i: (i, 0))
    return pl.pallas_call(
        functools.partial(_outproj_ffn_kernel, row_fn=row_fn),
        grid=(n // ROW_TILE,),
        in_specs=[tile(D_MODEL)] + [tile(MIX_W)] * 4 + [
            pl.BlockSpec((None, 4, D_MODEL), lambda i: (layer, 0, 0)),
            pl.BlockSpec((None, MOD_ROWS, N_MOD * D_MODEL), lambda i: (layer, 0, 0)),
            _resident((None, D_MODEL, D_MODEL), lambda i: (layer, 0, 0)),
            _resident((None, D_MODEL, FFN_HIDDEN), lambda i: (layer, 0, 0)),
            _resident((None, FFN_HIDDEN, D_MODEL), lambda i: (layer, 0, 0))],
        out_specs=tile(D_MODEL),
        out_shape=jax.ShapeDtypeStruct((n, D_MODEL), F32),
        compiler_params=_params("parallel"),
        name="out_projection_ffn",
    )(x, *ys, norm_g, mod, w_out_bf, w1_bf, w2_bf)


def _scan_kernel(*refs, n_rows, delta, steps, v_tiles, expand, reverse_odd):
    if expand:
        rows_ref, v_ref, s0_ref, e_ref, y_ref, sfin_ref, s_scr, x_scr = refs
    else:
        rows_ref, v_ref, s0_ref, y_ref, sfin_ref, s_scr = refs
        x_scr = rows_ref
    tb = pl.program_id(1)

    @pl.when(tb == 0)
    def _():
        s_scr[...] = s0_ref[...]

    if expand:
        for j in range(n_rows):
            flat = rows_ref[j].reshape(steps * HEAD_DIM, rows_ref.shape[-1])
            x_scr[j] = jnp.dot(flat, e_ref[...], preferred_element_type=F32,
                               precision=HIGHEST).reshape(steps, HEAD_DIM, LANES)

    n_part = max(1, 4 // v_tiles)
    tiles = [slice(i * SUBLANES, (i + 1) * SUBLANES) for i in range(v_tiles)]
    if delta:
        i_nkk, i_w, i_kka, i_kv, i_r = range(5)
    else:
        i_w, i_kv, i_r = range(3)

    def tree(parts):
        while len(parts) > 1:
            parts = [a + b for a, b in zip(parts[0::2], parts[1::2])]
        return parts[0]

    backward = (pl.program_id(0) % 2 == 1) if reverse_odd else None

    def step(i_step, carry):
        t = jnp.where(backward, steps - 1 - i_step, i_step) if reverse_odd else i_step
        row = lambda j, k: x_scr[j, t, k:k + 1, :]
        vt = [v_ref[t, tl, :] for tl in tiles]
        if delta:
            parts = [[None] * n_part for _ in tiles]
            for k in range(HEAD_DIM):
                nk = row(i_nkk, k)
                for i, tl in enumerate(tiles):
                    term = s_scr[k, tl, :] * nk
                    p = parts[i][k % n_part]
                    parts[i][k % n_part] = term if p is None else p + term
            sa = [tree(p) for p in parts]
        yparts = [[None] * n_part for _ in tiles]
        for k in range(HEAD_DIM):
            wk, kvk, rk = row(i_w, k), row(i_kv, k), row(i_r, k)
            if delta:
                kkak = row(i_kka, k)
            for i, tl in enumerate(tiles):
                s = s_scr[k, tl, :] * wk + vt[i] * kvk
                if delta:
                    s = s + sa[i] * kkak
                s_scr[k, tl, :] = s
                term = s * rk
                p = yparts[i][k % n_part]
                yparts[i][k % n_part] = term if p is None else p + term
        for i, tl in enumerate(tiles):
            y_ref[t, tl, :] = tree(yparts[i])
        return carry

    lax.fori_loop(0, steps, step, 0)

    @pl.when(tb == pl.num_programs(1) - 1)
    def _():
        sfin_ref[...] = s_scr[...]


def _scan(rows, v, s0, *, delta, steps, reverse_odd=False):
    groups, _, t_len, _, n_chain = rows.shape
    n_rows = 5 if delta else 3
    n_blocks = t_len // steps
    tblk = (lambda g, t: jnp.where(g % 2 == 1, n_blocks - 1 - t, t)) if reverse_odd else (lambda g, t: t)
    if v is None:
        vrows = HEAD_DIM
        v_spec = pl.BlockSpec((None, None, steps, vrows, LANES), lambda g, t: (g, n_rows, tblk(g, t), 0, 0))
        v = rows
    else:
        vrows = v.shape[2]
        v_spec = pl.BlockSpec((None, steps, vrows, LANES), lambda g, t: (g, tblk(g, t), 0, 0))
    v_tiles = vrows // SUBLANES
    expand = n_chain != LANES
    in_specs = [pl.BlockSpec((None, n_rows, steps, HEAD_DIM, n_chain), lambda g, t: (g, 0, tblk(g, t), 0, 0)),
                v_spec,
                pl.BlockSpec((None, HEAD_DIM, vrows, LANES), lambda g, t: (g, 0, 0, 0))]
    args = [rows, v, s0]
    scratch = [pltpu.VMEM((HEAD_DIM, vrows, LANES), F32)]
    if expand:
        lane = np.arange(LANES)
        e = (lane[None, :] % n_chain == np.arange(n_chain)[:, None]).astype(np.float32)
        in_specs.append(pl.BlockSpec((n_chain, LANES), lambda g, t: (0, 0)))
        args.append(jnp.asarray(e))
        scratch.append(pltpu.VMEM((n_rows, steps, HEAD_DIM, LANES), F32))
    return pl.pallas_call(
        functools.partial(_scan_kernel, n_rows=n_rows, delta=delta, steps=steps, v_tiles=v_tiles,
                          expand=expand, reverse_odd=reverse_odd),
        grid=(groups, n_blocks),
        in_specs=in_specs,
        out_specs=[pl.BlockSpec((None, steps, vrows, LANES), lambda g, t: (g, tblk(g, t), 0, 0)),
                   pl.BlockSpec((None, HEAD_DIM, vrows, LANES), lambda g, t: (g, 0, 0, 0))],
        out_shape=[jax.ShapeDtypeStruct((groups, t_len, vrows, LANES), F32),
                   jax.ShapeDtypeStruct((groups, HEAD_DIM, vrows, LANES), F32)],
        scratch_shapes=scratch,
        compiler_params=_params("parallel", "arbitrary"),
        name="rwkv_scan" if delta else "hgrn_scan",
    )(*args)


def _to_scan_layout(a, n_seq, t_len, per_dir_groups):
    n_arr = a.shape[1]
    a = a.reshape(2, n_arr, n_seq, t_len, N_HEADS, HEAD_DIM)
    a = jnp.stack([a[0], a[1][:, :, ::-1]])
    if per_dir_groups:
        r = a.transpose(0, 1, 3, 5, 2, 4).reshape(2, n_arr, t_len, HEAD_DIM, n_seq * N_HEADS)
        return r[:, :n_arr - 1], r[:, n_arr - 1]
    n_chain = 2 * n_seq * N_HEADS
    vg = LANES // n_chain
    rows = a[:, :n_arr - 1].transpose(1, 3, 5, 0, 2, 4).reshape(1, n_arr - 1, t_len, HEAD_DIM, n_chain)
    v = a[:, n_arr - 1].reshape(2, n_seq, t_len, N_HEADS, vg, HEAD_DIM // vg)
    v = v.transpose(2, 5, 4, 0, 1, 3).reshape(1, t_len, HEAD_DIM // vg, LANES)
    return rows, v


def _from_scan_layout(y, n_seq, t_len, per_dir_groups):
    if per_dir_groups:
        y = y.reshape(2, t_len, HEAD_DIM, n_seq, N_HEADS)
        y = y[0] + y[1][::-1]
        return y.transpose(2, 0, 3, 1).reshape(n_seq * t_len, MIX_W)
    n_chain = 2 * n_seq * N_HEADS
    vg = LANES // n_chain
    y = y.reshape(t_len, HEAD_DIM // vg, vg, 2, n_seq, N_HEADS)
    y = y[:, :, :, 0] + y[::-1, :, :, 1]
    return y.transpose(3, 0, 4, 2, 1).reshape(n_seq * t_len, MIX_W)


def _state_to_scan(s, key_last):
    n_seq = s.shape[0]
    vg = LANES // (2 * n_seq * N_HEADS)
    if key_last:
        s = s.reshape(n_seq, 2, N_HEADS, vg, HEAD_DIM // vg, HEAD_DIM).transpose(5, 4, 3, 1, 0, 2)
    else:
        s = s.reshape(n_seq, 2, N_HEADS, HEAD_DIM, vg, HEAD_DIM // vg).transpose(3, 5, 4, 1, 0, 2)
    return s.reshape(1, HEAD_DIM, HEAD_DIM // vg, LANES)


def _state_from_scan(s, n_seq, key_last):
    s = s.reshape(2, HEAD_DIM, HEAD_DIM, N_HEADS, n_seq)
    return s.transpose(4, 0, 3, 2, 1) if key_last else s.transpose(4, 0, 3, 1, 2)


def _shifted(x, d, t_len):
    row = lax.broadcasted_iota(jnp.int32, (t_len, 1), 0)
    prev_f = jnp.where(row == 0, 0.0, pltpu.roll(x, 1, 0))
    prev_b = jnp.where(row == t_len - 1, 0.0, pltpu.roll(x, t_len - 1, 0))
    return jnp.where(d == 0, prev_f, prev_b)


def _rwkv_pre_kernel(rkv_ref, lora_ref, mu_rkv_ref, mu_lo_ref, wl_ref, b0_ref, kk_ref, ka_ref, rk_ref,
                     seg_ref, scan_ref, bonus_ref, *, t_len):
    d = pl.program_id(1)
    x = rkv_ref[...]
    x = x + (_shifted(x, d, t_len) - x) * mu_rkv_ref[...]
    r, k, v = x[:, 0:MIX_W], x[:, MIX_W:2 * MIX_W], x[:, 2 * MIX_W:3 * MIX_W]
    lo = lora_ref[...]
    lo = lo + (_shifted(lo, d, t_len) - lo) * mu_lo_ref[...]
    lane = lax.broadcasted_iota(jnp.int32, lo.shape, 1)
    lo = jnp.where(lane < LORA_RANK, jnp.tanh(lo), lo)
    pre = _bdot(lo, wl_ref[...]) + b0_ref[...]
    w = jnp.exp(-RWKV_DECAY_SCALE * _sigmoid(pre[:, 0:MIX_W]))
    a = _sigmoid(pre[:, MIX_W:2 * MIX_W])
    seg = seg_ref[...]
    kk = k * kk_ref[...]
    kk = kk / jnp.maximum(jnp.sqrt(_seg_sum(kk * kk, seg)), 1e-12)
    k = k * (1.0 + (a - 1.0) * ka_ref[...])
    scan_ref[0] = -kk
    scan_ref[1] = w
    scan_ref[2] = kk * a
    scan_ref[3] = k
    scan_ref[4] = r
    scan_ref[5] = v
    bonus_ref[...] = _seg_sum(r * k * rk_ref[...], seg) * v


def _rwkv_pre(p_rw, lw, seg, n_seq, t_len, seq0):
    n = n_seq * t_len
    lora_blk = (3 * MIX_W + GATE_RANK) // (2 * LORA_RANK)
    vec = lambda w: pl.BlockSpec((None, 1, w), lambda i, d: (d, 0, 0))
    shared = lambda w: pl.BlockSpec((1, w), lambda i, d: (0, 0))
    return pl.pallas_call(
        functools.partial(_rwkv_pre_kernel, t_len=t_len),
        grid=(n_seq, 2),
        in_specs=[pl.BlockSpec((t_len, 3 * MIX_W), lambda i, d: (seq0 + i, 0)),
                  pl.BlockSpec((t_len, 2 * LORA_RANK), lambda i, d: (seq0 + i, lora_blk + d)),
                  vec(3 * MIX_W), vec(2 * LORA_RANK),
                  pl.BlockSpec((None, 2 * LORA_RANK, 2 * MIX_W), lambda i, d: (d, 0, 0)),
                  vec(2 * MIX_W), shared(MIX_W), shared(MIX_W), shared(MIX_W),
                  pl.BlockSpec((MIX_W, MIX_W), lambda i, d: (0, 0))],
        out_specs=[pl.BlockSpec((None, 6, t_len, MIX_W), lambda i, d: (d, 0, i, 0)),
                   pl.BlockSpec((None, t_len, MIX_W), lambda i, d: (d, i, 0))],
        out_shape=[jax.ShapeDtypeStruct((2, 6, n, MIX_W), F32),
                   jax.ShapeDtypeStruct((2, n, MIX_W), F32)],
        compiler_params=_params("parallel", "parallel"),
        name="rwkv_pre",
    )(p_rw, p_rw, lw["mu_rkv"], lw["mu_lo"], lw["w_lora"], lw["b_lora"], lw["rw_kk"], lw["rw_ka"],
      lw["rw_rk"], seg)


def _rwkv_post_kernel(y_ref, bonus_ref, gh_ref, g2_ref, lnw_ref, lnb_ref, seg_ref, o_ref):
    seg = seg_ref[...]
    y = y_ref[...]
    mu = _seg_sum(y, seg) * (1.0 / HEAD_DIM)
    yc = y - mu
    var = _seg_sum(yc * yc, seg) * (1.0 / HEAD_DIM)
    y = yc * lax.rsqrt(var + RWKV_LN_EPS) * lnw_ref[...] + lnb_ref[...]
    y = y + bonus_ref[0] + bonus_ref[1]
    o_ref[...] = y * _bdot(_sigmoid(gh_ref[...]), g2_ref[...])


def _rwkv_post(y, bonus, p_rw, lw, seg, tile0):
    n = y.shape[0]
    gh_blk = 3 * MIX_W // GATE_RANK
    shared = lambda w: pl.BlockSpec((1, w), lambda i: (0, 0))
    return pl.pallas_call(
        _rwkv_post_kernel,
        grid=(n // ROW_TILE,),
        in_specs=[pl.BlockSpec((ROW_TILE, MIX_W), lambda i: (i, 0)),
                  pl.BlockSpec((2, ROW_TILE, MIX_W), lambda i: (0, i, 0)),
                  pl.BlockSpec((ROW_TILE, GATE_RANK), lambda i: (tile0 + i, gh_blk)),
                  pl.BlockSpec((GATE_RANK, MIX_W), lambda i: (0, 0)),
                  shared(MIX_W), shared(MIX_W),
                  pl.BlockSpec((MIX_W, MIX_W), lambda i: (0, 0))],
        out_specs=pl.BlockSpec((ROW_TILE, MIX_W), lambda i: (i, 0)),
        out_shape=jax.ShapeDtypeStruct((n, MIX_W), F32),
        compiler_params=_params("parallel"),
        name="rwkv_post",
    )(y, bonus, p_rw, lw["rw_g2"], lw["rw_lnx_w"], lw["rw_lnx_b"], seg)


def _hgrn_pre_kernel(qv_ref, f_ref, lb_ref, scan_ref):
    q = qv_ref[:, 0:MIX_W]
    f_raw = f_ref[...]
    lb = lb_ref[...]
    scan_ref[0] = lb + (1.0 - lb) * _sigmoid(f_raw)
    scan_ref[1] = (1.0 - lb) * _sigmoid(-f_raw)
    scan_ref[2] = q * _sigmoid(q)
    scan_ref[3] = qv_ref[:, MIX_W:2 * MIX_W]


def _hgrn_pre(p_hg, lb, n_tok, tile0):
    return pl.pallas_call(
        _hgrn_pre_kernel,
        grid=(n_tok // ROW_TILE, 2),
        in_specs=[pl.BlockSpec((ROW_TILE, 2 * MIX_W), lambda i, d: (tile0 + i, 0)),
                  pl.BlockSpec((ROW_TILE, MIX_W), lambda i, d: (tile0 + i, 3 + d)),
                  pl.BlockSpec((None, 1, MIX_W), lambda i, d: (d, 0, 0))],
        out_specs=pl.BlockSpec((None, 4, ROW_TILE, MIX_W), lambda i, d: (d, 0, i, 0)),
        out_shape=jax.ShapeDtypeStruct((2, 4, n_tok, MIX_W), F32),
        compiler_params=_params("parallel", "parallel"),
        name="hgrn_pre",
    )(p_hg, p_hg, lb)


def _hgrn_post_kernel(o_ref, g_ref, nw_ref, seg_ref, out_ref):
    o = o_ref[...]
    ms = _seg_sum(o * o, seg_ref[...]) * (1.0 / HEAD_DIM)
    out_ref[...] = o * lax.rsqrt(ms + NORM_EPS) * nw_ref[...] * _sigmoid(g_ref[...])


def _hgrn_post(o, p_hg, hg_norm, seg, tile0):
    n = o.shape[0]
    return pl.pallas_call(
        _hgrn_post_kernel,
        grid=(n // ROW_TILE,),
        in_specs=[pl.BlockSpec((ROW_TILE, MIX_W), lambda i: (i, 0)),
                  pl.BlockSpec((ROW_TILE, MIX_W), lambda i: (tile0 + i, 2)),
                  pl.BlockSpec((1, MIX_W), lambda i: (0, 0)),
                  pl.BlockSpec((MIX_W, MIX_W), lambda i: (0, 0))],
        out_specs=pl.BlockSpec((ROW_TILE, MIX_W), lambda i: (i, 0)),
        out_shape=jax.ShapeDtypeStruct((n, MIX_W), F32),
        compiler_params=_params("parallel"),
        name="hgrn_post",
    )(o, p_hg, hg_norm, seg)


CTX_TILE_STEPS = ROW_TILE // 32


def _fm_to_scan(a, n_seq):
    grp = lax.broadcasted_iota(jnp.int32, (HEAD_DIM, LANES), 1) // n_seq
    rolled = []
    for h in range(N_HEADS):
        blk = a[h * HEAD_DIM:(h + 1) * HEAD_DIM]
        rolled.append([blk if s == 0 else pltpu.roll(blk, s * n_seq, 1) for s in range(N_HEADS)])
    tiles = []
    for t in range(N_HEADS):
        out = rolled[0][(0 - t) % N_HEADS]
        for h in range(1, N_HEADS):
            out = jnp.where(grp == h, rolled[h][(h - t) % N_HEADS], out)
        tiles.append(out)
    return tiles


def _scan_to_fm(tiles, n_seq):
    grp = lax.broadcasted_iota(jnp.int32, (HEAD_DIM, LANES), 1) // n_seq
    rolled = [[tl if s == 0 else pltpu.roll(tl, s * n_seq, 1) for s in range(N_HEADS)] for tl in tiles]
    blocks = []
    for h in range(N_HEADS):
        out = rolled[0][(0 - h) % N_HEADS]
        for t in range(1, N_HEADS):
            out = jnp.where(grp == t, rolled[t][(t - h) % N_HEADS], out)
        blocks.append(out)
    return jnp.concatenate(blocks, axis=0)


def _store_scan_tiles(scan_ref, d, j, a, n_seq):
    for c in range(a.shape[1] // LANES):
        for t, tile in enumerate(_fm_to_scan(a[:, c * LANES:(c + 1) * LANES], n_seq)):
            scan_ref[d, j, c * N_HEADS + t] = tile


def _load_scan_tiles(y_ref, n_seq):
    cols = []
    for c in range(y_ref.shape[1] // N_HEADS):
        tiles = [y_ref[0, c * N_HEADS + t] + y_ref[1, c * N_HEADS + t] for t in range(N_HEADS)]
        cols.append(_scan_to_fm(tiles, n_seq))
    return jnp.concatenate(cols, axis=1)


def _head_sum_rows(x):
    w = x.shape[1]
    s = jnp.sum(x.reshape(N_HEADS, HEAD_DIM, w), axis=1, keepdims=True)
    return jnp.broadcast_to(s, (N_HEADS, HEAD_DIM, w)).reshape(N_HEADS * HEAD_DIM, w)


def _lane_halo(halo, shift, valid, width):
    h = jnp.where(valid, pltpu.roll(halo, shift, 1), 0.0)
    return jnp.concatenate([h] * (width // LANES), axis=1)


def _prev_step(x, halo, valid, n_seq):
    width = x.shape[1]
    lane = lax.broadcasted_iota(jnp.int32, (1, width), 1)
    return jnp.where(lane < n_seq, _lane_halo(halo, n_seq, valid, width), pltpu.roll(x, n_seq, 1))


def _next_step(x, halo, valid, n_seq):
    width = x.shape[1]
    lane = lax.broadcasted_iota(jnp.int32, (1, width), 1)
    return jnp.where(lane >= width - n_seq, _lane_halo(halo, LANES - n_seq, valid, width),
                     pltpu.roll(x, width - n_seq, 1))


def _rwkv_pre_fm_kernel(x_ref, hp_ref, hn_ref, mu_rkv_ref, mu_lo_ref, w2t_ref, a2t_ref, w0_ref, a0_ref,
                        kk_ref, ka_ref, rk_ref, scan_ref, bonus_ref, *, n_seq):
    i = pl.program_id(0)
    last = pl.num_programs(0) - 1
    rkv_rows = slice(0, 3 * MIX_W)
    x = x_ref[rkv_rows, :]
    bonus = None
    for d in range(2):
        if d == 0:
            shifted = lambda a, rows: _prev_step(a, hp_ref[rows, :], i > 0, n_seq)
        else:
            shifted = lambda a, rows: _next_step(a, hn_ref[rows, :], i < last, n_seq)
        x3 = x + (shifted(x, rkv_rows) - x) * mu_rkv_ref[d]
        r, k, v = x3[0:MIX_W], x3[MIX_W:2 * MIX_W], x3[2 * MIX_W:3 * MIX_W]
        lo_rows = slice(3 * MIX_W + GATE_RANK + 2 * LORA_RANK * d, 3 * MIX_W + GATE_RANK + 2 * LORA_RANK * (d + 1))
        lo = x_ref[lo_rows, :]
        lo = lo + (shifted(lo, lo_rows) - lo) * mu_lo_ref[d]
        w_pre = _bdot(w2t_ref[d], jnp.tanh(lo[0:LORA_RANK])) + w0_ref[d]
        a_pre = _bdot(a2t_ref[d], lo[LORA_RANK:2 * LORA_RANK]) + a0_ref[d]
        w = jnp.exp(-RWKV_DECAY_SCALE * _sigmoid(w_pre))
        a = _sigmoid(a_pre)
        kk = k * kk_ref[...]
        kk = kk / jnp.maximum(jnp.sqrt(_head_sum_rows(kk * kk)), 1e-12)
        k = k * (1.0 + (a - 1.0) * ka_ref[...])
        for j, arr in enumerate((-kk, w, kk * a, k, r, v)):
            _store_scan_tiles(scan_ref, d, j, arr, n_seq)
        b = _head_sum_rows(r * k * rk_ref[...]) * v
        bonus = b if bonus is None else bonus + b
    bonus_ref[...] = bonus


def _rwkv_pre_fm(pt_rw, lw, n_seq, t_len):
    n = pt_rw.shape[1]
    n_tiles = n // ROW_TILE
    per_tile = ROW_TILE // LANES
    col = lambda rows: pl.BlockSpec((rows, 1), lambda i: (0, 0))
    dcol = lambda rows: pl.BlockSpec((2, rows, 1), lambda i: (0, 0, 0))
    return pl.pallas_call(
        functools.partial(_rwkv_pre_fm_kernel, n_seq=n_seq),
        grid=(n_tiles,),
        in_specs=[pl.BlockSpec((RW_COLS, ROW_TILE), lambda i: (0, i)),
                  pl.BlockSpec((RW_COLS, LANES), lambda i: (0, jnp.maximum(i * per_tile - 1, 0))),
                  pl.BlockSpec((RW_COLS, LANES),
                               lambda i: (0, jnp.minimum((i + 1) * per_tile, n_tiles * per_tile - 1))),
                  dcol(3 * MIX_W), dcol(2 * LORA_RANK),
                  pl.BlockSpec((2, MIX_W, LORA_RANK), lambda i: (0, 0, 0)),
                  pl.BlockSpec((2, MIX_W, LORA_RANK), lambda i: (0, 0, 0)),
                  dcol(MIX_W), dcol(MIX_W), col(MIX_W), col(MIX_W), col(MIX_W)],
        out_specs=[pl.BlockSpec((2, 6, CTX_TILE_STEPS, HEAD_DIM, LANES), lambda i: (0, 0, i, 0, 0)),
                   pl.BlockSpec((MIX_W, ROW_TILE), lambda i: (0, i))],
        out_shape=[jax.ShapeDtypeStruct((2, 6, t_len, HEAD_DIM, LANES), F32),
                   jax.ShapeDtypeStruct((MIX_W, n), F32)],
        compiler_params=_params("parallel"),
        name="rwkv_pre_fm",
    )(pt_rw, pt_rw, pt_rw, lw["mu_rkv_col"], lw["mu_lo_col"], lw["w2t"], lw["a2t"], lw["w0_col"],
      lw["a0_col"], lw["kk_col"], lw["ka_col"], lw["rk_col"])


def _rwkv_post_fm_kernel(y_ref, bonus_ref, gh_ref, g2t_ref, lnw_ref, lnb_ref, o_ref, *, n_seq):
    y = _load_scan_tiles(y_ref, n_seq)
    mu = _head_sum_rows(y) * (1.0 / HEAD_DIM)
    yc = y - mu
    var = _head_sum_rows(yc * yc) * (1.0 / HEAD_DIM)
    y = yc * lax.rsqrt(var + RWKV_LN_EPS) * lnw_ref[...] + lnb_ref[...] + bonus_ref[...]
    g = _bdot(g2t_ref[...], _sigmoid(gh_ref[...]))
    o_ref[...] = (y * g).T


def _rwkv_post_fm(y, bonus, pt_rw, lw, n_seq):
    n = bonus.shape[1]
    col = pl.BlockSpec((MIX_W, 1), lambda i: (0, 0))
    return pl.pallas_call(
        functools.partial(_rwkv_post_fm_kernel, n_seq=n_seq),
        grid=(n // ROW_TILE,),
        in_specs=[pl.BlockSpec((2, CTX_TILE_STEPS, HEAD_DIM, LANES), lambda i: (0, i, 0, 0)),
                  pl.BlockSpec((MIX_W, ROW_TILE), lambda i: (0, i)),
                  pl.BlockSpec((GATE_RANK, ROW_TILE), lambda i: (3 * MIX_W // GATE_RANK, i)),
                  pl.BlockSpec((MIX_W, GATE_RANK), lambda i: (0, 0)),
                  col, col],
        out_specs=pl.BlockSpec((ROW_TILE, MIX_W), lambda i: (i, 0)),
        out_shape=jax.ShapeDtypeStruct((n, MIX_W), F32),
        compiler_params=_params("parallel"),
        name="rwkv_post_fm",
    )(y, bonus, pt_rw, lw["g2t"], lw["lnw_col"], lw["lnb_col"])


def _hgrn_pre_fm_kernel(qv_ref, ff_ref, fb_ref, lb_ref, scan_ref, *, n_seq):
    q = qv_ref[0:MIX_W, :]
    q = q * _sigmoid(q)
    v = qv_ref[MIX_W:2 * MIX_W, :]
    for d, f_ref in enumerate((ff_ref, fb_ref)):
        f_raw = f_ref[...]
        lb = lb_ref[d]
        f = lb + (1.0 - lb) * _sigmoid(f_raw)
        kv = (1.0 - lb) * _sigmoid(-f_raw)
        for j, arr in enumerate((f, kv, q, v)):
            _store_scan_tiles(scan_ref, d, j, arr, n_seq)


def _hgrn_pre_fm(pt_hg, lb_col, n_seq, t_len):
    n = pt_hg.shape[1]
    return pl.pallas_call(
        functools.partial(_hgrn_pre_fm_kernel, n_seq=n_seq),
        grid=(n // ROW_TILE,),
        in_specs=[pl.BlockSpec((2 * MIX_W, ROW_TILE), lambda i: (0, i)),
                  pl.BlockSpec((MIX_W, ROW_TILE), lambda i: (3, i)),
                  pl.BlockSpec((MIX_W, ROW_TILE), lambda i: (4, i)),
                  pl.BlockSpec((2, MIX_W, 1), lambda i: (0, 0, 0))],
        out_specs=pl.BlockSpec((2, 4, CTX_TILE_STEPS, HEAD_DIM, LANES), lambda i: (0, 0, i, 0, 0)),
        out_shape=jax.ShapeDtypeStruct((2, 4, t_len, HEAD_DIM, LANES), F32),
        compiler_params=_params("parallel"),
        name="hgrn_pre_fm",
    )(pt_hg, pt_hg, pt_hg, lb_col)


def _hgrn_post_fm_kernel(o_ref, g_ref, nw_ref, out_ref, *, n_seq):
    o = _load_scan_tiles(o_ref, n_seq)
    ms = _head_sum_rows(o * o) * (1.0 / HEAD_DIM)
    out_ref[...] = (o * lax.rsqrt(ms + NORM_EPS) * nw_ref[...] * _sigmoid(g_ref[...])).T


def _hgrn_post_fm(o, pt_hg, nw_col, n_seq):
    n = pt_hg.shape[1]
    return pl.pallas_call(
        functools.partial(_hgrn_post_fm_kernel, n_seq=n_seq),
        grid=(n // ROW_TILE,),
        in_specs=[pl.BlockSpec((2, CTX_TILE_STEPS, HEAD_DIM, LANES), lambda i: (0, i, 0, 0)),
                  pl.BlockSpec((MIX_W, ROW_TILE), lambda i: (2, i)),
                  pl.BlockSpec((MIX_W, 1), lambda i: (0, 0))],
        out_specs=pl.BlockSpec((ROW_TILE, MIX_W), lambda i: (i, 0)),
        out_shape=jax.ShapeDtypeStruct((n, MIX_W), F32),
        compiler_params=_params("parallel"),
        name="hgrn_post_fm",
    )(o, pt_hg, nw_col)


def _ctx_recurrent_mixers(pt_rw, pt_hg, lw, n_seq, t_len):
    assert n_seq * N_HEADS == LANES
    s0 = jnp.zeros((2, HEAD_DIM, HEAD_DIM, LANES), F32)
    scan_in, bonus = _rwkv_pre_fm(pt_rw, lw, n_seq, t_len)
    y, s_rw = _scan(scan_in, None, s0, delta=True, steps=32, reverse_odd=True)
    ya = _rwkv_post_fm(y, bonus, pt_rw, lw, n_seq)
    scan_in = _hgrn_pre_fm(pt_hg, lw["hg_lb_col"], n_seq, t_len)
    o, s_hg = _scan(scan_in, None, s0, delta=False, steps=32, reverse_odd=True)
    yc = _hgrn_post_fm(o, pt_hg, lw["hg_norm_col"], n_seq)
    return ya, yc, s_rw, s_hg


def _head(x, h):
    return x[:, h * HEAD_DIM:(h + 1) * HEAD_DIM]


def _softmax_av(scores, values, sink=None):
    m = None
    for s in scores:
        mi = jnp.max(s, axis=-1, keepdims=True)
        m = mi if m is None else jnp.maximum(m, mi)
    if sink is not None:
        m = jnp.maximum(m, sink)
    den = None if sink is None else jnp.exp(sink - m)
    out = None
    for s, v in zip(scores, values):
        e = jnp.exp(s - m)
        di = jnp.sum(e, axis=-1, keepdims=True)
        den = di if den is None else den + di
        oi = _bdot(e, v)
        out = oi if out is None else out + oi
    return out / den


def _ctx_attn_kernel(sink_ref, p_ref, y_ref, cache_ref, *, kv_heads, use_sink):
    group = N_HEADS // kv_heads
    kv_w = kv_heads * HEAD_DIM
    for j in range(SUBLANES):
        p = p_ref[:, j, :]
        q_all = p[:, 0:MIX_W] * ATTN_SCALE
        k_all = p[:, MIX_W:MIX_W + kv_w]
        v_all = p[:, MIX_W + kv_w:MIX_W + 2 * kv_w]
        cache_ref[j, 0] = k_all
        cache_ref[j, 1] = v_all
        outs = []
        for h in range(N_HEADS):
            kvh = h // group
            outs.append(_softmax_av([_bdot_nt(_head(q_all, h), _head(k_all, kvh))], [_head(v_all, kvh)],
                                    sink=sink_ref[h] if use_sink else None))
        y_ref[:, j, :] = jnp.concatenate(outs, axis=-1)


def _ctx_attention(p, sink, n_seq, t_len, kv_heads, use_sink):
    kv_w = kv_heads * HEAD_DIM
    cols = MIX_W + 2 * kv_w
    y, cache = pl.pallas_call(
        functools.partial(_ctx_attn_kernel, kv_heads=kv_heads, use_sink=use_sink),
        grid=(n_seq // SUBLANES,),
        in_specs=[pl.BlockSpec(memory_space=pltpu.SMEM),
                  pl.BlockSpec((t_len, SUBLANES, cols), lambda i: (0, i, 0))],
        out_specs=[pl.BlockSpec((t_len, SUBLANES, MIX_W), lambda i: (0, i, 0)),
                   pl.BlockSpec((SUBLANES, 2, t_len, kv_w), lambda i: (i, 0, 0, 0))],
        out_shape=[jax.ShapeDtypeStruct((t_len, n_seq, MIX_W), F32),
                   jax.ShapeDtypeStruct((n_seq, 2, t_len, kv_w), F32)],
        compiler_params=_params("parallel"),
        name="ctx_attention",
    )(sink, p.reshape(t_len, n_seq, cols))
    return y.reshape(t_len * n_seq, MIX_W), cache


def _nat_latent_kernel(p_ref, ckv_ref, bias_ref, o_ref, *, rows):
    kh = min(NAT_KH, rows)
    n_loc = kh * GRID_W
    ck_all = ckv_ref[0]
    cv_all = ckv_ref[1]
    for r in range(rows):
        rs = min(max(r - kh // 2, 0), rows - kh)
        roff0 = rs - r + NAT_KH - 1
        q_all = p_ref[r * GRID_W:(r + 1) * GRID_W, 0:MIX_W] * ATTN_SCALE
        k_all = p_ref[rs * GRID_W:rs * GRID_W + n_loc, MIX_W:2 * MIX_W]
        v_all = p_ref[rs * GRID_W:rs * GRID_W + n_loc, 2 * MIX_W:3 * MIX_W]
        outs = []
        for h in range(N_HEADS):
            q = _head(q_all, h)
            bias = jnp.concatenate([bias_ref[h, roff0 + 2 * j] for j in range(kh // 2)], axis=-1)
            s_loc = _bdot_nt(q, _head(k_all, h)) + bias
            outs.append(_softmax_av([s_loc, _bdot_nt(q, _head(ck_all, h))],
                                    [_head(v_all, h), _head(cv_all, h)]))
        o_ref[r * GRID_W:(r + 1) * GRID_W, :] = jnp.concatenate(outs, axis=-1)


def _nat_bias_table(rpb):
    q = np.arange(GRID_W)[:, None]
    kc = np.arange(GRID_W)[None, :]
    win = np.clip(q - NAT_KW // 2, 0, GRID_W - NAT_KW)
    ok = (kc >= win) & (kc < win + NAT_KW)
    coff = np.clip(kc - q, -(NAT_KW - 1), NAT_KW - 1) + NAT_KW - 1
    t = jnp.where(ok[None, None], rpb[:, :, coff], MASK_VALUE)
    return jnp.concatenate([t[:, :-1], t[:, 1:]], axis=-1)


def _nat_latent(p_nat, cache, bias, layer, n_seq, t_len, tile0):
    rows = t_len // GRID_W
    return pl.pallas_call(
        functools.partial(_nat_latent_kernel, rows=rows),
        grid=(n_seq,),
        in_specs=[pl.BlockSpec((t_len, NAT_COLS), lambda i: (tile0 + i, 0)),
                  pl.BlockSpec((None, None, 2, cache.shape[3], MIX_W), lambda i: (i, layer, 0, 0, 0)),
                  pl.BlockSpec(bias.shape, lambda i: (0, 0, 0, 0))],
        out_specs=pl.BlockSpec((t_len, MIX_W), lambda i: (i, 0)),
        out_shape=jax.ShapeDtypeStruct((n_seq * t_len, MIX_W), F32),
        compiler_params=_params("parallel"),
        name="nat_latent",
    )(p_nat, cache, bias)


def _rope(x, cos, sin_signed):
    width = x.shape[-1]
    lane = lax.broadcasted_iota(jnp.int32, x.shape, 1)
    half = HEAD_DIM // 4
    partner = jnp.where(lane % (2 * half) < half, pltpu.roll(x, width - half, 1), pltpu.roll(x, half, 1))
    return x * cos + partner * sin_signed


def _swa_latent_kernel(sink_ref, p_ref, ckv_ref, cos_ref, sin_ref, o_ref, *, t_len):
    hd = HEAD_DIM
    kv_w = SWA_KV_HEADS * hd
    group = N_HEADS // SWA_KV_HEADS
    blk = SWA_BLOCK
    nb = t_len // blk
    q = _rope(p_ref[:, 0:MIX_W], cos_ref[...], sin_ref[...]) * ATTN_SCALE
    k = _rope(p_ref[:, MIX_W:MIX_W + kv_w], cos_ref[:, 0:kv_w], sin_ref[:, 0:kv_w])
    v = p_ref[:, MIX_W + kv_w:MIX_W + 2 * kv_w]
    row = lax.broadcasted_iota(jnp.int32, (group * blk, 1), 0)
    ck_all = ckv_ref[0]
    cv_all = ckv_ref[1]
    for n in range(nb):
        lo = max(n - 1, 0) * blk
        hi = min(n + 2, nb) * blk
        qpos = n * blk + row % blk
        kpos = lo + lax.broadcasted_iota(jnp.int32, (group * blk, hi - lo), 1)
        ok = jnp.abs(kpos - qpos) <= SWA_WINDOW
        heads = [None] * N_HEADS
        for kvh in range(SWA_KV_HEADS):
            qs = jnp.concatenate([_head(q[n * blk:(n + 1) * blk], kvh * group + g) for g in range(group)],
                                 axis=0)
            sink = jnp.concatenate([jnp.full((blk, 1), sink_ref[kvh * group + g], F32)
                                    for g in range(group)], axis=0)
            s_loc = jnp.where(ok, _bdot_nt(qs, _head(k[lo:hi], kvh)), MASK_VALUE)
            s_ctx = _bdot_nt(qs, _head(ck_all, kvh))
            o = _softmax_av([s_loc, s_ctx], [_head(v[lo:hi], kvh), _head(cv_all, kvh)], sink=sink)
            for g in range(group):
                heads[kvh * group + g] = o[g * blk:(g + 1) * blk]
        o_ref[n * blk:(n + 1) * blk, :] = jnp.concatenate(heads, axis=-1)


def _rope_tables(t_len):
    t = jnp.arange(t_len)
    half = HEAD_DIM // 2
    nf = half // 2
    inv = 1.0 / (ROPE_BASE ** (jnp.arange(nf, dtype=F32) / nf))
    cols = []
    for pos in ((t // GRID_W).astype(F32), (t % GRID_W).astype(F32)):
        ang = pos[:, None] * inv[None, :]
        cols.append((jnp.cos(ang), jnp.sin(ang)))
    cos = jnp.concatenate([cols[0][0], cols[0][0], cols[1][0], cols[1][0]], axis=-1)
    sin = jnp.concatenate([-cols[0][1], cols[0][1], -cols[1][1], cols[1][1]], axis=-1)
    return jnp.tile(cos, (1, N_HEADS)), jnp.tile(sin, (1, N_HEADS))


def _swa_latent(p_swa, cache, sink, rope, layer, n_seq, t_len, tile0):
    kv_w = SWA_KV_HEADS * HEAD_DIM
    return pl.pallas_call(
        functools.partial(_swa_latent_kernel, t_len=t_len),
        grid=(n_seq,),
        in_specs=[pl.BlockSpec(memory_space=pltpu.SMEM),
                  pl.BlockSpec((t_len, SWA_COLS), lambda i: (tile0 + i, 0)),
                  pl.BlockSpec((None, None, 2, cache.shape[3], kv_w), lambda i: (i, layer, 0, 0, 0)),
                  pl.BlockSpec((t_len, MIX_W), lambda i: (0, 0)),
                  pl.BlockSpec((t_len, MIX_W), lambda i: (0, 0))],
        out_specs=pl.BlockSpec((t_len, MIX_W), lambda i: (i, 0)),
        out_shape=jax.ShapeDtypeStruct((n_seq * t_len, MIX_W), F32),
        compiler_params=_params("parallel"),
        name="swa_latent",
    )(sink, p_swa, cache, rope[0], rope[1])


def _recurrent_mixers(p_rw, p_hg, lw, seg, n_seq, t_len, tile0, states):
    ctx = states is None
    seq0 = tile0 * ROW_TILE // t_len
    n_tok = n_seq * t_len
    steps = 32

    scan_in, bonus = _rwkv_pre(p_rw, lw, seg, n_seq, t_len, seq0)
    rows, v = _to_scan_layout(scan_in, n_seq, t_len, ctx)
    s0 = jnp.zeros((2, HEAD_DIM, HEAD_DIM, LANES), F32) if ctx else _state_to_scan(states[0], True)
    y, s_rw = _scan(rows, v, s0, delta=True, steps=steps)
    ya = _rwkv_post(_from_scan_layout(y, n_seq, t_len, ctx), bonus, p_rw, lw, seg, tile0)

    scan_in = _hgrn_pre(p_hg, lw["hg_lb"], n_tok, tile0)
    rows, v = _to_scan_layout(scan_in, n_seq, t_len, ctx)
    s0 = jnp.zeros((2, HEAD_DIM, HEAD_DIM, LANES), F32) if ctx else _state_to_scan(states[1], False)
    o, s_hg = _scan(rows, v, s0, delta=False, steps=steps)
    yc = _hgrn_post(_from_scan_layout(o, n_seq, t_len, ctx), p_hg, lw["hg_norm"], seg, tile0)
    return ya, yc, s_rw, s_hg


def kernel(x_prompt, x_sample, cache_nat_kv, cache_swa_kv, state_rwkv, state_hgrn, c, c_ctx, norm_g, mod_w, mod_b, w_in, w_out, rw_mu_rkv, rw_mu_lora, rw_w0, rw_w2, rw_a0, rw_a2, rw_g2, rw_kk, rw_ka, rw_rk, rw_lnx_w, rw_lnx_b, nat_rpb, hg_lb_logits, hg_norm, swa_sink, ffn_w1, ffn_w2):
    n_ctx, t_ctx, _ = x_prompt.shape
    n_lat, t_lat, _ = x_sample.shape
    past = cache_nat_kv.shape[3]

    cond = jnp.zeros((MOD_ROWS, D_MODEL), F32).at[0].set(c_ctx).at[1:1 + n_lat].set(c)
    mod = _modulation_all(cond, mod_w, mod_b)

    w_in_bf, w_out_bf = w_in.astype(BF16), w_out.astype(BF16)
    w1_bf, w2_bf = ffn_w1.astype(BF16), ffn_w2.astype(BF16)
    cuts = np.cumsum((0,) + SLABS)
    lat_slabs = tuple(w_in_bf[:, :, cuts[j]:cuts[j + 1]] for j in range(4))
    ctx_fm = (True, False, True, False)
    ctx_slabs = tuple(w.transpose(0, 2, 1) if fm else w for w, fm in zip(lat_slabs, ctx_fm))

    lb_sm = jax.nn.softmax(hg_lb_logits.astype(F32), axis=1)
    hg_lb = jnp.cumsum(lb_sm, axis=1) - lb_sm[:, :1]

    head_of = np.arange(MIX_W) // HEAD_DIM
    seg = jnp.asarray((head_of[:, None] == head_of[None, :]).astype(np.float32))
    rope = _rope_tables(t_lat)
    zeros_lora = jnp.zeros((LORA_RANK, MIX_W), F32)
    cache_nat = cache_nat_kv.reshape(n_lat, DEPTH, 2, past, MIX_W)
    cache_swa = cache_swa_kv.reshape(n_lat, DEPTH, 2, past, SWA_KV_HEADS * HEAD_DIM)

    xp = x_prompt.transpose(1, 0, 2).reshape(t_ctx * n_ctx, D_MODEL)
    xs = x_sample.reshape(n_lat * t_lat, D_MODEL)
    ctx_row, lat_row = _mod_row(t_ctx, 0), _mod_row(t_lat, 1)
    nat_out, swa_out, rw_out, hg_out = [], [], [], []
    for l in range(DEPTH):
        lw = {
            "mu_rkv": rw_mu_rkv[l].reshape(2, 1, 3 * MIX_W),
            "mu_lo": rw_mu_lora[l].reshape(2, 1, 2 * LORA_RANK),
            "w_lora": jnp.stack([jnp.concatenate(
                [jnp.concatenate([rw_w2[l, d], zeros_lora], axis=1),
                 jnp.concatenate([zeros_lora, rw_a2[l, d]], axis=1)], axis=0) for d in range(2)]),
            "b_lora": jnp.concatenate([rw_w0[l], rw_a0[l]], axis=-1).reshape(2, 1, 2 * MIX_W),
            "rw_kk": rw_kk[l].reshape(1, MIX_W), "rw_ka": rw_ka[l].reshape(1, MIX_W),
            "rw_rk": rw_rk[l].reshape(1, MIX_W), "rw_g2": rw_g2[l],
            "rw_lnx_w": rw_lnx_w[l].reshape(1, MIX_W), "rw_lnx_b": rw_lnx_b[l].reshape(1, MIX_W),
            "hg_lb": hg_lb[:, l].reshape(2, 1, MIX_W), "hg_norm": hg_norm[l].reshape(1, MIX_W),
            "mu_rkv_col": rw_mu_rkv[l].reshape(2, 3 * MIX_W, 1), "mu_lo_col": rw_mu_lora[l].reshape(2, 2 * LORA_RANK, 1),
            "w2t": rw_w2[l].transpose(0, 2, 1), "a2t": rw_a2[l].transpose(0, 2, 1),
            "w0_col": rw_w0[l].reshape(2, MIX_W, 1), "a0_col": rw_a0[l].reshape(2, MIX_W, 1),
            "kk_col": rw_kk[l].reshape(MIX_W, 1), "ka_col": rw_ka[l].reshape(MIX_W, 1),
            "rk_col": rw_rk[l].reshape(MIX_W, 1), "g2t": rw_g2[l].T,
            "lnw_col": rw_lnx_w[l].reshape(MIX_W, 1), "lnb_col": rw_lnx_b[l].reshape(MIX_W, 1),
            "hg_lb_col": hg_lb[:, l].reshape(2, MIX_W, 1), "hg_norm_col": hg_norm[l].reshape(MIX_W, 1),
        }
        sink = swa_sink[l]

        pt_rw, p_nat, pt_hg, p_swa = _in_projection(xp, norm_g, mod, ctx_slabs, l, ctx_row, ctx_fm)
        ya, yc, s_rw, s_hg = _ctx_recurrent_mixers(pt_rw, pt_hg, lw, n_ctx, t_ctx)
        yb, cnat = _ctx_attention(p_nat, sink, n_ctx, t_ctx, N_HEADS, False)
        yd, cswa = _ctx_attention(p_swa, sink, n_ctx, t_ctx, SWA_KV_HEADS, True)
        xp = _out_projection_ffn(xp, (ya, yb, yc, yd), norm_g, mod, w_out_bf, w1_bf, w2_bf, l, ctx_row)
        nat_out.append(cnat.reshape(n_ctx, 2, t_ctx, N_HEADS, HEAD_DIM))
        swa_out.append(cswa.reshape(n_ctx, 2, t_ctx, SWA_KV_HEADS, HEAD_DIM))
        rw_out.append(_state_from_scan(s_rw, n_ctx, True))
        hg_out.append(_state_from_scan(s_hg, n_ctx, False))

        p_rw, p_nat, p_hg, p_swa = _in_projection(xs, norm_g, mod, lat_slabs, l, lat_row, (False,) * 4)
        ya, yc, _, _ = _recurrent_mixers(p_rw, p_hg, lw, seg, n_lat, t_lat, 0,
                                         (state_rwkv[:, l], state_hgrn[:, l]))
        yb = _nat_latent(p_nat, cache_nat, _nat_bias_table(nat_rpb[l]), l, n_lat, t_lat, 0)
        yd = _swa_latent(p_swa, cache_swa, sink, rope, l, n_lat, t_lat, 0)
        xs = _out_projection_ffn(xs, (ya, yb, yc, yd), norm_g, mod, w_out_bf, w1_bf, w2_bf, l, lat_row)

    return (xp.reshape(t_ctx, n_ctx, D_MODEL).transpose(1, 0, 2), xs.reshape(x_sample.shape),
            jnp.stack(nat_out, axis=1), jnp.stack(swa_out, axis=1),
            jnp.stack(rw_out, axis=1), jnp.stack(hg_out, axis=1))
```

```python
import functools

import numpy as np
import jax
import jax.numpy as jnp
from jax import lax
from jax.experimental import pallas as pl
from jax.experimental.pallas import tpu as pltpu

F32 = jnp.float32
BF16 = jnp.bfloat16
HIGHEST = lax.Precision.HIGHEST

D_MODEL = 1024
DEPTH = 4
GRID_W = 64
HEAD_DIM = 64
N_HEADS = 4
MIX_W = N_HEADS * HEAD_DIM
SWA_KV_HEADS = 2
LORA_RANK = 64
GATE_RANK = 128
RWKV_DECAY_SCALE = 0.6065306597126334
RWKV_LN_EPS = 64e-5
NAT_KH = 8
NAT_KW = 16
SWA_WINDOW = 128
SWA_BLOCK = 128
ROPE_BASE = 10000.0
FFN_HIDDEN = 4 * D_MODEL
NORM_EPS = 1e-6
MASK_VALUE = -1e30
N_MOD = 6
ATTN_SCALE = HEAD_DIM ** -0.5

RW_COLS = 3 * MIX_W + GATE_RANK + 4 * LORA_RANK
NAT_COLS = 3 * MIX_W
HG_COLS = 5 * MIX_W
SWA_COLS = MIX_W + 2 * SWA_KV_HEADS * HEAD_DIM
SLABS = (RW_COLS, NAT_COLS, HG_COLS, SWA_COLS)
D_IN = sum(SLABS)

LANES = 128
SUBLANES = 8
ROW_TILE = 256
MOD_ROWS = 8
SCAN_KEY_CHUNK = 16


def _sigmoid(x):
    return 1.0 / (1.0 + jnp.exp(-x))


def _rms_rows(x, g):
    ms = jnp.mean(x * x, axis=-1, keepdims=True)
    return x * lax.rsqrt(ms + NORM_EPS) * g


def _bdot(a, b):
    return jnp.dot(a.astype(BF16), b.astype(BF16), preferred_element_type=F32)


def _bdot_nt(a, b):
    return lax.dot_general(a.astype(BF16), b.astype(BF16), (((1,), (1,)), ((), ())),
                           preferred_element_type=F32)


def _seg_sum(x, seg):
    return jnp.dot(x, seg, preferred_element_type=F32, precision=HIGHEST)


def _exact_select_dot(x, sel):
    hi = x.astype(BF16)
    r1 = x - hi.astype(F32)
    mid = r1.astype(BF16)
    lo = (r1 - mid.astype(F32)).astype(BF16)
    dot = lambda a: jnp.dot(a, sel, preferred_element_type=F32)
    return dot(hi) + dot(mid) + dot(lo)


def _params(*sem):
    return pltpu.CompilerParams(dimension_semantics=sem)


def _resident(shape, index_map):
    return pl.BlockSpec(shape, index_map, pipeline_mode=pl.Buffered(1))


def _mod_kernel(cond_ref, w_ref, b_ref, o_ref):
    cnd = cond_ref[...]
    s = cnd * _sigmoid(cnd)
    o_ref[...] = _bdot(s, w_ref[...]) + b_ref[...]


def _modulation_all(cond, mod_w, mod_b):
    tn = 1536
    n = N_MOD * D_MODEL
    return pl.pallas_call(
        _mod_kernel,
        grid=(DEPTH, n // tn),
        in_specs=[pl.BlockSpec((MOD_ROWS, D_MODEL), lambda l, j: (0, 0)),
                  pl.BlockSpec((None, D_MODEL, tn), lambda l, j: (l, 0, j)),
                  pl.BlockSpec((None, 1, tn), lambda l, j: (l, 0, j))],
        out_specs=pl.BlockSpec((None, MOD_ROWS, tn), lambda l, j: (l, 0, j)),
        out_shape=jax.ShapeDtypeStruct((DEPTH, MOD_ROWS, n), F32),
        compiler_params=_params("parallel", "parallel"),
        name="modulation",
    )(cond, mod_w, mod_b.reshape(DEPTH, 1, n))


def _mod_row(seq_len, first_row):
    if first_row == 0:
        return lambda i: 0
    return lambda i: first_row + (i * ROW_TILE) // seq_len


SEQ_TILE_STEPS = SUBLANES


def _tile_perm(n_seq, to_seq_major):
    r = np.arange(ROW_TILE)
    src = (r % SEQ_TILE_STEPS) * n_seq + r // SEQ_TILE_STEPS
    p = (src[:, None] == r[None, :])
    return jnp.asarray(p if to_seq_major else p.T, dtype=BF16)


def _permute_rows(perm_ref, xb):
    return jnp.dot(perm_ref[...], xb, preferred_element_type=F32).astype(BF16)


def _inproj_kernel(*refs, row_fn, modes):
    x_ref, g_ref, mod_ref = refs[0:3]
    w_refs = refs[3:7]
    perm_ref = refs[7] if "seq" in modes else None
    o_refs = refs[-4:]
    row = row_fn(pl.program_id(0))
    shift = mod_ref[pl.ds(row, 1), 0:D_MODEL]
    scale = mod_ref[pl.ds(row, 1), D_MODEL:2 * D_MODEL]
    h = _rms_rows(x_ref[...], g_ref[0:1, :]) * (1.0 + scale) + shift
    hb = h.astype(BF16)
    hb_seq = _permute_rows(perm_ref, hb) if perm_ref is not None else None
    for w_ref, o_ref, mode in zip(w_refs, o_refs, modes):
        if mode == "fm":
            o_ref[...] = lax.dot_general(w_ref[...], hb, (((1,), (1,)), ((), ())),
                                         preferred_element_type=F32)
        elif mode == "seq":
            o_ref[...] = jnp.dot(hb_seq, w_ref[...], preferred_element_type=F32).reshape(o_ref.shape)
        else:
            o_ref[...] = jnp.dot(hb, w_ref[...], preferred_element_type=F32)


def _in_projection(x, norm_g, mod, w_slabs, layer, row_fn, modes, n_seq=None):
    n = x.shape[0]
    out_specs, out_shape = [], []
    for width, mode in zip(SLABS, modes):
        if mode == "fm":
            out_specs.append(pl.BlockSpec((width, ROW_TILE), lambda i: (0, i)))
            out_shape.append(jax.ShapeDtypeStruct((width, n), F32))
        elif mode == "seq":
            out_specs.append(pl.BlockSpec((n_seq, SEQ_TILE_STEPS, width), lambda i: (0, i, 0)))
            out_shape.append(jax.ShapeDtypeStruct((n_seq, n // n_seq, width), F32))
        else:
            out_specs.append(pl.BlockSpec((ROW_TILE, width), lambda i: (i, 0)))
            out_shape.append(jax.ShapeDtypeStruct((n, width), F32))
    in_specs = [pl.BlockSpec((ROW_TILE, D_MODEL), lambda i: (i, 0)),
                pl.BlockSpec((None, 4, D_MODEL), lambda i: (layer, 0, 0)),
                pl.BlockSpec((None, MOD_ROWS, N_MOD * D_MODEL), lambda i: (layer, 0, 0))]
    in_specs += [_resident((None,) + w.shape[1:], lambda i: (layer, 0, 0)) for w in w_slabs]
    args = [x, norm_g, mod, *w_slabs]
    if "seq" in modes:
        assert n_seq * SEQ_TILE_STEPS == ROW_TILE
        in_specs.append(pl.BlockSpec((ROW_TILE, ROW_TILE), lambda i: (0, 0)))
        args.append(_tile_perm(n_seq, True))
    return pl.pallas_call(
        functools.partial(_inproj_kernel, row_fn=row_fn, modes=modes),
        grid=(n // ROW_TILE,),
        in_specs=in_specs,
        out_specs=out_specs,
        out_shape=out_shape,
        compiler_params=_params("parallel"),
        name="in_projection",
    )(*args)


def _outproj_ffn_kernel(*refs, row_fn, modes):
    x_ref = refs[0]
    y_refs = refs[1:5]
    g_ref, mod_ref, wo_ref, w1_ref, w2_ref = refs[5:10]
    perm_ref = refs[10] if "seq" in modes else None
    o_ref = refs[-1]
    row = row_fn(pl.program_id(0))
    mod = lambda j: mod_ref[pl.ds(row, 1), j * D_MODEL:(j + 1) * D_MODEL]
    x = x_ref[...]
    acc = None
    for j, (y_ref, mode) in enumerate(zip(y_refs, modes)):
        yb = y_ref[...].reshape(ROW_TILE, MIX_W).astype(BF16)
        if mode == "seq":
            yb = _permute_rows(perm_ref, yb)
        part = jnp.dot(yb, wo_ref[j * MIX_W:(j + 1) * MIX_W, :], preferred_element_type=F32)
        acc = part if acc is None else acc + part
    x = x + mod(2) * _rms_rows(acc, g_ref[1:2, :])
    h = (_rms_rows(x, g_ref[2:3, :]) * (1.0 + mod(4)) + mod(3)).astype(BF16)
    f = None
    for j in range(FFN_HIDDEN // D_MODEL):
        u = jnp.dot(h, w1_ref[:, j * D_MODEL:(j + 1) * D_MODEL], preferred_element_type=F32)
        u = jnp.square(jnp.maximum(u, 0.0)).astype(BF16)
        part = jnp.dot(u, w2_ref[j * D_MODEL:(j + 1) * D_MODEL, :], preferred_element_type=F32)
        f = part if f is None else f + part
    o_ref[...] = x + mod(5) * _rms_rows(f, g_ref[3:4, :])


def _out_projection_ffn(x, ys, norm_g, mod, w_out_bf, w1_bf, w2_bf, layer, row_fn, modes, n_seq=None):
    n = x.shape[0]
    tile = lambda w: pl.BlockSpec((ROW_TILE, w), lambda i: (i, 0))
    y_specs = [pl.BlockSpec((n_seq, SEQ_TILE_STEPS, MIX_W), lambda i: (0, i, 0)) if m == "seq" else tile(MIX_W)
               for m in modes]
    in_specs = [tile(D_MODEL)] + y_specs + [
        pl.BlockSpec((None, 4, D_MODEL), lambda i: (layer, 0, 0)),
        pl.BlockSpec((None, MOD_ROWS, N_MOD * D_MODEL), lambda i: (layer, 0, 0)),
        _resident((None, D_MODEL, D_MODEL), lambda i: (layer, 0, 0)),
        _resident((None, D_MODEL, FFN_HIDDEN), lambda i: (layer, 0, 0)),
        _resident((None, FFN_HIDDEN, D_MODEL), lambda i: (layer, 0, 0))]
    args = [x, *ys, norm_g, mod, w_out_bf, w1_bf, w2_bf]
    if "seq" in modes:
        in_specs.append(pl.BlockSpec((ROW_TILE, ROW_TILE), lambda i: (0, 0)))
        args.append(_tile_perm(n_seq, False))
    return pl.pallas_call(
        functools.partial(_outproj_ffn_kernel, row_fn=row_fn, modes=modes),
        grid=(n // ROW_TILE,),
        in_specs=in_specs,
        out_specs=tile(D_MODEL),
        out_shape=jax.ShapeDtypeStruct((n, D_MODEL), F32),
        compiler_params=_params("parallel"),
        name="out_projection_ffn",
    )(*args)


def _scan_kernel(*refs, n_rows, delta, steps, v_tiles, expand, reverse_odd):
    if expand:
        rows_ref, v_ref, s0_ref, e_ref, y_ref, sfin_ref, s_scr, x_scr = refs
    else:
        rows_ref, v_ref, s0_ref, y_ref, sfin_ref, s_scr = refs
        x_scr = rows_ref
    tb = pl.program_id(1)

    @pl.when(tb == 0)
    def _():
        s_scr[...] = s0_ref[...]

    if expand:
        for j in range(n_rows):
            flat = rows_ref[j].reshape(steps * HEAD_DIM, rows_ref.shape[-1])
            x_scr[j] = _exact_select_dot(flat, e_ref[...]).reshape(steps, HEAD_DIM, LANES)

    n_part = max(1, 4 // v_tiles)
    tiles = [slice(i * SUBLANES, (i + 1) * SUBLANES) for i in range(v_tiles)]
    if delta:
        i_nkk, i_w, i_kka, i_kv, i_r = range(5)
    else:
        i_w, i_kv, i_r = range(3)

    k_chunk = SCAN_KEY_CHUNK if v_tiles > 1 else HEAD_DIM
    zero = jnp.zeros((SUBLANES, LANES), F32)
    acc0 = tuple(zero for _ in range(v_tiles * n_part))

    def over_keys(body):
        def chunk(c, acc):
            acc = list(acc)
            for kk in range(k_chunk):
                acc = body(c * k_chunk + kk, kk, acc)
            return tuple(acc)
        if k_chunk == HEAD_DIM:
            acc = chunk(0, acc0)
        else:
            acc = lax.fori_loop(0, HEAD_DIM // k_chunk, chunk, acc0)
        sums = []
        for i in range(v_tiles):
            parts = list(acc[i * n_part:(i + 1) * n_part])
            while len(parts) > 1:
                parts = [a + b for a, b in zip(parts[0::2], parts[1::2])]
            sums.append(parts[0])
        return sums

    backward = (pl.program_id(0) % 2 == 1) if reverse_odd else None

    def step(i_step, carry):
        t = jnp.where(backward, steps - 1 - i_step, i_step) if reverse_odd else i_step
        row = lambda j, k: x_scr[j, t, pl.ds(k, 1), :]
        vt = [v_ref[t, tl, :] for tl in tiles]
        if delta:
            def sa_body(k, kk, acc):
                nk = row(i_nkk, k)
                for i, tl in enumerate(tiles):
                    p = i * n_part + kk % n_part
                    acc[p] = acc[p] + s_scr[k, tl, :] * nk
                return acc
            sa = over_keys(sa_body)

        def update_body(k, kk, acc):
            wk, kvk, rk = row(i_w, k), row(i_kv, k), row(i_r, k)
            if delta:
                kkak = row(i_kka, k)
            for i, tl in enumerate(tiles):
                s = s_scr[k, tl, :] * wk + vt[i] * kvk
                if delta:
                    s = s + sa[i] * kkak
                s_scr[k, tl, :] = s
                p = i * n_part + kk % n_part
                acc[p] = acc[p] + s * rk
            return acc
        y = over_keys(update_body)
        for i, tl in enumerate(tiles):
            y_ref[t, tl, :] = y[i]
        return carry

    lax.fori_loop(0, steps, step, 0)

    @pl.when(tb == pl.num_programs(1) - 1)
    def _():
        sfin_ref[...] = s_scr[...]


def _scan(rows, v, s0, *, delta, steps, reverse_odd=False):
    groups, _, t_len, _, n_chain = rows.shape
    n_rows = 5 if delta else 3
    n_blocks = t_len // steps
    tblk = (lambda g, t: jnp.where(g % 2 == 1, n_blocks - 1 - t, t)) if reverse_odd else (lambda g, t: t)
    if v is None:
        vrows = HEAD_DIM
        v_spec = pl.BlockSpec((None, None, steps, vrows, LANES), lambda g, t: (g, n_rows, tblk(g, t), 0, 0))
        v = rows
    else:
        vrows = v.shape[2]
        v_spec = pl.BlockSpec((None, steps, vrows, LANES), lambda g, t: (g, tblk(g, t), 0, 0))
    v_tiles = vrows // SUBLANES
    expand = n_chain != LANES
    in_specs = [pl.BlockSpec((None, n_rows, steps, HEAD_DIM, n_chain), lambda g, t: (g, 0, tblk(g, t), 0, 0)),
                v_spec,
                pl.BlockSpec((None, HEAD_DIM, vrows, LANES), lambda g, t: (g, 0, 0, 0))]
    args = [rows, v, s0]
    scratch = [pltpu.VMEM((HEAD_DIM, vrows, LANES), F32)]
    if expand:
        lane = np.arange(LANES)
        e = (lane[None, :] % n_chain == np.arange(n_chain)[:, None]).astype(np.float32)
        in_specs.append(pl.BlockSpec((n_chain, LANES), lambda g, t: (0, 0)))
        args.append(jnp.asarray(e, dtype=BF16))
        scratch.append(pltpu.VMEM((n_rows, steps, HEAD_DIM, LANES), F32))
    return pl.pallas_call(
        functools.partial(_scan_kernel, n_rows=n_rows, delta=delta, steps=steps, v_tiles=v_tiles,
                          expand=expand, reverse_odd=reverse_odd),
        grid=(groups, n_blocks),
        in_specs=in_specs,
        out_specs=[pl.BlockSpec((None, steps, vrows, LANES), lambda g, t: (g, tblk(g, t), 0, 0)),
                   pl.BlockSpec((None, HEAD_DIM, vrows, LANES), lambda g, t: (g, 0, 0, 0))],
        out_shape=[jax.ShapeDtypeStruct((groups, t_len, vrows, LANES), F32),
                   jax.ShapeDtypeStruct((groups, HEAD_DIM, vrows, LANES), F32)],
        scratch_shapes=scratch,
        compiler_params=_params("parallel", "arbitrary"),
        name="rwkv_scan" if delta else "hgrn_scan",
    )(*args)


def _to_scan_layout(a, n_seq, t_len, per_dir_groups):
    n_arr = a.shape[1]
    a = a.reshape(2, n_arr, n_seq, t_len, N_HEADS, HEAD_DIM)
    a = jnp.stack([a[0], a[1][:, :, ::-1]])
    if per_dir_groups:
        r = a.transpose(0, 1, 3, 5, 2, 4).reshape(2, n_arr, t_len, HEAD_DIM, n_seq * N_HEADS)
        return r[:, :n_arr - 1], r[:, n_arr - 1]
    n_chain = 2 * n_seq * N_HEADS
    vg = LANES // n_chain
    rows = a[:, :n_arr - 1].transpose(1, 3, 5, 0, 2, 4).reshape(1, n_arr - 1, t_len, HEAD_DIM, n_chain)
    v = a[:, n_arr - 1].reshape(2, n_seq, t_len, N_HEADS, vg, HEAD_DIM // vg)
    v = v.transpose(2, 5, 4, 0, 1, 3).reshape(1, t_len, HEAD_DIM // vg, LANES)
    return rows, v


def _from_scan_layout(y, n_seq, t_len, per_dir_groups):
    if per_dir_groups:
        y = y.reshape(2, t_len, HEAD_DIM, n_seq, N_HEADS)
        y = y[0] + y[1][::-1]
        return y.transpose(2, 0, 3, 1).reshape(n_seq * t_len, MIX_W)
    n_chain = 2 * n_seq * N_HEADS
    vg = LANES // n_chain
    y = y.reshape(t_len, HEAD_DIM // vg, vg, 2, n_seq, N_HEADS)
    y = y[:, :, :, 0] + y[::-1, :, :, 1]
    return y.transpose(3, 0, 4, 2, 1).reshape(n_seq * t_len, MIX_W)


def _state_to_scan(s, key_last):
    n_seq = s.shape[0]
    vg = LANES // (2 * n_seq * N_HEADS)
    if key_last:
        s = s.reshape(n_seq, 2, N_HEADS, vg, HEAD_DIM // vg, HEAD_DIM).transpose(5, 4, 3, 1, 0, 2)
    else:
        s = s.reshape(n_seq, 2, N_HEADS, HEAD_DIM, vg, HEAD_DIM // vg).transpose(3, 5, 4, 1, 0, 2)
    return s.reshape(1, HEAD_DIM, HEAD_DIM // vg, LANES)


def _state_from_scan(s, n_seq, key_last):
    s = s.reshape(2, HEAD_DIM, HEAD_DIM, N_HEADS, n_seq)
    return s.transpose(4, 0, 3, 2, 1) if key_last else s.transpose(4, 0, 3, 1, 2)


def _shifted(x, d, t_len):
    row = lax.broadcasted_iota(jnp.int32, (t_len, 1), 0)
    prev_f = jnp.where(row == 0, 0.0, pltpu.roll(x, 1, 0))
    prev_b = jnp.where(row == t_len - 1, 0.0, pltpu.roll(x, t_len - 1, 0))
    return jnp.where(d == 0, prev_f, prev_b)


def _rwkv_pre_kernel(rkv_ref, lora_ref, mu_rkv_ref, mu_lo_ref, wl_ref, b0_ref, kk_ref, ka_ref, rk_ref,
                     seg_ref, scan_ref, bonus_ref, *, t_len):
    d = pl.program_id(1)
    x = rkv_ref[...]
    x = x + (_shifted(x, d, t_len) - x) * mu_rkv_ref[...]
    r, k, v = x[:, 0:MIX_W], x[:, MIX_W:2 * MIX_W], x[:, 2 * MIX_W:3 * MIX_W]
    lo = lora_ref[...]
    lo = lo + (_shifted(lo, d, t_len) - lo) * mu_lo_ref[...]
    lane = lax.broadcasted_iota(jnp.int32, lo.shape, 1)
    lo = jnp.where(lane < LORA_RANK, jnp.tanh(lo), lo)
    pre = _bdot(lo, wl_ref[...]) + b0_ref[...]
    w = jnp.exp(-RWKV_DECAY_SCALE * _sigmoid(pre[:, 0:MIX_W]))
    a = _sigmoid(pre[:, MIX_W:2 * MIX_W])
    seg = seg_ref[...]
    kk = k * kk_ref[...]
    kk = kk / jnp.maximum(jnp.sqrt(_seg_sum(kk * kk, seg)), 1e-12)
    k = k * (1.0 + (a - 1.0) * ka_ref[...])
    scan_ref[0] = -kk
    scan_ref[1] = w
    scan_ref[2] = kk * a
    scan_ref[3] = k
    scan_ref[4] = r
    scan_ref[5] = v
    bonus_ref[...] = _seg_sum(r * k * rk_ref[...], seg) * v


def _rwkv_pre(p_rw, lw, seg, n_seq, t_len, seq0):
    n = n_seq * t_len
    lora_blk = (3 * MIX_W + GATE_RANK) // (2 * LORA_RANK)
    vec = lambda w: pl.BlockSpec((None, 1, w), lambda i, d: (d, 0, 0))
    shared = lambda w: pl.BlockSpec((1, w), lambda i, d: (0, 0))
    return pl.pallas_call(
        functools.partial(_rwkv_pre_kernel, t_len=t_len),
        grid=(n_seq, 2),
        in_specs=[pl.BlockSpec((t_len, 3 * MIX_W), lambda i, d: (seq0 + i, 0)),
                  pl.BlockSpec((t_len, 2 * LORA_RANK), lambda i, d: (seq0 + i, lora_blk + d)),
                  vec(3 * MIX_W), vec(2 * LORA_RANK),
                  pl.BlockSpec((None, 2 * LORA_RANK, 2 * MIX_W), lambda i, d: (d, 0, 0)),
                  vec(2 * MIX_W), shared(MIX_W), shared(MIX_W), shared(MIX_W),
                  pl.BlockSpec((MIX_W, MIX_W), lambda i, d: (0, 0))],
        out_specs=[pl.BlockSpec((None, 6, t_len, MIX_W), lambda i, d: (d, 0, i, 0)),
                   pl.BlockSpec((None, t_len, MIX_W), lambda i, d: (d, i, 0))],
        out_shape=[jax.ShapeDtypeStruct((2, 6, n, MIX_W), F32),
                   jax.ShapeDtypeStruct((2, n, MIX_W), F32)],
        compiler_params=_params("parallel", "parallel"),
        name="rwkv_pre",
    )(p_rw, p_rw, lw["mu_rkv"], lw["mu_lo"], lw["w_lora"], lw["b_lora"], lw["rw_kk"], lw["rw_ka"],
      lw["rw_rk"], seg)


def _rwkv_post_kernel(y_ref, bonus_ref, gh_ref, g2_ref, lnw_ref, lnb_ref, seg_ref, o_ref):
    seg = seg_ref[...]
    y = y_ref[...]
    mu = _seg_sum(y, seg) * (1.0 / HEAD_DIM)
    yc = y - mu
    var = _seg_sum(yc * yc, seg) * (1.0 / HEAD_DIM)
    y = yc * lax.rsqrt(var + RWKV_LN_EPS) * lnw_ref[...] + lnb_ref[...]
    y = y + bonus_ref[0] + bonus_ref[1]
    o_ref[...] = y * _bdot(_sigmoid(gh_ref[...]), g2_ref[...])


def _rwkv_post(y, bonus, p_rw, lw, seg, tile0):
    n = y.shape[0]
    gh_blk = 3 * MIX_W // GATE_RANK
    shared = lambda w: pl.BlockSpec((1, w), lambda i: (0, 0))
    return pl.pallas_call(
        _rwkv_post_kernel,
        grid=(n // ROW_TILE,),
        in_specs=[pl.BlockSpec((ROW_TILE, MIX_W), lambda i: (i, 0)),
                  pl.BlockSpec((2, ROW_TILE, MIX_W), lambda i: (0, i, 0)),
                  pl.BlockSpec((ROW_TILE, GATE_RANK), lambda i: (tile0 + i, gh_blk)),
                  pl.BlockSpec((GATE_RANK, MIX_W), lambda i: (0, 0)),
                  shared(MIX_W), shared(MIX_W),
                  pl.BlockSpec((MIX_W, MIX_W), lambda i: (0, 0))],
        out_specs=pl.BlockSpec((ROW_TILE, MIX_W), lambda i: (i, 0)),
        out_shape=jax.ShapeDtypeStruct((n, MIX_W), F32),
        compiler_params=_params("parallel"),
        name="rwkv_post",
    )(y, bonus, p_rw, lw["rw_g2"], lw["rw_lnx_w"], lw["rw_lnx_b"], seg)


def _hgrn_pre_kernel(qv_ref, f_ref, lb_ref, scan_ref):
    q = qv_ref[:, 0:MIX_W]
    f_raw = f_ref[...]
    lb = lb_ref[...]
    scan_ref[0] = lb + (1.0 - lb) * _sigmoid(f_raw)
    scan_ref[1] = (1.0 - lb) * _sigmoid(-f_raw)
    scan_ref[2] = q * _sigmoid(q)
    scan_ref[3] = qv_ref[:, MIX_W:2 * MIX_W]


def _hgrn_pre(p_hg, lb, n_tok, tile0):
    return pl.pallas_call(
        _hgrn_pre_kernel,
        grid=(n_tok // ROW_TILE, 2),
        in_specs=[pl.BlockSpec((ROW_TILE, 2 * MIX_W), lambda i, d: (tile0 + i, 0)),
                  pl.BlockSpec((ROW_TILE, MIX_W), lambda i, d: (tile0 + i, 3 + d)),
                  pl.BlockSpec((None, 1, MIX_W), lambda i, d: (d, 0, 0))],
        out_specs=pl.BlockSpec((None, 4, ROW_TILE, MIX_W), lambda i, d: (d, 0, i, 0)),
        out_shape=jax.ShapeDtypeStruct((2, 4, n_tok, MIX_W), F32),
        compiler_params=_params("parallel", "parallel"),
        name="hgrn_pre",
    )(p_hg, p_hg, lb)


def _hgrn_post_kernel(o_ref, g_ref, nw_ref, seg_ref, out_ref):
    o = o_ref[...]
    ms = _seg_sum(o * o, seg_ref[...]) * (1.0 / HEAD_DIM)
    out_ref[...] = o * lax.rsqrt(ms + NORM_EPS) * nw_ref[...] * _sigmoid(g_ref[...])


def _hgrn_post(o, p_hg, hg_norm, seg, tile0):
    n = o.shape[0]
    return pl.pallas_call(
        _hgrn_post_kernel,
        grid=(n // ROW_TILE,),
        in_specs=[pl.BlockSpec((ROW_TILE, MIX_W), lambda i: (i, 0)),
                  pl.BlockSpec((ROW_TILE, MIX_W), lambda i: (tile0 + i, 2)),
                  pl.BlockSpec((1, MIX_W), lambda i: (0, 0)),
                  pl.BlockSpec((MIX_W, MIX_W), lambda i: (0, 0))],
        out_specs=pl.BlockSpec((ROW_TILE, MIX_W), lambda i: (i, 0)),
        out_shape=jax.ShapeDtypeStruct((n, MIX_W), F32),
        compiler_params=_params("parallel"),
        name="hgrn_post",
    )(o, p_hg, hg_norm, seg)


CTX_TILE_STEPS = ROW_TILE // 32


def _fm_to_scan(a, n_seq):
    grp = lax.broadcasted_iota(jnp.int32, (HEAD_DIM, LANES), 1) // n_seq
    rolled = []
    for h in range(N_HEADS):
        blk = a[h * HEAD_DIM:(h + 1) * HEAD_DIM]
        rolled.append([blk if s == 0 else pltpu.roll(blk, s * n_seq, 1) for s in range(N_HEADS)])
    tiles = []
    for t in range(N_HEADS):
        out = rolled[0][(0 - t) % N_HEADS]
        for h in range(1, N_HEADS):
            out = jnp.where(grp == h, rolled[h][(h - t) % N_HEADS], out)
        tiles.append(out)
    return tiles


def _scan_to_fm(tiles, n_seq):
    grp = lax.broadcasted_iota(jnp.int32, (HEAD_DIM, LANES), 1) // n_seq
    rolled = [[tl if s == 0 else pltpu.roll(tl, s * n_seq, 1) for s in range(N_HEADS)] for tl in tiles]
    blocks = []
    for h in range(N_HEADS):
        out = rolled[0][(0 - h) % N_HEADS]
        for t in range(1, N_HEADS):
            out = jnp.where(grp == t, rolled[t][(t - h) % N_HEADS], out)
        blocks.append(out)
    return jnp.concatenate(blocks, axis=0)


def _store_scan_tiles(scan_ref, d, j, a, n_seq):
    for c in range(a.shape[1] // LANES):
        for t, tile in enumerate(_fm_to_scan(a[:, c * LANES:(c + 1) * LANES], n_seq)):
            scan_ref[d, j, c * N_HEADS + t] = tile


def _load_scan_tiles(y_ref, n_seq):
    cols = []
    for c in range(y_ref.shape[1] // N_HEADS):
        tiles = [y_ref[0, c * N_HEADS + t] + y_ref[1, c * N_HEADS + t] for t in range(N_HEADS)]
        cols.append(_scan_to_fm(tiles, n_seq))
    return jnp.concatenate(cols, axis=1)


def _head_sum_rows(x):
    w = x.shape[1]
    s = jnp.sum(x.reshape(N_HEADS, HEAD_DIM, w), axis=1, keepdims=True)
    return jnp.broadcast_to(s, (N_HEADS, HEAD_DIM, w)).reshape(N_HEADS * HEAD_DIM, w)


def _lane_halo(halo, shift, valid, width):
    h = jnp.where(valid, pltpu.roll(halo, shift, 1), 0.0)
    return jnp.concatenate([h] * (width // LANES), axis=1)


def _prev_step(x, halo, valid, n_seq):
    width = x.shape[1]
    lane = lax.broadcasted_iota(jnp.int32, (1, width), 1)
    return jnp.where(lane < n_seq, _lane_halo(halo, n_seq, valid, width), pltpu.roll(x, n_seq, 1))


def _next_step(x, halo, valid, n_seq):
    width = x.shape[1]
    lane = lax.broadcasted_iota(jnp.int32, (1, width), 1)
    return jnp.where(lane >= width - n_seq, _lane_halo(halo, LANES - n_seq, valid, width),
                     pltpu.roll(x, width - n_seq, 1))


def _rwkv_pre_fm_kernel(x_ref, hp_ref, hn_ref, mu_rkv_ref, mu_lo_ref, w2t_ref, a2t_ref, w0_ref, a0_ref,
                        kk_ref, ka_ref, rk_ref, scan_ref, bonus_ref, *, n_seq):
    i = pl.program_id(0)
    last = pl.num_programs(0) - 1
    rkv_rows = slice(0, 3 * MIX_W)
    x = x_ref[rkv_rows, :]
    bonus = None
    for d in range(2):
        if d == 0:
            shifted = lambda a, rows: _prev_step(a, hp_ref[rows, :], i > 0, n_seq)
        else:
            shifted = lambda a, rows: _next_step(a, hn_ref[rows, :], i < last, n_seq)
        x3 = x + (shifted(x, rkv_rows) - x) * mu_rkv_ref[d]
        r, k, v = x3[0:MIX_W], x3[MIX_W:2 * MIX_W], x3[2 * MIX_W:3 * MIX_W]
        lo_rows = slice(3 * MIX_W + GATE_RANK + 2 * LORA_RANK * d, 3 * MIX_W + GATE_RANK + 2 * LORA_RANK * (d + 1))
        lo = x_ref[lo_rows, :]
        lo = lo + (shifted(lo, lo_rows) - lo) * mu_lo_ref[d]
        w_pre = _bdot(w2t_ref[d], jnp.tanh(lo[0:LORA_RANK])) + w0_ref[d]
        a_pre = _bdot(a2t_ref[d], lo[LORA_RANK:2 * LORA_RANK]) + a0_ref[d]
        w = jnp.exp(-RWKV_DECAY_SCALE * _sigmoid(w_pre))
        a = _sigmoid(a_pre)
        kk = k * kk_ref[...]
        kk = kk / jnp.maximum(jnp.sqrt(_head_sum_rows(kk * kk)), 1e-12)
        k = k * (1.0 + (a - 1.0) * ka_ref[...])
        for j, arr in enumerate((-kk, w, kk * a, k, r, v)):
            _store_scan_tiles(scan_ref, d, j, arr, n_seq)
        b = _head_sum_rows(r * k * rk_ref[...]) * v
        bonus = b if bonus is None else bonus + b
    bonus_ref[...] = bonus


def _rwkv_pre_fm(pt_rw, lw, n_seq, t_len):
    n = pt_rw.shape[1]
    n_tiles = n // ROW_TILE
    per_tile = ROW_TILE // LANES
    col = lambda rows: pl.BlockSpec((rows, 1), lambda i: (0, 0))
    dcol = lambda rows: pl.BlockSpec((2, rows, 1), lambda i: (0, 0, 0))
    return pl.pallas_call(
        functools.partial(_rwkv_pre_fm_kernel, n_seq=n_seq),
        grid=(n_tiles,),
        in_specs=[pl.BlockSpec((RW_COLS, ROW_TILE), lambda i: (0, i)),
                  pl.BlockSpec((RW_COLS, LANES), lambda i: (0, jnp.maximum(i * per_tile - 1, 0))),
                  pl.BlockSpec((RW_COLS, LANES),
                               lambda i: (0, jnp.minimum((i + 1) * per_tile, n_tiles * per_tile - 1))),
                  dcol(3 * MIX_W), dcol(2 * LORA_RANK),
                  pl.BlockSpec((2, MIX_W, LORA_RANK), lambda i: (0, 0, 0)),
                  pl.BlockSpec((2, MIX_W, LORA_RANK), lambda i: (0, 0, 0)),
                  dcol(MIX_W), dcol(MIX_W), col(MIX_W), col(MIX_W), col(MIX_W)],
        out_specs=[pl.BlockSpec((2, 6, CTX_TILE_STEPS, HEAD_DIM, LANES), lambda i: (0, 0, i, 0, 0)),
                   pl.BlockSpec((MIX_W, ROW_TILE), lambda i: (0, i))],
        out_shape=[jax.ShapeDtypeStruct((2, 6, t_len, HEAD_DIM, LANES), F32),
                   jax.ShapeDtypeStruct((MIX_W, n), F32)],
        compiler_params=_params("parallel"),
        name="rwkv_pre_fm",
    )(pt_rw, pt_rw, pt_rw, lw["mu_rkv_col"], lw["mu_lo_col"], lw["w2t"], lw["a2t"], lw["w0_col"],
      lw["a0_col"], lw["kk_col"], lw["ka_col"], lw["rk_col"])


def _rwkv_post_fm_kernel(y_ref, bonus_ref, gh_ref, g2t_ref, lnw_ref, lnb_ref, o_ref, *, n_seq):
    y = _load_scan_tiles(y_ref, n_seq)
    mu = _head_sum_rows(y) * (1.0 / HEAD_DIM)
    yc = y - mu
    var = _head_sum_rows(yc * yc) * (1.0 / HEAD_DIM)
    y = yc * lax.rsqrt(var + RWKV_LN_EPS) * lnw_ref[...] + lnb_ref[...] + bonus_ref[...]
    g = _bdot(g2t_ref[...], _sigmoid(gh_ref[...]))
    o_ref[...] = (y * g).T


def _rwkv_post_fm(y, bonus, pt_rw, lw, n_seq):
    n = bonus.shape[1]
    col = pl.BlockSpec((MIX_W, 1), lambda i: (0, 0))
    return pl.pallas_call(
        functools.partial(_rwkv_post_fm_kernel, n_seq=n_seq),
        grid=(n // ROW_TILE,),
        in_specs=[pl.BlockSpec((2, CTX_TILE_STEPS, HEAD_DIM, LANES), lambda i: (0, i, 0, 0)),
                  pl.BlockSpec((MIX_W, ROW_TILE), lambda i: (0, i)),
                  pl.BlockSpec((GATE_RANK, ROW_TILE), lambda i: (3 * MIX_W // GATE_RANK, i)),
                  pl.BlockSpec((MIX_W, GATE_RANK), lambda i: (0, 0)),
                  col, col],
        out_specs=pl.BlockSpec((ROW_TILE, MIX_W), lambda i: (i, 0)),
        out_shape=jax.ShapeDtypeStruct((n, MIX_W), F32),
        compiler_params=_params("parallel"),
        name="rwkv_post_fm",
    )(y, bonus, pt_rw, lw["g2t"], lw["lnw_col"], lw["lnb_col"])


def _hgrn_pre_fm_kernel(qv_ref, ff_ref, fb_ref, lb_ref, scan_ref, *, n_seq):
    q = qv_ref[0:MIX_W, :]
    q = q * _sigmoid(q)
    v = qv_ref[MIX_W:2 * MIX_W, :]
    for d, f_ref in enumerate((ff_ref, fb_ref)):
        f_raw = f_ref[...]
        lb = lb_ref[d]
        f = lb + (1.0 - lb) * _sigmoid(f_raw)
        kv = (1.0 - lb) * _sigmoid(-f_raw)
        for j, arr in enumerate((f, kv, q, v)):
            _store_scan_tiles(scan_ref, d, j, arr, n_seq)


def _hgrn_pre_fm(pt_hg, lb_col, n_seq, t_len):
    n = pt_hg.shape[1]
    return pl.pallas_call(
        functools.partial(_hgrn_pre_fm_kernel, n_seq=n_seq),
        grid=(n // ROW_TILE,),
        in_specs=[pl.BlockSpec((2 * MIX_W, ROW_TILE), lambda i: (0, i)),
                  pl.BlockSpec((MIX_W, ROW_TILE), lambda i: (3, i)),
                  pl.BlockSpec((MIX_W, ROW_TILE), lambda i: (4, i)),
                  pl.BlockSpec((2, MIX_W, 1), lambda i: (0, 0, 0))],
        out_specs=pl.BlockSpec((2, 4, CTX_TILE_STEPS, HEAD_DIM, LANES), lambda i: (0, 0, i, 0, 0)),
        out_shape=jax.ShapeDtypeStruct((2, 4, t_len, HEAD_DIM, LANES), F32),
        compiler_params=_params("parallel"),
        name="hgrn_pre_fm",
    )(pt_hg, pt_hg, pt_hg, lb_col)


def _hgrn_post_fm_kernel(o_ref, g_ref, nw_ref, out_ref, *, n_seq):
    o = _load_scan_tiles(o_ref, n_seq)
    ms = _head_sum_rows(o * o) * (1.0 / HEAD_DIM)
    out_ref[...] = (o * lax.rsqrt(ms + NORM_EPS) * nw_ref[...] * _sigmoid(g_ref[...])).T


def _hgrn_post_fm(o, pt_hg, nw_col, n_seq):
    n = pt_hg.shape[1]
    return pl.pallas_call(
        functools.partial(_hgrn_post_fm_kernel, n_seq=n_seq),
        grid=(n // ROW_TILE,),
        in_specs=[pl.BlockSpec((2, CTX_TILE_STEPS, HEAD_DIM, LANES), lambda i: (0, i, 0, 0)),
                  pl.BlockSpec((MIX_W, ROW_TILE), lambda i: (2, i)),
                  pl.BlockSpec((MIX_W, 1), lambda i: (0, 0))],
        out_specs=pl.BlockSpec((ROW_TILE, MIX_W), lambda i: (i, 0)),
        out_shape=jax.ShapeDtypeStruct((n, MIX_W), F32),
        compiler_params=_params("parallel"),
        name="hgrn_post_fm",
    )(o, pt_hg, nw_col)


def _ctx_recurrent_mixers(pt_rw, pt_hg, lw, n_seq, t_len):
    assert n_seq * N_HEADS == LANES
    s0 = jnp.zeros((2, HEAD_DIM, HEAD_DIM, LANES), F32)
    scan_in, bonus = _rwkv_pre_fm(pt_rw, lw, n_seq, t_len)
    y, s_rw = _scan(scan_in, None, s0, delta=True, steps=32, reverse_odd=True)
    ya = _rwkv_post_fm(y, bonus, pt_rw, lw, n_seq)
    scan_in = _hgrn_pre_fm(pt_hg, lw["hg_lb_col"], n_seq, t_len)
    o, s_hg = _scan(scan_in, None, s0, delta=False, steps=32, reverse_odd=True)
    yc = _hgrn_post_fm(o, pt_hg, lw["hg_norm_col"], n_seq)
    return ya, yc, s_rw, s_hg


def _head(x, h):
    return x[:, h * HEAD_DIM:(h + 1) * HEAD_DIM]


def _softmax_av(scores, values, sink=None):
    m = None
    for s in scores:
        mi = jnp.max(s, axis=-1, keepdims=True)
        m = mi if m is None else jnp.maximum(m, mi)
    if sink is not None:
        m = jnp.maximum(m, sink)
    den = None if sink is None else jnp.exp(sink - m)
    out = None
    for s, v in zip(scores, values):
        e = jnp.exp(s - m)
        di = jnp.sum(e, axis=-1, keepdims=True)
        den = di if den is None else den + di
        oi = _bdot(e, v)
        out = oi if out is None else out + oi
    return out / den


def _ctx_attn_kernel(sink_ref, p_ref, y_ref, cache_ref, *, kv_heads, use_sink):
    group = N_HEADS // kv_heads
    kv_w = kv_heads * HEAD_DIM
    q_all = p_ref[:, 0:MIX_W] * ATTN_SCALE
    k_all = p_ref[:, MIX_W:MIX_W + kv_w]
    v_all = p_ref[:, MIX_W + kv_w:MIX_W + 2 * kv_w]
    cache_ref[0] = k_all
    cache_ref[1] = v_all
    outs = []
    for h in range(N_HEADS):
        kvh = h // group
        outs.append(_softmax_av([_bdot_nt(_head(q_all, h), _head(k_all, kvh))], [_head(v_all, kvh)],
                                sink=sink_ref[h] if use_sink else None))
    y_ref[...] = jnp.concatenate(outs, axis=-1)


def _ctx_attention(p, sink, kv_heads, use_sink):
    n_seq, t_len, cols = p.shape
    kv_w = kv_heads * HEAD_DIM
    return pl.pallas_call(
        functools.partial(_ctx_attn_kernel, kv_heads=kv_heads, use_sink=use_sink),
        grid=(n_seq,),
        in_specs=[pl.BlockSpec(memory_space=pltpu.SMEM),
                  pl.BlockSpec((None, t_len, cols), lambda i: (i, 0, 0))],
        out_specs=[pl.BlockSpec((None, t_len, MIX_W), lambda i: (i, 0, 0)),
                   pl.BlockSpec((None, 2, t_len, kv_w), lambda i: (i, 0, 0, 0))],
        out_shape=[jax.ShapeDtypeStruct((n_seq, t_len, MIX_W), F32),
                   jax.ShapeDtypeStruct((n_seq, 2, t_len, kv_w), F32)],
        compiler_params=_params("parallel"),
        name="ctx_attention",
    )(sink, p)


def _nat_latent_kernel(p_ref, ckv_ref, bias_ref, o_ref, *, rows):
    kh = min(NAT_KH, rows)
    n_loc = kh * GRID_W
    ck_all = ckv_ref[0]
    cv_all = ckv_ref[1]
    for r in range(rows):
        rs = min(max(r - kh // 2, 0), rows - kh)
        roff0 = rs - r + NAT_KH - 1
        q_all = p_ref[r * GRID_W:(r + 1) * GRID_W, 0:MIX_W] * ATTN_SCALE
        k_all = p_ref[rs * GRID_W:rs * GRID_W + n_loc, MIX_W:2 * MIX_W]
        v_all = p_ref[rs * GRID_W:rs * GRID_W + n_loc, 2 * MIX_W:3 * MIX_W]
        outs = []
        for h in range(N_HEADS):
            q = _head(q_all, h)
            bias = jnp.concatenate([bias_ref[h, roff0 + 2 * j] for j in range(kh // 2)], axis=-1)
            s_loc = _bdot_nt(q, _head(k_all, h)) + bias
            outs.append(_softmax_av([s_loc, _bdot_nt(q, _head(ck_all, h))],
                                    [_head(v_all, h), _head(cv_all, h)]))
        o_ref[r * GRID_W:(r + 1) * GRID_W, :] = jnp.concatenate(outs, axis=-1)


def _nat_bias_table(rpb):
    q = np.arange(GRID_W)[:, None]
    kc = np.arange(GRID_W)[None, :]
    win = np.clip(q - NAT_KW // 2, 0, GRID_W - NAT_KW)
    ok = (kc >= win) & (kc < win + NAT_KW)
    coff = np.clip(kc - q, -(NAT_KW - 1), NAT_KW - 1) + NAT_KW - 1
    t = jnp.where(ok[None, None], rpb[:, :, coff], MASK_VALUE)
    return jnp.concatenate([t[:, :-1], t[:, 1:]], axis=-1)


def _nat_latent(p_nat, cache, bias, layer, n_seq, t_len, tile0):
    rows = t_len // GRID_W
    return pl.pallas_call(
        functools.partial(_nat_latent_kernel, rows=rows),
        grid=(n_seq,),
        in_specs=[pl.BlockSpec((t_len, NAT_COLS), lambda i: (tile0 + i, 0)),
                  pl.BlockSpec((None, None, 2, cache.shape[3], MIX_W), lambda i: (i, layer, 0, 0, 0)),
                  pl.BlockSpec(bias.shape, lambda i: (0, 0, 0, 0))],
        out_specs=pl.BlockSpec((t_len, MIX_W), lambda i: (i, 0)),
        out_shape=jax.ShapeDtypeStruct((n_seq * t_len, MIX_W), F32),
        compiler_params=_params("parallel"),
        name="nat_latent",
    )(p_nat, cache, bias)


def _rope(x, cos, sin_signed):
    width = x.shape[-1]
    lane = lax.broadcasted_iota(jnp.int32, x.shape, 1)
    half = HEAD_DIM // 4
    partner = jnp.where(lane % (2 * half) < half, pltpu.roll(x, width - half, 1), pltpu.roll(x, half, 1))
    return x * cos + partner * sin_signed


def _swa_latent_kernel(sink_ref, p_ref, ckv_ref, cos_ref, sin_ref, o_ref, *, t_len):
    hd = HEAD_DIM
    kv_w = SWA_KV_HEADS * hd
    group = N_HEADS // SWA_KV_HEADS
    blk = SWA_BLOCK
    nb = t_len // blk
    q = _rope(p_ref[:, 0:MIX_W], cos_ref[...], sin_ref[...]) * ATTN_SCALE
    k = _rope(p_ref[:, MIX_W:MIX_W + kv_w], cos_ref[:, 0:kv_w], sin_ref[:, 0:kv_w])
    v = p_ref[:, MIX_W + kv_w:MIX_W + 2 * kv_w]
    row = lax.broadcasted_iota(jnp.int32, (group * blk, 1), 0)
    ck_all = ckv_ref[0]
    cv_all = ckv_ref[1]
    for n in range(nb):
        lo = max(n - 1, 0) * blk
        hi = min(n + 2, nb) * blk
        qpos = n * blk + row % blk
        kpos = lo + lax.broadcasted_iota(jnp.int32, (group * blk, hi - lo), 1)
        ok = jnp.abs(kpos - qpos) <= SWA_WINDOW
        heads = [None] * N_HEADS
        for kvh in range(SWA_KV_HEADS):
            qs = jnp.concatenate([_head(q[n * blk:(n + 1) * blk], kvh * group + g) for g in range(group)],
                                 axis=0)
            sink = jnp.concatenate([jnp.full((blk, 1), sink_ref[kvh * group + g], F32)
                                    for g in range(group)], axis=0)
            s_loc = jnp.where(ok, _bdot_nt(qs, _head(k[lo:hi], kvh)), MASK_VALUE)
            s_ctx = _bdot_nt(qs, _head(ck_all, kvh))
            o = _softmax_av([s_loc, s_ctx], [_head(v[lo:hi], kvh), _head(cv_all, kvh)], sink=sink)
            for g in range(group):
                heads[kvh * group + g] = o[g * blk:(g + 1) * blk]
        o_ref[n * blk:(n + 1) * blk, :] = jnp.concatenate(heads, axis=-1)


def _rope_tables(t_len):
    t = jnp.arange(t_len)
    half = HEAD_DIM // 2
    nf = half // 2
    inv = 1.0 / (ROPE_BASE ** (jnp.arange(nf, dtype=F32) / nf))
    cols = []
    for pos in ((t // GRID_W).astype(F32), (t % GRID_W).astype(F32)):
        ang = pos[:, None] * inv[None, :]
        cols.append((jnp.cos(ang), jnp.sin(ang)))
    cos = jnp.concatenate([cols[0][0], cols[0][0], cols[1][0], cols[1][0]], axis=-1)
    sin = jnp.concatenate([-cols[0][1], cols[0][1], -cols[1][1], cols[1][1]], axis=-1)
    return jnp.tile(cos, (1, N_HEADS)), jnp.tile(sin, (1, N_HEADS))


def _swa_latent(p_swa, cache, sink, rope, layer, n_seq, t_len, tile0):
    kv_w = SWA_KV_HEADS * HEAD_DIM
    return pl.pallas_call(
        functools.partial(_swa_latent_kernel, t_len=t_len),
        grid=(n_seq,),
        in_specs=[pl.BlockSpec(memory_space=pltpu.SMEM),
                  pl.BlockSpec((t_len, SWA_COLS), lambda i: (tile0 + i, 0)),
                  pl.BlockSpec((None, None, 2, cache.shape[3], kv_w), lambda i: (i, layer, 0, 0, 0)),
                  pl.BlockSpec((t_len, MIX_W), lambda i: (0, 0)),
                  pl.BlockSpec((t_len, MIX_W), lambda i: (0, 0))],
        out_specs=pl.BlockSpec((t_len, MIX_W), lambda i: (i, 0)),
        out_shape=jax.ShapeDtypeStruct((n_seq * t_len, MIX_W), F32),
        compiler_params=_params("parallel"),
        name="swa_latent",
    )(sink, p_swa, cache, rope[0], rope[1])


def _recurrent_mixers(p_rw, p_hg, lw, seg, n_seq, t_len, tile0, states):
    ctx = states is None
    seq0 = tile0 * ROW_TILE // t_len
    n_tok = n_seq * t_len
    steps = 32

    scan_in, bonus = _rwkv_pre(p_rw, lw, seg, n_seq, t_len, seq0)
    rows, v = _to_scan_layout(scan_in, n_seq, t_len, ctx)
    s0 = jnp.zeros((2, HEAD_DIM, HEAD_DIM, LANES), F32) if ctx else _state_to_scan(states[0], True)
    y, s_rw = _scan(rows, v, s0, delta=True, steps=steps)
    ya = _rwkv_post(_from_scan_layout(y, n_seq, t_len, ctx), bonus, p_rw, lw, seg, tile0)

    scan_in = _hgrn_pre(p_hg, lw["hg_lb"], n_tok, tile0)
    rows, v = _to_scan_layout(scan_in, n_seq, t_len, ctx)
    s0 = jnp.zeros((2, HEAD_DIM, HEAD_DIM, LANES), F32) if ctx else _state_to_scan(states[1], False)
    o, s_hg = _scan(rows, v, s0, delta=False, steps=steps)
    yc = _hgrn_post(_from_scan_layout(o, n_seq, t_len, ctx), p_hg, lw["hg_norm"], seg, tile0)
    return ya, yc, s_rw, s_hg


def kernel(x_prompt, x_sample, cache_nat_kv, cache_swa_kv, state_rwkv, state_hgrn, c, c_ctx, norm_g, mod_w, mod_b, w_in, w_out, rw_mu_rkv, rw_mu_lora, rw_w0, rw_w2, rw_a0, rw_a2, rw_g2, rw_kk, rw_ka, rw_rk, rw_lnx_w, rw_lnx_b, nat_rpb, hg_lb_logits, hg_norm, swa_sink, ffn_w1, ffn_w2):
    n_ctx, t_ctx, _ = x_prompt.shape
    n_lat, t_lat, _ = x_sample.shape
    past = cache_nat_kv.shape[3]

    cond = jnp.zeros((MOD_ROWS, D_MODEL), F32).at[0].set(c_ctx).at[1:1 + n_lat].set(c)
    mod = _modulation_all(cond, mod_w, mod_b)

    w_in_bf, w_out_bf = w_in.astype(BF16), w_out.astype(BF16)
    w1_bf, w2_bf = ffn_w1.astype(BF16), ffn_w2.astype(BF16)
    cuts = np.cumsum((0,) + SLABS)
    lat_slabs = tuple(w_in_bf[:, :, cuts[j]:cuts[j + 1]] for j in range(4))
    ctx_modes = ("fm", "seq", "fm", "seq")
    ctx_y_modes = ("tok", "seq", "tok", "seq")
    lat_modes = ("tok",) * 4
    ctx_slabs = tuple(w.transpose(0, 2, 1) if m == "fm" else w for w, m in zip(lat_slabs, ctx_modes))

    lb_sm = jax.nn.softmax(hg_lb_logits.astype(F32), axis=1)
    hg_lb = jnp.cumsum(lb_sm, axis=1) - lb_sm[:, :1]

    head_of = np.arange(MIX_W) // HEAD_DIM
    seg = jnp.asarray((head_of[:, None] == head_of[None, :]).astype(np.float32))
    rope = _rope_tables(t_lat)
    zeros_lora = jnp.zeros((LORA_RANK, MIX_W), F32)
    cache_nat = cache_nat_kv.reshape(n_lat, DEPTH, 2, past, MIX_W)
    cache_swa = cache_swa_kv.reshape(n_lat, DEPTH, 2, past, SWA_KV_HEADS * HEAD_DIM)

    xp = x_prompt.transpose(1, 0, 2).reshape(t_ctx * n_ctx, D_MODEL)
    xs = x_sample.reshape(n_lat * t_lat, D_MODEL)
    ctx_row, lat_row = _mod_row(t_ctx, 0), _mod_row(t_lat, 1)
    nat_out, swa_out, rw_out, hg_out = [], [], [], []
    for l in range(DEPTH):
        lw = {
            "mu_rkv": rw_mu_rkv[l].reshape(2, 1, 3 * MIX_W),
            "mu_lo": rw_mu_lora[l].reshape(2, 1, 2 * LORA_RANK),
            "w_lora": jnp.stack([jnp.concatenate(
                [jnp.concatenate([rw_w2[l, d], zeros_lora], axis=1),
                 jnp.concatenate([zeros_lora, rw_a2[l, d]], axis=1)], axis=0) for d in range(2)]),
            "b_lora": jnp.concatenate([rw_w0[l], rw_a0[l]], axis=-1).reshape(2, 1, 2 * MIX_W),
            "rw_kk": rw_kk[l].reshape(1, MIX_W), "rw_ka": rw_ka[l].reshape(1, MIX_W),
            "rw_rk": rw_rk[l].reshape(1, MIX_W), "rw_g2": rw_g2[l],
            "rw_lnx_w": rw_lnx_w[l].reshape(1, MIX_W), "rw_lnx_b": rw_lnx_b[l].reshape(1, MIX_W),
            "hg_lb": hg_lb[:, l].reshape(2, 1, MIX_W), "hg_norm": hg_norm[l].reshape(1, MIX_W),
            "mu_rkv_col": rw_mu_rkv[l].reshape(2, 3 * MIX_W, 1), "mu_lo_col": rw_mu_lora[l].reshape(2, 2 * LORA_RANK, 1),
            "w2t": rw_w2[l].transpose(0, 2, 1), "a2t": rw_a2[l].transpose(0, 2, 1),
            "w0_col": rw_w0[l].reshape(2, MIX_W, 1), "a0_col": rw_a0[l].reshape(2, MIX_W, 1),
            "kk_col": rw_kk[l].reshape(MIX_W, 1), "ka_col": rw_ka[l].reshape(MIX_W, 1),
            "rk_col": rw_rk[l].reshape(MIX_W, 1), "g2t": rw_g2[l].T,
            "lnw_col": rw_lnx_w[l].reshape(MIX_W, 1), "lnb_col": rw_lnx_b[l].reshape(MIX_W, 1),
            "hg_lb_col": hg_lb[:, l].reshape(2, MIX_W, 1), "hg_norm_col": hg_norm[l].reshape(MIX_W, 1),
        }
        sink = swa_sink[l]

        pt_rw, p_nat, pt_hg, p_swa = _in_projection(xp, norm_g, mod, ctx_slabs, l, ctx_row, ctx_modes, n_ctx)
        ya, yc, s_rw, s_hg = _ctx_recurrent_mixers(pt_rw, pt_hg, lw, n_ctx, t_ctx)
        yb, cnat = _ctx_attention(p_nat, sink, N_HEADS, False)
        yd, cswa = _ctx_attention(p_swa, sink, SWA_KV_HEADS, True)
        xp = _out_projection_ffn(xp, (ya, yb, yc, yd), norm_g, mod, w_out_bf, w1_bf, w2_bf, l, ctx_row,
                                 ctx_y_modes, n_ctx)
        nat_out.append(cnat.reshape(n_ctx, 2, t_ctx, N_HEADS, HEAD_DIM))
        swa_out.append(cswa.reshape(n_ctx, 2, t_ctx, SWA_KV_HEADS, HEAD_DIM))
        rw_out.append(_state_from_scan(s_rw, n_ctx, True))
        hg_out.append(_state_from_scan(s_hg, n_ctx, False))

        p_rw, p_nat, p_hg, p_swa = _in_projection(xs, norm_g, mod, lat_slabs, l, lat_row, lat_modes)
        ya, yc, _, _ = _recurrent_mixers(p_rw, p_hg, lw, seg, n_lat, t_lat, 0,
                                         (state_rwkv[:, l], state_hgrn[:, l]))
        yb = _nat_latent(p_nat, cache_nat, _nat_bias_table(nat_rpb[l]), l, n_lat, t_lat, 0)
        yd = _swa_latent(p_swa, cache_swa, sink, rope, l, n_lat, t_lat, 0)
        xs = _out_projection_ffn(xs, (ya, yb, yc, yd), norm_g, mod, w_out_bf, w1_bf, w2_bf, l, lat_row,
                                 lat_modes)

    return (xp.reshape(t_ctx, n_ctx, D_MODEL).transpose(1, 0, 2), xs.reshape(x_sample.shape),
            jnp.stack(nat_out, axis=1), jnp.stack(swa_out, axis=1),
            jnp.stack(rw_out, axis=1), jnp.stack(hg_out, axis=1))
```

```python
import functools

import numpy as np
import jax
import jax.numpy as jnp
from jax import lax
from jax.experimental import pallas as pl
from jax.experimental.pallas import tpu as pltpu

F32 = jnp.float32
BF16 = jnp.bfloat16
HIGHEST = lax.Precision.HIGHEST

D_MODEL = 1024
DEPTH = 4
GRID_W = 64
HEAD_DIM = 64
N_HEADS = 4
MIX_W = N_HEADS * HEAD_DIM
SWA_KV_HEADS = 2
LORA_RANK = 64
GATE_RANK = 128
RWKV_DECAY_SCALE = 0.6065306597126334
RWKV_LN_EPS = 64e-5
NAT_KH = 8
NAT_KW = 16
SWA_WINDOW = 128
SWA_BLOCK = 128
ROPE_BASE = 10000.0
FFN_HIDDEN = 4 * D_MODEL
NORM_EPS = 1e-6
MASK_VALUE = -1e30
N_MOD = 6
ATTN_SCALE = HEAD_DIM ** -0.5

RW_COLS = 3 * MIX_W + GATE_RANK + 4 * LORA_RANK
NAT_COLS = 3 * MIX_W
HG_COLS = 5 * MIX_W
SWA_COLS = MIX_W + 2 * SWA_KV_HEADS * HEAD_DIM
SLABS = (RW_COLS, NAT_COLS, HG_COLS, SWA_COLS)
D_IN = sum(SLABS)

LANES = 128
SUBLANES = 8
ROW_TILE = 256
MOD_ROWS = 8
SCAN_KEY_CHUNK = 16


def _sigmoid(x):
    return 1.0 / (1.0 + jnp.exp(-x))


def _rms_rows(x, g):
    ms = jnp.mean(x * x, axis=-1, keepdims=True)
    return x * lax.rsqrt(ms + NORM_EPS) * g


def _bdot(a, b):
    return jnp.dot(a.astype(BF16), b.astype(BF16), preferred_element_type=F32)


def _bdot_nt(a, b):
    return lax.dot_general(a.astype(BF16), b.astype(BF16), (((1,), (1,)), ((), ())),
                           preferred_element_type=F32)


def _seg_sum(x, seg):
    return jnp.dot(x, seg, preferred_element_type=F32, precision=HIGHEST)


def _exact_select_dot(x, sel):
    hi = x.astype(BF16)
    r1 = x - hi.astype(F32)
    mid = r1.astype(BF16)
    lo = (r1 - mid.astype(F32)).astype(BF16)
    dot = lambda a: jnp.dot(a, sel, preferred_element_type=F32)
    return dot(hi) + dot(mid) + dot(lo)


def _params(*sem):
    return pltpu.CompilerParams(dimension_semantics=sem)


def _resident(shape, index_map):
    return pl.BlockSpec(shape, index_map, pipeline_mode=pl.Buffered(1))


def _mod_kernel(cond_ref, w_ref, b_ref, o_ref):
    cnd = cond_ref[...]
    s = cnd * _sigmoid(cnd)
    o_ref[...] = _bdot(s, w_ref[...]) + b_ref[...]


def _modulation_all(cond, mod_w, mod_b):
    tn = 1536
    n = N_MOD * D_MODEL
    return pl.pallas_call(
        _mod_kernel,
        grid=(DEPTH, n // tn),
        in_specs=[pl.BlockSpec((MOD_ROWS, D_MODEL), lambda l, j: (0, 0)),
                  pl.BlockSpec((None, D_MODEL, tn), lambda l, j: (l, 0, j)),
                  pl.BlockSpec((None, 1, tn), lambda l, j: (l, 0, j))],
        out_specs=pl.BlockSpec((None, MOD_ROWS, tn), lambda l, j: (l, 0, j)),
        out_shape=jax.ShapeDtypeStruct((DEPTH, MOD_ROWS, n), F32),
        compiler_params=_params("parallel", "parallel"),
        name="modulation",
    )(cond, mod_w, mod_b.reshape(DEPTH, 1, n))


def _mod_row(seq_len, first_row):
    if first_row == 0:
        return lambda i: 0
    return lambda i: first_row + (i * ROW_TILE) // seq_len


SEQ_TILE_STEPS = SUBLANES


def _tile_perm(n_seq, to_seq_major):
    r = np.arange(ROW_TILE)
    src = (r % SEQ_TILE_STEPS) * n_seq + r // SEQ_TILE_STEPS
    p = (src[:, None] == r[None, :])
    return jnp.asarray(p if to_seq_major else p.T, dtype=BF16)


def _permute_rows(perm_ref, xb):
    return jnp.dot(perm_ref[...], xb, preferred_element_type=F32).astype(BF16)


def _inproj_kernel(*refs, row_fn, modes):
    x_ref, g_ref, mod_ref = refs[0:3]
    w_refs = refs[3:7]
    perm_ref = refs[7] if "seq" in modes else None
    o_refs = refs[-4:]
    row = row_fn(pl.program_id(0))
    shift = mod_ref[pl.ds(row, 1), 0:D_MODEL]
    scale = mod_ref[pl.ds(row, 1), D_MODEL:2 * D_MODEL]
    h = _rms_rows(x_ref[...], g_ref[0:1, :]) * (1.0 + scale) + shift
    hb = h.astype(BF16)
    hb_seq = _permute_rows(perm_ref, hb) if perm_ref is not None else None
    for w_ref, o_ref, mode in zip(w_refs, o_refs, modes):
        if mode == "fm":
            o_ref[...] = lax.dot_general(w_ref[...], hb, (((1,), (1,)), ((), ())),
                                         preferred_element_type=F32)
        elif mode == "seq":
            o_ref[...] = jnp.dot(hb_seq, w_ref[...], preferred_element_type=F32).reshape(o_ref.shape)
        else:
            o_ref[...] = jnp.dot(hb, w_ref[...], preferred_element_type=F32)


def _in_projection(x, norm_g, mod, w_slabs, layer, row_fn, modes, n_seq=None):
    n = x.shape[0]
    out_specs, out_shape = [], []
    for width, mode in zip(SLABS, modes):
        if mode == "fm":
            out_specs.append(pl.BlockSpec((width, ROW_TILE), lambda i: (0, i)))
            out_shape.append(jax.ShapeDtypeStruct((width, n), F32))
        elif mode == "seq":
            out_specs.append(pl.BlockSpec((n_seq, SEQ_TILE_STEPS, width), lambda i: (0, i, 0)))
            out_shape.append(jax.ShapeDtypeStruct((n_seq, n // n_seq, width), F32))
        else:
            out_specs.append(pl.BlockSpec((ROW_TILE, width), lambda i: (i, 0)))
            out_shape.append(jax.ShapeDtypeStruct((n, width), F32))
    in_specs = [pl.BlockSpec((ROW_TILE, D_MODEL), lambda i: (i, 0)),
                pl.BlockSpec((None, 4, D_MODEL), lambda i: (layer, 0, 0)),
                pl.BlockSpec((None, MOD_ROWS, N_MOD * D_MODEL), lambda i: (layer, 0, 0))]
    in_specs += [_resident((None,) + w.shape[1:], lambda i: (layer, 0, 0)) for w in w_slabs]
    args = [x, norm_g, mod, *w_slabs]
    if "seq" in modes:
        assert n_seq * SEQ_TILE_STEPS == ROW_TILE
        in_specs.append(pl.BlockSpec((ROW_TILE, ROW_TILE), lambda i: (0, 0)))
        args.append(_tile_perm(n_seq, True))
    return pl.pallas_call(
        functools.partial(_inproj_kernel, row_fn=row_fn, modes=modes),
        grid=(n // ROW_TILE,),
        in_specs=in_specs,
        out_specs=out_specs,
        out_shape=out_shape,
        compiler_params=_params("parallel"),
        name="in_projection",
    )(*args)


def _outproj_ffn_kernel(*refs, row_fn, modes):
    x_ref = refs[0]
    y_refs = refs[1:5]
    g_ref, mod_ref, wo_ref, w1_ref, w2_ref = refs[5:10]
    perm_ref = refs[10] if "seq" in modes else None
    o_ref = refs[-1]
    row = row_fn(pl.program_id(0))
    mod = lambda j: mod_ref[pl.ds(row, 1), j * D_MODEL:(j + 1) * D_MODEL]
    x = x_ref[...]
    acc = None
    for j, (y_ref, mode) in enumerate(zip(y_refs, modes)):
        yb = y_ref[...].reshape(ROW_TILE, MIX_W).astype(BF16)
        if mode == "seq":
            yb = _permute_rows(perm_ref, yb)
        part = jnp.dot(yb, wo_ref[j * MIX_W:(j + 1) * MIX_W, :], preferred_element_type=F32)
        acc = part if acc is None else acc + part
    x = x + mod(2) * _rms_rows(acc, g_ref[1:2, :])
    h = (_rms_rows(x, g_ref[2:3, :]) * (1.0 + mod(4)) + mod(3)).astype(BF16)
    f = None
    for j in range(FFN_HIDDEN // D_MODEL):
        u = jnp.dot(h, w1_ref[:, j * D_MODEL:(j + 1) * D_MODEL], preferred_element_type=F32)
        u = jnp.square(jnp.maximum(u, 0.0)).astype(BF16)
        part = jnp.dot(u, w2_ref[j * D_MODEL:(j + 1) * D_MODEL, :], preferred_element_type=F32)
        f = part if f is None else f + part
    o_ref[...] = x + mod(5) * _rms_rows(f, g_ref[3:4, :])


def _out_projection_ffn(x, ys, norm_g, mod, w_out_bf, w1_bf, w2_bf, layer, row_fn, modes, n_seq=None):
    n = x.shape[0]
    tile = lambda w: pl.BlockSpec((ROW_TILE, w), lambda i: (i, 0))
    y_specs = [pl.BlockSpec((n_seq, SEQ_TILE_STEPS, MIX_W), lambda i: (0, i, 0)) if m == "seq" else tile(MIX_W)
               for m in modes]
    in_specs = [tile(D_MODEL)] + y_specs + [
        pl.BlockSpec((None, 4, D_MODEL), lambda i: (layer, 0, 0)),
        pl.BlockSpec((None, MOD_ROWS, N_MOD * D_MODEL), lambda i: (layer, 0, 0)),
        _resident((None, D_MODEL, D_MODEL), lambda i: (layer, 0, 0)),
        _resident((None, D_MODEL, FFN_HIDDEN), lambda i: (layer, 0, 0)),
        _resident((None, FFN_HIDDEN, D_MODEL), lambda i: (layer, 0, 0))]
    args = [x, *ys, norm_g, mod, w_out_bf, w1_bf, w2_bf]
    if "seq" in modes:
        in_specs.append(pl.BlockSpec((ROW_TILE, ROW_TILE), lambda i: (0, 0)))
        args.append(_tile_perm(n_seq, False))
    return pl.pallas_call(
        functools.partial(_outproj_ffn_kernel, row_fn=row_fn, modes=modes),
        grid=(n // ROW_TILE,),
        in_specs=in_specs,
        out_specs=tile(D_MODEL),
        out_shape=jax.ShapeDtypeStruct((n, D_MODEL), F32),
        compiler_params=_params("parallel"),
        name="out_projection_ffn",
    )(*args)


def _scan_kernel(*refs, n_rows, delta, steps, v_tiles, expand, reverse_odd):
    if expand:
        rows_ref, v_ref, s0_ref, e_ref, y_ref, sfin_ref, s_scr, x_scr = refs
    else:
        rows_ref, v_ref, s0_ref, y_ref, sfin_ref, s_scr = refs
        x_scr = rows_ref
    tb = pl.program_id(1)

    @pl.when(tb == 0)
    def _():
        s_scr[...] = s0_ref[...]

    if expand:
        for j in range(n_rows):
            flat = rows_ref[j].reshape(steps * HEAD_DIM, rows_ref.shape[-1])
            x_scr[j] = _exact_select_dot(flat, e_ref[...]).reshape(steps, HEAD_DIM, LANES)

    n_part = max(1, 4 // v_tiles)
    tiles = [slice(i * SUBLANES, (i + 1) * SUBLANES) for i in range(v_tiles)]
    if delta:
        i_nkk, i_w, i_kka, i_kv, i_r = range(5)
    else:
        i_w, i_kv, i_r = range(3)

    k_chunk = SCAN_KEY_CHUNK if v_tiles > 1 else HEAD_DIM
    zero = jnp.zeros((SUBLANES, LANES), F32)
    acc0 = tuple(zero for _ in range(v_tiles * n_part))

    def over_keys(body):
        def chunk(c, acc):
            acc = list(acc)
            for kk in range(k_chunk):
                acc = body(c * k_chunk + kk, kk, acc)
            return tuple(acc)
        if k_chunk == HEAD_DIM:
            acc = chunk(0, acc0)
        else:
            acc = lax.fori_loop(0, HEAD_DIM // k_chunk, chunk, acc0)
        sums = []
        for i in range(v_tiles):
            parts = list(acc[i * n_part:(i + 1) * n_part])
            while len(parts) > 1:
                parts = [a + b for a, b in zip(parts[0::2], parts[1::2])]
            sums.append(parts[0])
        return sums

    backward = (pl.program_id(0) % 2 == 1) if reverse_odd else None

    def step(i_step, carry):
        t = jnp.where(backward, steps - 1 - i_step, i_step) if reverse_odd else i_step
        row = lambda j, k: x_scr[j, t, pl.ds(k, 1), :]
        vt = [v_ref[t, tl, :] for tl in tiles]
        if delta:
            def sa_body(k, kk, acc):
                nk = row(i_nkk, k)
                for i, tl in enumerate(tiles):
                    p = i * n_part + kk % n_part
                    acc[p] = acc[p] + s_scr[k, tl, :] * nk
                return acc
            sa = over_keys(sa_body)

        def update_body(k, kk, acc):
            wk, kvk, rk = row(i_w, k), row(i_kv, k), row(i_r, k)
            if delta:
                kkak = row(i_kka, k)
            for i, tl in enumerate(tiles):
                s = s_scr[k, tl, :] * wk + vt[i] * kvk
                if delta:
                    s = s + sa[i] * kkak
                s_scr[k, tl, :] = s
                p = i * n_part + kk % n_part
                acc[p] = acc[p] + s * rk
            return acc
        y = over_keys(update_body)
        for i, tl in enumerate(tiles):
            y_ref[t, tl, :] = y[i]
        return carry

    lax.fori_loop(0, steps, step, 0)

    @pl.when(tb == pl.num_programs(1) - 1)
    def _():
        sfin_ref[...] = s_scr[...]


def _scan(rows, v, s0, *, delta, steps, reverse_odd=False):
    groups, _, t_len, _, n_chain = rows.shape
    n_rows = 5 if delta else 3
    n_blocks = t_len // steps
    tblk = (lambda g, t: jnp.where(g % 2 == 1, n_blocks - 1 - t, t)) if reverse_odd else (lambda g, t: t)
    if v is None:
        vrows = HEAD_DIM
        v_spec = pl.BlockSpec((None, None, steps, vrows, LANES), lambda g, t: (g, n_rows, tblk(g, t), 0, 0))
        v = rows
    else:
        vrows = v.shape[2]
        v_spec = pl.BlockSpec((None, steps, vrows, LANES), lambda g, t: (g, tblk(g, t), 0, 0))
    v_tiles = vrows // SUBLANES
    expand = n_chain != LANES
    in_specs = [pl.BlockSpec((None, n_rows, steps, HEAD_DIM, n_chain), lambda g, t: (g, 0, tblk(g, t), 0, 0)),
                v_spec,
                pl.BlockSpec((None, HEAD_DIM, vrows, LANES), lambda g, t: (g, 0, 0, 0))]
    args = [rows, v, s0]
    scratch = [pltpu.VMEM((HEAD_DIM, vrows, LANES), F32)]
    if expand:
        lane = np.arange(LANES)
        e = (lane[None, :] % n_chain == np.arange(n_chain)[:, None]).astype(np.float32)
        in_specs.append(pl.BlockSpec((n_chain, LANES), lambda g, t: (0, 0)))
        args.append(jnp.asarray(e, dtype=BF16))
        scratch.append(pltpu.VMEM((n_rows, steps, HEAD_DIM, LANES), F32))
    return pl.pallas_call(
        functools.partial(_scan_kernel, n_rows=n_rows, delta=delta, steps=steps, v_tiles=v_tiles,
                          expand=expand, reverse_odd=reverse_odd),
        grid=(groups, n_blocks),
        in_specs=in_specs,
        out_specs=[pl.BlockSpec((None, steps, vrows, LANES), lambda g, t: (g, tblk(g, t), 0, 0)),
                   pl.BlockSpec((None, HEAD_DIM, vrows, LANES), lambda g, t: (g, 0, 0, 0))],
        out_shape=[jax.ShapeDtypeStruct((groups, t_len, vrows, LANES), F32),
                   jax.ShapeDtypeStruct((groups, HEAD_DIM, vrows, LANES), F32)],
        scratch_shapes=scratch,
        compiler_params=_params("parallel", "arbitrary"),
        name="rwkv_scan" if delta else "hgrn_scan",
    )(*args)


def _to_scan_layout(a, n_seq, t_len, per_dir_groups):
    n_arr = a.shape[1]
    a = a.reshape(2, n_arr, n_seq, t_len, N_HEADS, HEAD_DIM)
    a = jnp.stack([a[0], a[1][:, :, ::-1]])
    if per_dir_groups:
        r = a.transpose(0, 1, 3, 5, 2, 4).reshape(2, n_arr, t_len, HEAD_DIM, n_seq * N_HEADS)
        return r[:, :n_arr - 1], r[:, n_arr - 1]
    n_chain = 2 * n_seq * N_HEADS
    vg = LANES // n_chain
    rows = a[:, :n_arr - 1].transpose(1, 3, 5, 0, 2, 4).reshape(1, n_arr - 1, t_len, HEAD_DIM, n_chain)
    v = a[:, n_arr - 1].reshape(2, n_seq, t_len, N_HEADS, vg, HEAD_DIM // vg)
    v = v.transpose(2, 5, 4, 0, 1, 3).reshape(1, t_len, HEAD_DIM // vg, LANES)
    return rows, v


def _from_scan_layout(y, n_seq, t_len, per_dir_groups):
    if per_dir_groups:
        y = y.reshape(2, t_len, HEAD_DIM, n_seq, N_HEADS)
        y = y[0] + y[1][::-1]
        return y.transpose(2, 0, 3, 1).reshape(n_seq * t_len, MIX_W)
    n_chain = 2 * n_seq * N_HEADS
    vg = LANES // n_chain
    y = y.reshape(t_len, HEAD_DIM // vg, vg, 2, n_seq, N_HEADS)
    y = y[:, :, :, 0] + y[::-1, :, :, 1]
    return y.transpose(3, 0, 4, 2, 1).reshape(n_seq * t_len, MIX_W)


def _state_to_scan(s, key_last):
    n_seq = s.shape[0]
    vg = LANES // (2 * n_seq * N_HEADS)
    if key_last:
        s = s.reshape(n_seq, 2, N_HEADS, vg, HEAD_DIM // vg, HEAD_DIM).transpose(5, 4, 3, 1, 0, 2)
    else:
        s = s.reshape(n_seq, 2, N_HEADS, HEAD_DIM, vg, HEAD_DIM // vg).transpose(3, 5, 4, 1, 0, 2)
    return s.reshape(1, HEAD_DIM, HEAD_DIM // vg, LANES)


def _state_from_scan(s, n_seq, key_last):
    s = s.reshape(2, HEAD_DIM, HEAD_DIM, N_HEADS, n_seq)
    return s.transpose(4, 0, 3, 2, 1) if key_last else s.transpose(4, 0, 3, 1, 2)


def _shifted(x, d, t_len):
    row = lax.broadcasted_iota(jnp.int32, (t_len, 1), 0)
    prev_f = jnp.where(row == 0, 0.0, pltpu.roll(x, 1, 0))
    prev_b = jnp.where(row == t_len - 1, 0.0, pltpu.roll(x, t_len - 1, 0))
    return jnp.where(d == 0, prev_f, prev_b)


def _rwkv_pre_kernel(rkv_ref, lora_ref, mu_rkv_ref, mu_lo_ref, wl_ref, b0_ref, kk_ref, ka_ref, rk_ref,
                     seg_ref, scan_ref, bonus_ref, *, t_len):
    d = pl.program_id(1)
    x = rkv_ref[...]
    x = x + (_shifted(x, d, t_len) - x) * mu_rkv_ref[...]
    r, k, v = x[:, 0:MIX_W], x[:, MIX_W:2 * MIX_W], x[:, 2 * MIX_W:3 * MIX_W]
    lo = lora_ref[...]
    lo = lo + (_shifted(lo, d, t_len) - lo) * mu_lo_ref[...]
    lane = lax.broadcasted_iota(jnp.int32, lo.shape, 1)
    lo = jnp.where(lane < LORA_RANK, jnp.tanh(lo), lo)
    pre = _bdot(lo, wl_ref[...]) + b0_ref[...]
    w = jnp.exp(-RWKV_DECAY_SCALE * _sigmoid(pre[:, 0:MIX_W]))
    a = _sigmoid(pre[:, MIX_W:2 * MIX_W])
    seg = seg_ref[...]
    kk = k * kk_ref[...]
    kk = kk / jnp.maximum(jnp.sqrt(_seg_sum(kk * kk, seg)), 1e-12)
    k = k * (1.0 + (a - 1.0) * ka_ref[...])
    scan_ref[0] = -kk
    scan_ref[1] = w
    scan_ref[2] = kk * a
    scan_ref[3] = k
    scan_ref[4] = r
    scan_ref[5] = v
    bonus_ref[...] = _seg_sum(r * k * rk_ref[...], seg) * v


def _rwkv_pre(p_rw, lw, seg, n_seq, t_len, seq0):
    n = n_seq * t_len
    lora_blk = (3 * MIX_W + GATE_RANK) // (2 * LORA_RANK)
    vec = lambda w: pl.BlockSpec((None, 1, w), lambda i, d: (d, 0, 0))
    shared = lambda w: pl.BlockSpec((1, w), lambda i, d: (0, 0))
    return pl.pallas_call(
        functools.partial(_rwkv_pre_kernel, t_len=t_len),
        grid=(n_seq, 2),
        in_specs=[pl.BlockSpec((t_len, 3 * MIX_W), lambda i, d: (seq0 + i, 0)),
                  pl.BlockSpec((t_len, 2 * LORA_RANK), lambda i, d: (seq0 + i, lora_blk + d)),
                  vec(3 * MIX_W), vec(2 * LORA_RANK),
                  pl.BlockSpec((None, 2 * LORA_RANK, 2 * MIX_W), lambda i, d: (d, 0, 0)),
                  vec(2 * MIX_W), shared(MIX_W), shared(MIX_W), shared(MIX_W),
                  pl.BlockSpec((MIX_W, MIX_W), lambda i, d: (0, 0))],
        out_specs=[pl.BlockSpec((None, 6, t_len, MIX_W), lambda i, d: (d, 0, i, 0)),
                   pl.BlockSpec((None, t_len, MIX_W), lambda i, d: (d, i, 0))],
        out_shape=[jax.ShapeDtypeStruct((2, 6, n, MIX_W), F32),
                   jax.ShapeDtypeStruct((2, n, MIX_W), F32)],
        compiler_params=_params("parallel", "parallel"),
        name="rwkv_pre",
    )(p_rw, p_rw, lw["mu_rkv"], lw["mu_lo"], lw["w_lora"], lw["b_lora"], lw["rw_kk"], lw["rw_ka"],
      lw["rw_rk"], seg)


def _rwkv_post_kernel(y_ref, bonus_ref, gh_ref, g2_ref, lnw_ref, lnb_ref, seg_ref, o_ref):
    seg = seg_ref[...]
    y = y_ref[...]
    mu = _seg_sum(y, seg) * (1.0 / HEAD_DIM)
    yc = y - mu
    var = _seg_sum(yc * yc, seg) * (1.0 / HEAD_DIM)
    y = yc * lax.rsqrt(var + RWKV_LN_EPS) * lnw_ref[...] + lnb_ref[...]
    y = y + bonus_ref[0] + bonus_ref[1]
    o_ref[...] = y * _bdot(_sigmoid(gh_ref[...]), g2_ref[...])


def _rwkv_post(y, bonus, p_rw, lw, seg, tile0):
    n = y.shape[0]
    gh_blk = 3 * MIX_W // GATE_RANK
    shared = lambda w: pl.BlockSpec((1, w), lambda i: (0, 0))
    return pl.pallas_call(
        _rwkv_post_kernel,
        grid=(n // ROW_TILE,),
        in_specs=[pl.BlockSpec((ROW_TILE, MIX_W), lambda i: (i, 0)),
                  pl.BlockSpec((2, ROW_TILE, MIX_W), lambda i: (0, i, 0)),
                  pl.BlockSpec((ROW_TILE, GATE_RANK), lambda i: (tile0 + i, gh_blk)),
                  pl.BlockSpec((GATE_RANK, MIX_W), lambda i: (0, 0)),
                  shared(MIX_W), shared(MIX_W),
                  pl.BlockSpec((MIX_W, MIX_W), lambda i: (0, 0))],
        out_specs=pl.BlockSpec((ROW_TILE, MIX_W), lambda i: (i, 0)),
        out_shape=jax.ShapeDtypeStruct((n, MIX_W), F32),
        compiler_params=_params("parallel"),
        name="rwkv_post",
    )(y, bonus, p_rw, lw["rw_g2"], lw["rw_lnx_w"], lw["rw_lnx_b"], seg)


def _hgrn_pre_kernel(qv_ref, f_ref, lb_ref, scan_ref):
    q = qv_ref[:, 0:MIX_W]
    f_raw = f_ref[...]
    lb = lb_ref[...]
    scan_ref[0] = lb + (1.0 - lb) * _sigmoid(f_raw)
    scan_ref[1] = (1.0 - lb) * _sigmoid(-f_raw)
    scan_ref[2] = q * _sigmoid(q)
    scan_ref[3] = qv_ref[:, MIX_W:2 * MIX_W]


def _hgrn_pre(p_hg, lb, n_tok, tile0):
    return pl.pallas_call(
        _hgrn_pre_kernel,
        grid=(n_tok // ROW_TILE, 2),
        in_specs=[pl.BlockSpec((ROW_TILE, 2 * MIX_W), lambda i, d: (tile0 + i, 0)),
                  pl.BlockSpec((ROW_TILE, MIX_W), lambda i, d: (tile0 + i, 3 + d)),
                  pl.BlockSpec((None, 1, MIX_W), lambda i, d: (d, 0, 0))],
        out_specs=pl.BlockSpec((None, 4, ROW_TILE, MIX_W), lambda i, d: (d, 0, i, 0)),
        out_shape=jax.ShapeDtypeStruct((2, 4, n_tok, MIX_W), F32),
        compiler_params=_params("parallel", "parallel"),
        name="hgrn_pre",
    )(p_hg, p_hg, lb)


def _hgrn_post_kernel(o_ref, g_ref, nw_ref, seg_ref, out_ref):
    o = o_ref[...]
    ms = _seg_sum(o * o, seg_ref[...]) * (1.0 / HEAD_DIM)
    out_ref[...] = o * lax.rsqrt(ms + NORM_EPS) * nw_ref[...] * _sigmoid(g_ref[...])


def _hgrn_post(o, p_hg, hg_norm, seg, tile0):
    n = o.shape[0]
    return pl.pallas_call(
        _hgrn_post_kernel,
        grid=(n // ROW_TILE,),
        in_specs=[pl.BlockSpec((ROW_TILE, MIX_W), lambda i: (i, 0)),
                  pl.BlockSpec((ROW_TILE, MIX_W), lambda i: (tile0 + i, 2)),
                  pl.BlockSpec((1, MIX_W), lambda i: (0, 0)),
                  pl.BlockSpec((MIX_W, MIX_W), lambda i: (0, 0))],
        out_specs=pl.BlockSpec((ROW_TILE, MIX_W), lambda i: (i, 0)),
        out_shape=jax.ShapeDtypeStruct((n, MIX_W), F32),
        compiler_params=_params("parallel"),
        name="hgrn_post",
    )(o, p_hg, hg_norm, seg)


CTX_TILE_STEPS = ROW_TILE // 32


def _fm_to_scan(a, n_seq):
    grp = lax.broadcasted_iota(jnp.int32, (HEAD_DIM, LANES), 1) // n_seq
    rolled = []
    for h in range(N_HEADS):
        blk = a[h * HEAD_DIM:(h + 1) * HEAD_DIM]
        rolled.append([blk if s == 0 else pltpu.roll(blk, s * n_seq, 1) for s in range(N_HEADS)])
    tiles = []
    for t in range(N_HEADS):
        out = rolled[0][(0 - t) % N_HEADS]
        for h in range(1, N_HEADS):
            out = jnp.where(grp == h, rolled[h][(h - t) % N_HEADS], out)
        tiles.append(out)
    return tiles


def _scan_to_fm(tiles, n_seq):
    grp = lax.broadcasted_iota(jnp.int32, (HEAD_DIM, LANES), 1) // n_seq
    rolled = [[tl if s == 0 else pltpu.roll(tl, s * n_seq, 1) for s in range(N_HEADS)] for tl in tiles]
    blocks = []
    for h in range(N_HEADS):
        out = rolled[0][(0 - h) % N_HEADS]
        for t in range(1, N_HEADS):
            out = jnp.where(grp == t, rolled[t][(t - h) % N_HEADS], out)
        blocks.append(out)
    return jnp.concatenate(blocks, axis=0)


def _store_scan_tiles(scan_ref, d, j, a, n_seq):
    for c in range(a.shape[1] // LANES):
        for t, tile in enumerate(_fm_to_scan(a[:, c * LANES:(c + 1) * LANES], n_seq)):
            scan_ref[d, j, c * N_HEADS + t] = tile


def _load_scan_tiles(y_ref, n_seq):
    cols = []
    for c in range(y_ref.shape[1] // N_HEADS):
        tiles = [y_ref[0, c * N_HEADS + t] + y_ref[1, c * N_HEADS + t] for t in range(N_HEADS)]
        cols.append(_scan_to_fm(tiles, n_seq))
    return jnp.concatenate(cols, axis=1)


def _head_sum_rows(x):
    w = x.shape[1]
    s = jnp.sum(x.reshape(N_HEADS, HEAD_DIM, w), axis=1, keepdims=True)
    return jnp.broadcast_to(s, (N_HEADS, HEAD_DIM, w)).reshape(N_HEADS * HEAD_DIM, w)


def _lane_halo(halo, shift, valid, width):
    h = jnp.where(valid, pltpu.roll(halo, shift, 1), 0.0)
    return jnp.concatenate([h] * (width // LANES), axis=1)


def _prev_step(x, halo, valid, n_seq):
    width = x.shape[1]
    lane = lax.broadcasted_iota(jnp.int32, (1, width), 1)
    return jnp.where(lane < n_seq, _lane_halo(halo, n_seq, valid, width), pltpu.roll(x, n_seq, 1))


def _next_step(x, halo, valid, n_seq):
    width = x.shape[1]
    lane = lax.broadcasted_iota(jnp.int32, (1, width), 1)
    return jnp.where(lane >= width - n_seq, _lane_halo(halo, LANES - n_seq, valid, width),
                     pltpu.roll(x, width - n_seq, 1))


def _rwkv_pre_fm_kernel(x_ref, hp_ref, hn_ref, mu_rkv_ref, mu_lo_ref, w2t_ref, a2t_ref, w0_ref, a0_ref,
                        kk_ref, ka_ref, rk_ref, scan_ref, bonus_ref, *, n_seq):
    i = pl.program_id(0)
    last = pl.num_programs(0) - 1
    rkv_rows = slice(0, 3 * MIX_W)
    x = x_ref[rkv_rows, :]
    bonus = None
    for d in range(2):
        if d == 0:
            shifted = lambda a, rows: _prev_step(a, hp_ref[rows, :], i > 0, n_seq)
        else:
            shifted = lambda a, rows: _next_step(a, hn_ref[rows, :], i < last, n_seq)
        x3 = x + (shifted(x, rkv_rows) - x) * mu_rkv_ref[d]
        r, k, v = x3[0:MIX_W], x3[MIX_W:2 * MIX_W], x3[2 * MIX_W:3 * MIX_W]
        lo_rows = slice(3 * MIX_W + GATE_RANK + 2 * LORA_RANK * d, 3 * MIX_W + GATE_RANK + 2 * LORA_RANK * (d + 1))
        lo = x_ref[lo_rows, :]
        lo = lo + (shifted(lo, lo_rows) - lo) * mu_lo_ref[d]
        w_pre = _bdot(w2t_ref[d], jnp.tanh(lo[0:LORA_RANK])) + w0_ref[d]
        a_pre = _bdot(a2t_ref[d], lo[LORA_RANK:2 * LORA_RANK]) + a0_ref[d]
        w = jnp.exp(-RWKV_DECAY_SCALE * _sigmoid(w_pre))
        a = _sigmoid(a_pre)
        kk = k * kk_ref[...]
        kk = kk / jnp.maximum(jnp.sqrt(_head_sum_rows(kk * kk)), 1e-12)
        k = k * (1.0 + (a - 1.0) * ka_ref[...])
        for j, arr in enumerate((-kk, w, kk * a, k, r, v)):
            _store_scan_tiles(scan_ref, d, j, arr, n_seq)
        b = _head_sum_rows(r * k * rk_ref[...]) * v
        bonus = b if bonus is None else bonus + b
    bonus_ref[...] = bonus


def _rwkv_pre_fm(pt_rw, lw, n_seq, t_len):
    n = pt_rw.shape[1]
    n_tiles = n // ROW_TILE
    per_tile = ROW_TILE // LANES
    col = lambda rows: pl.BlockSpec((rows, 1), lambda i: (0, 0))
    dcol = lambda rows: pl.BlockSpec((2, rows, 1), lambda i: (0, 0, 0))
    return pl.pallas_call(
        functools.partial(_rwkv_pre_fm_kernel, n_seq=n_seq),
        grid=(n_tiles,),
        in_specs=[pl.BlockSpec((RW_COLS, ROW_TILE), lambda i: (0, i)),
                  pl.BlockSpec((RW_COLS, LANES), lambda i: (0, jnp.maximum(i * per_tile - 1, 0))),
                  pl.BlockSpec((RW_COLS, LANES),
                               lambda i: (0, jnp.minimum((i + 1) * per_tile, n_tiles * per_tile - 1))),
                  dcol(3 * MIX_W), dcol(2 * LORA_RANK),
                  pl.BlockSpec((2, MIX_W, LORA_RANK), lambda i: (0, 0, 0)),
                  pl.BlockSpec((2, MIX_W, LORA_RANK), lambda i: (0, 0, 0)),
                  dcol(MIX_W), dcol(MIX_W), col(MIX_W), col(MIX_W), col(MIX_W)],
        out_specs=[pl.BlockSpec((2, 6, CTX_TILE_STEPS, HEAD_DIM, LANES), lambda i: (0, 0, i, 0, 0)),
                   pl.BlockSpec((MIX_W, ROW_TILE), lambda i: (0, i))],
        out_shape=[jax.ShapeDtypeStruct((2, 6, t_len, HEAD_DIM, LANES), F32),
                   jax.ShapeDtypeStruct((MIX_W, n), F32)],
        compiler_params=_params("parallel"),
        name="rwkv_pre_fm",
    )(pt_rw, pt_rw, pt_rw, lw["mu_rkv_col"], lw["mu_lo_col"], lw["w2t"], lw["a2t"], lw["w0_col"],
      lw["a0_col"], lw["kk_col"], lw["ka_col"], lw["rk_col"])


def _rwkv_post_fm_kernel(y_ref, bonus_ref, gh_ref, g2t_ref, lnw_ref, lnb_ref, o_ref, *, n_seq):
    y = _load_scan_tiles(y_ref, n_seq)
    mu = _head_sum_rows(y) * (1.0 / HEAD_DIM)
    yc = y - mu
    var = _head_sum_rows(yc * yc) * (1.0 / HEAD_DIM)
    y = yc * lax.rsqrt(var + RWKV_LN_EPS) * lnw_ref[...] + lnb_ref[...] + bonus_ref[...]
    g = _bdot(g2t_ref[...], _sigmoid(gh_ref[...]))
    o_ref[...] = (y * g).T


def _rwkv_post_fm(y, bonus, pt_rw, lw, n_seq):
    n = bonus.shape[1]
    col = pl.BlockSpec((MIX_W, 1), lambda i: (0, 0))
    return pl.pallas_call(
        functools.partial(_rwkv_post_fm_kernel, n_seq=n_seq),
        grid=(n // ROW_TILE,),
        in_specs=[pl.BlockSpec((2, CTX_TILE_STEPS, HEAD_DIM, LANES), lambda i: (0, i, 0, 0)),
                  pl.BlockSpec((MIX_W, ROW_TILE), lambda i: (0, i)),
                  pl.BlockSpec((GATE_RANK, ROW_TILE), lambda i: (3 * MIX_W // GATE_RANK, i)),
                  pl.BlockSpec((MIX_W, GATE_RANK), lambda i: (0, 0)),
                  col, col],
        out_specs=pl.BlockSpec((ROW_TILE, MIX_W), lambda i: (i, 0)),
        out_shape=jax.ShapeDtypeStruct((n, MIX_W), F32),
        compiler_params=_params("parallel"),
        name="rwkv_post_fm",
    )(y, bonus, pt_rw, lw["g2t"], lw["lnw_col"], lw["lnb_col"])


def _hgrn_pre_fm_kernel(qv_ref, ff_ref, fb_ref, lb_ref, scan_ref, *, n_seq):
    q = qv_ref[0:MIX_W, :]
    q = q * _sigmoid(q)
    v = qv_ref[MIX_W:2 * MIX_W, :]
    for d, f_ref in enumerate((ff_ref, fb_ref)):
        f_raw = f_ref[...]
        lb = lb_ref[d]
        f = lb + (1.0 - lb) * _sigmoid(f_raw)
        kv = (1.0 - lb) * _sigmoid(-f_raw)
        for j, arr in enumerate((f, kv, q, v)):
            _store_scan_tiles(scan_ref, d, j, arr, n_seq)


def _hgrn_pre_fm(pt_hg, lb_col, n_seq, t_len):
    n = pt_hg.shape[1]
    return pl.pallas_call(
        functools.partial(_hgrn_pre_fm_kernel, n_seq=n_seq),
        grid=(n // ROW_TILE,),
        in_specs=[pl.BlockSpec((2 * MIX_W, ROW_TILE), lambda i: (0, i)),
                  pl.BlockSpec((MIX_W, ROW_TILE), lambda i: (3, i)),
                  pl.BlockSpec((MIX_W, ROW_TILE), lambda i: (4, i)),
                  pl.BlockSpec((2, MIX_W, 1), lambda i: (0, 0, 0))],
        out_specs=pl.BlockSpec((2, 4, CTX_TILE_STEPS, HEAD_DIM, LANES), lambda i: (0, 0, i, 0, 0)),
        out_shape=jax.ShapeDtypeStruct((2, 4, t_len, HEAD_DIM, LANES), F32),
        compiler_params=_params("parallel"),
        name="hgrn_pre_fm",
    )(pt_hg, pt_hg, pt_hg, lb_col)


def _hgrn_post_fm_kernel(o_ref, g_ref, nw_ref, out_ref, *, n_seq):
    o = _load_scan_tiles(o_ref, n_seq)
    ms = _head_sum_rows(o * o) * (1.0 / HEAD_DIM)
    out_ref[...] = (o * lax.rsqrt(ms + NORM_EPS) * nw_ref[...] * _sigmoid(g_ref[...])).T


def _hgrn_post_fm(o, pt_hg, nw_col, n_seq):
    n = pt_hg.shape[1]
    return pl.pallas_call(
        functools.partial(_hgrn_post_fm_kernel, n_seq=n_seq),
        grid=(n // ROW_TILE,),
        in_specs=[pl.BlockSpec((2, CTX_TILE_STEPS, HEAD_DIM, LANES), lambda i: (0, i, 0, 0)),
                  pl.BlockSpec((MIX_W, ROW_TILE), lambda i: (2, i)),
                  pl.BlockSpec((MIX_W, 1), lambda i: (0, 0))],
        out_specs=pl.BlockSpec((ROW_TILE, MIX_W), lambda i: (i, 0)),
        out_shape=jax.ShapeDtypeStruct((n, MIX_W), F32),
        compiler_params=_params("parallel"),
        name="hgrn_post_fm",
    )(o, pt_hg, nw_col)


def _ctx_recurrent_mixers(pt_rw, pt_hg, lw, n_seq, t_len):
    assert n_seq * N_HEADS == LANES
    s0 = jnp.zeros((2, HEAD_DIM, HEAD_DIM, LANES), F32)
    scan_in, bonus = _rwkv_pre_fm(pt_rw, lw, n_seq, t_len)
    y, s_rw = _scan(scan_in, None, s0, delta=True, steps=32, reverse_odd=True)
    ya = _rwkv_post_fm(y, bonus, pt_rw, lw, n_seq)
    scan_in = _hgrn_pre_fm(pt_hg, lw["hg_lb_col"], n_seq, t_len)
    o, s_hg = _scan(scan_in, None, s0, delta=False, steps=32, reverse_odd=True)
    yc = _hgrn_post_fm(o, pt_hg, lw["hg_norm_col"], n_seq)
    return ya, yc, s_rw, s_hg


def _head(x, h):
    return x[:, h * HEAD_DIM:(h + 1) * HEAD_DIM]


def _softmax_av(scores, values, sink=None):
    m = None
    for s in scores:
        mi = jnp.max(s, axis=-1, keepdims=True)
        m = mi if m is None else jnp.maximum(m, mi)
    if sink is not None:
        m = jnp.maximum(m, sink)
    den = None if sink is None else jnp.exp(sink - m)
    out = None
    for s, v in zip(scores, values):
        e = jnp.exp(s - m)
        di = jnp.sum(e, axis=-1, keepdims=True)
        den = di if den is None else den + di
        oi = _bdot(e, v)
        out = oi if out is None else out + oi
    return out / den


def _ctx_attn_kernel(sink_ref, p_ref, y_ref, cache_ref, *, kv_heads, use_sink):
    group = N_HEADS // kv_heads
    kv_w = kv_heads * HEAD_DIM
    q_all = p_ref[:, 0:MIX_W] * ATTN_SCALE
    k_all = p_ref[:, MIX_W:MIX_W + kv_w]
    v_all = p_ref[:, MIX_W + kv_w:MIX_W + 2 * kv_w]
    cache_ref[0] = k_all
    cache_ref[1] = v_all
    outs = []
    for h in range(N_HEADS):
        kvh = h // group
        outs.append(_softmax_av([_bdot_nt(_head(q_all, h), _head(k_all, kvh))], [_head(v_all, kvh)],
                                sink=sink_ref[h] if use_sink else None))
    y_ref[...] = jnp.concatenate(outs, axis=-1)


def _ctx_attention(p, sink, kv_heads, use_sink):
    n_seq, t_len, cols = p.shape
    kv_w = kv_heads * HEAD_DIM
    return pl.pallas_call(
        functools.partial(_ctx_attn_kernel, kv_heads=kv_heads, use_sink=use_sink),
        grid=(n_seq,),
        in_specs=[pl.BlockSpec(memory_space=pltpu.SMEM),
                  pl.BlockSpec((None, t_len, cols), lambda i: (i, 0, 0))],
        out_specs=[pl.BlockSpec((None, t_len, MIX_W), lambda i: (i, 0, 0)),
                   pl.BlockSpec((None, 2, t_len, kv_w), lambda i: (i, 0, 0, 0))],
        out_shape=[jax.ShapeDtypeStruct((n_seq, t_len, MIX_W), F32),
                   jax.ShapeDtypeStruct((n_seq, 2, t_len, kv_w), F32)],
        compiler_params=_params("parallel"),
        name="ctx_attention",
    )(sink, p)


def _nat_latent_kernel(p_ref, ckv_ref, bias_ref, o_ref, *, rows):
    kh = min(NAT_KH, rows)
    n_loc = kh * GRID_W
    ck_all = ckv_ref[0]
    cv_all = ckv_ref[1]
    for r in range(rows):
        rs = min(max(r - kh // 2, 0), rows - kh)
        roff0 = rs - r + NAT_KH - 1
        q_all = p_ref[r * GRID_W:(r + 1) * GRID_W, 0:MIX_W] * ATTN_SCALE
        k_all = p_ref[rs * GRID_W:rs * GRID_W + n_loc, MIX_W:2 * MIX_W]
        v_all = p_ref[rs * GRID_W:rs * GRID_W + n_loc, 2 * MIX_W:3 * MIX_W]
        outs = []
        for h in range(N_HEADS):
            q = _head(q_all, h)
            bias = jnp.concatenate([bias_ref[h, roff0 + 2 * j] for j in range(kh // 2)], axis=-1)
            s_loc = _bdot_nt(q, _head(k_all, h)) + bias
            outs.append(_softmax_av([s_loc, _bdot_nt(q, _head(ck_all, h))],
                                    [_head(v_all, h), _head(cv_all, h)]))
        o_ref[r * GRID_W:(r + 1) * GRID_W, :] = jnp.concatenate(outs, axis=-1)


def _nat_bias_table(rpb):
    q = np.arange(GRID_W)[:, None]
    kc = np.arange(GRID_W)[None, :]
    win = np.clip(q - NAT_KW // 2, 0, GRID_W - NAT_KW)
    ok = (kc >= win) & (kc < win + NAT_KW)
    coff = np.clip(kc - q, -(NAT_KW - 1), NAT_KW - 1) + NAT_KW - 1
    t = jnp.where(ok[None, None], rpb[:, :, coff], MASK_VALUE)
    return jnp.concatenate([t[:, :-1], t[:, 1:]], axis=-1)


def _nat_latent(p_nat, cache, bias, layer, n_seq, t_len, tile0):
    rows = t_len // GRID_W
    return pl.pallas_call(
        functools.partial(_nat_latent_kernel, rows=rows),
        grid=(n_seq,),
        in_specs=[pl.BlockSpec((t_len, NAT_COLS), lambda i: (tile0 + i, 0)),
                  pl.BlockSpec((None, None, 2, cache.shape[3], MIX_W), lambda i: (i, layer, 0, 0, 0)),
                  pl.BlockSpec(bias.shape, lambda i: (0, 0, 0, 0))],
        out_specs=pl.BlockSpec((t_len, MIX_W), lambda i: (i, 0)),
        out_shape=jax.ShapeDtypeStruct((n_seq * t_len, MIX_W), F32),
        compiler_params=_params("parallel"),
        name="nat_latent",
    )(p_nat, cache, bias)


def _rope(x, cos, sin_signed):
    width = x.shape[-1]
    lane = lax.broadcasted_iota(jnp.int32, x.shape, 1)
    half = HEAD_DIM // 4
    partner = jnp.where(lane % (2 * half) < half, pltpu.roll(x, width - half, 1), pltpu.roll(x, half, 1))
    return x * cos + partner * sin_signed


def _swa_latent_kernel(sink_ref, p_ref, ckv_ref, cos_ref, sin_ref, o_ref, *, t_len):
    hd = HEAD_DIM
    kv_w = SWA_KV_HEADS * hd
    group = N_HEADS // SWA_KV_HEADS
    blk = SWA_BLOCK
    nb = t_len // blk
    q = _rope(p_ref[:, 0:MIX_W], cos_ref[...], sin_ref[...]) * ATTN_SCALE
    k = _rope(p_ref[:, MIX_W:MIX_W + kv_w], cos_ref[:, 0:kv_w], sin_ref[:, 0:kv_w])
    v = p_ref[:, MIX_W + kv_w:MIX_W + 2 * kv_w]
    row = lax.broadcasted_iota(jnp.int32, (group * blk, 1), 0)
    ck_all = ckv_ref[0]
    cv_all = ckv_ref[1]
    for n in range(nb):
        lo = max(n - 1, 0) * blk
        hi = min(n + 2, nb) * blk
        qpos = n * blk + row % blk
        kpos = lo + lax.broadcasted_iota(jnp.int32, (group * blk, hi - lo), 1)
        ok = jnp.abs(kpos - qpos) <= SWA_WINDOW
        heads = [None] * N_HEADS
        for kvh in range(SWA_KV_HEADS):
            qs = jnp.concatenate([_head(q[n * blk:(n + 1) * blk], kvh * group + g) for g in range(group)],
                                 axis=0)
            sink = jnp.concatenate([jnp.full((blk, 1), sink_ref[kvh * group + g], F32)
                                    for g in range(group)], axis=0)
            s_loc = jnp.where(ok, _bdot_nt(qs, _head(k[lo:hi], kvh)), MASK_VALUE)
            s_ctx = _bdot_nt(qs, _head(ck_all, kvh))
            o = _softmax_av([s_loc, s_ctx], [_head(v[lo:hi], kvh), _head(cv_all, kvh)], sink=sink)
            for g in range(group):
                heads[kvh * group + g] = o[g * blk:(g + 1) * blk]
        o_ref[n * blk:(n + 1) * blk, :] = jnp.concatenate(heads, axis=-1)


def _rope_tables(t_len):
    t = jnp.arange(t_len)
    half = HEAD_DIM // 2
    nf = half // 2
    inv = 1.0 / (ROPE_BASE ** (jnp.arange(nf, dtype=F32) / nf))
    cols = []
    for pos in ((t // GRID_W).astype(F32), (t % GRID_W).astype(F32)):
        ang = pos[:, None] * inv[None, :]
        cols.append((jnp.cos(ang), jnp.sin(ang)))
    cos = jnp.concatenate([cols[0][0], cols[0][0], cols[1][0], cols[1][0]], axis=-1)
    sin = jnp.concatenate([-cols[0][1], cols[0][1], -cols[1][1], cols[1][1]], axis=-1)
    return jnp.tile(cos, (1, N_HEADS)), jnp.tile(sin, (1, N_HEADS))


def _swa_latent(p_swa, cache, sink, rope, layer, n_seq, t_len, tile0):
    kv_w = SWA_KV_HEADS * HEAD_DIM
    return pl.pallas_call(
        functools.partial(_swa_latent_kernel, t_len=t_len),
        grid=(n_seq,),
        in_specs=[pl.BlockSpec(memory_space=pltpu.SMEM),
                  pl.BlockSpec((t_len, SWA_COLS), lambda i: (tile0 + i, 0)),
                  pl.BlockSpec((None, None, 2, cache.shape[3], kv_w), lambda i: (i, layer, 0, 0, 0)),
                  pl.BlockSpec((t_len, MIX_W), lambda i: (0, 0)),
                  pl.BlockSpec((t_len, MIX_W), lambda i: (0, 0))],
        out_specs=pl.BlockSpec((t_len, MIX_W), lambda i: (i, 0)),
        out_shape=jax.ShapeDtypeStruct((n_seq * t_len, MIX_W), F32),
        compiler_params=_params("parallel"),
        name="swa_latent",
    )(sink, p_swa, cache, rope[0], rope[1])


LAT_TILE = 256


def _exact_row_select(sel, x):
    hi = x.astype(BF16)
    r1 = x - hi.astype(F32)
    mid = r1.astype(BF16)
    lo = (r1 - mid.astype(F32)).astype(BF16)
    dot = lambda a: jnp.dot(sel, a, preferred_element_type=F32)
    return dot(hi) + dot(mid) + dot(lo)


def _latent_perms(n_seq):
    n_chain = 2 * n_seq * N_HEADS
    vs_n = HEAD_DIM // (LANES // n_chain)
    rows = np.zeros((2, n_seq, MIX_W, HEAD_DIM * n_chain), np.float32)
    vals = np.zeros((2, n_seq, MIX_W, HEAD_DIM * n_chain), np.float32)
    for d in range(2):
        for b in range(n_seq):
            for h in range(N_HEADS):
                c = (d * n_seq + b) * N_HEADS + h
                for e in range(HEAD_DIM):
                    rows[d, b, h * HEAD_DIM + e, e * n_chain + c] = 1.0
                    vals[d, b, h * HEAD_DIM + e, (e % vs_n) * LANES + (e // vs_n) * n_chain + c] = 1.0
    rev = np.eye(LAT_TILE, dtype=np.float32)[::-1]
    return (jnp.asarray(rows, dtype=BF16), jnp.asarray(vals, dtype=BF16),
            jnp.asarray(vals.transpose(0, 1, 3, 2), dtype=BF16), jnp.asarray(rev, dtype=BF16))


def _scan_order_tile(pf_ref, pb_ref, hf_ref, hb_ref, rev_ref, d, b):
    first = pl.program_id(0) == 0
    if d == 0:
        x, halo = pf_ref[b], hf_ref[b, SUBLANES - 1:SUBLANES, :]
    else:
        x, halo = _exact_row_select(rev_ref[...], pb_ref[b]), hb_ref[b, 0:1, :]
    return x, jnp.where(first, 0.0, halo)


def _lat_rwkv_pre_kernel(pf_ref, pb_ref, hf_ref, hb_ref, rev_ref, prow_ref, pval_ref, mu_rkv_ref, mu_lo_ref,
                         wl_ref, b0_ref, kk_ref, ka_ref, rk_ref, seg_ref, z_ref, v_ref, bn_ref, *, n_seq):
    seg = seg_ref[...]
    row0 = lax.broadcasted_iota(jnp.int32, (LAT_TILE, 1), 0) == 0
    acc = [None] * 7
    for d in range(2):
        for b in range(n_seq):
            x, halo = _scan_order_tile(pf_ref, pb_ref, hf_ref, hb_ref, rev_ref, d, b)
            prev = jnp.where(row0, halo, pltpu.roll(x, 1, 0))
            rkv, rkv_prev = x[:, 0:3 * MIX_W], prev[:, 0:3 * MIX_W]
            rkv = rkv + (rkv_prev - rkv) * mu_rkv_ref[d]
            r, k, v = rkv[:, 0:MIX_W], rkv[:, MIX_W:2 * MIX_W], rkv[:, 2 * MIX_W:3 * MIX_W]
            c0 = 3 * MIX_W + GATE_RANK + 2 * LORA_RANK * d
            lo, lo_prev = x[:, c0:c0 + 2 * LORA_RANK], prev[:, c0:c0 + 2 * LORA_RANK]
            lo = lo + (lo_prev - lo) * mu_lo_ref[d]
            lane = lax.broadcasted_iota(jnp.int32, lo.shape, 1)
            lo = jnp.where(lane < LORA_RANK, jnp.tanh(lo), lo)
            pre = _bdot(lo, wl_ref[d]) + b0_ref[d]
            w = jnp.exp(-RWKV_DECAY_SCALE * _sigmoid(pre[:, 0:MIX_W]))
            a = _sigmoid(pre[:, MIX_W:2 * MIX_W])
            kk = k * kk_ref[...]
            kk = kk / jnp.maximum(jnp.sqrt(_seg_sum(kk * kk, seg)), 1e-12)
            k = k * (1.0 + (a - 1.0) * ka_ref[...])
            bonus = _seg_sum(r * k * rk_ref[...], seg) * v
            placed = [_exact_select_dot(arr, prow_ref[d, b]) for arr in (-kk, w, kk * a, k, r)]
            placed += [_exact_select_dot(arr, pval_ref[d, b]) for arr in (v, bonus)]
            acc = [p if s is None else s + p for s, p in zip(acc, placed)]
    for j in range(5):
        z_ref[j] = acc[j]
    v_ref[...] = acc[5]
    bn_ref[...] = acc[6]


def _lat_tile_specs(n_seq, t_len, width):
    n_tiles = t_len // LAT_TILE
    per = LAT_TILE // SUBLANES
    return [pl.BlockSpec((n_seq, LAT_TILE, width), lambda i: (0, i, 0)),
            pl.BlockSpec((n_seq, LAT_TILE, width), lambda i: (0, n_tiles - 1 - i, 0)),
            pl.BlockSpec((n_seq, SUBLANES, width), lambda i: (0, jnp.maximum(i * per - 1, 0), 0)),
            pl.BlockSpec((n_seq, SUBLANES, width),
                         lambda i: (0, jnp.minimum((n_tiles - i) * per, n_tiles * per - 1), 0))]


def _lat_rwkv_pre(p_rw, lw, seg, perms, n_seq, t_len):
    prow, pval, _, rev = perms
    width = prow.shape[-1]
    p3 = p_rw.reshape(n_seq, t_len, RW_COLS)
    whole = lambda a: pl.BlockSpec(a.shape, lambda i: (0,) * a.ndim)
    params = [lw["mu_rkv"], lw["mu_lo"], lw["w_lora"], lw["b_lora"], lw["rw_kk"], lw["rw_ka"], lw["rw_rk"], seg]
    return pl.pallas_call(
        functools.partial(_lat_rwkv_pre_kernel, n_seq=n_seq),
        grid=(t_len // LAT_TILE,),
        in_specs=_lat_tile_specs(n_seq, t_len, RW_COLS) + [whole(a) for a in (rev, prow, pval, *params)],
        out_specs=[pl.BlockSpec((5, LAT_TILE, width), lambda i: (0, i, 0)),
                   pl.BlockSpec((LAT_TILE, width), lambda i: (i, 0)),
                   pl.BlockSpec((LAT_TILE, width), lambda i: (i, 0))],
        out_shape=[jax.ShapeDtypeStruct((5, t_len, width), F32),
                   jax.ShapeDtypeStruct((t_len, width), F32),
                   jax.ShapeDtypeStruct((t_len, width), F32)],
        compiler_params=_params("parallel"),
        name="lat_rwkv_pre",
    )(p3, p3, p3, p3, rev, prow, pval, *params)


def _unplace(yf_ref, yb_ref, rev_ref, pinv_ref, b):
    fwd = _exact_select_dot(yf_ref[...], pinv_ref[0, b])
    bwd = _exact_select_dot(yb_ref[...], pinv_ref[1, b])
    return fwd + _exact_row_select(rev_ref[...], bwd)


def _lat_rwkv_post_kernel(yf_ref, yb_ref, bf_ref, bb_ref, rev_ref, pinv_ref, gh_ref, g2_ref, lnw_ref, lnb_ref,
                          seg_ref, o_ref):
    b = pl.program_id(0)
    seg = seg_ref[...]
    y = _unplace(yf_ref, yb_ref, rev_ref, pinv_ref, b)
    mu = _seg_sum(y, seg) * (1.0 / HEAD_DIM)
    yc = y - mu
    var = _seg_sum(yc * yc, seg) * (1.0 / HEAD_DIM)
    y = yc * lax.rsqrt(var + RWKV_LN_EPS) * lnw_ref[...] + lnb_ref[...]
    y = y + _unplace(bf_ref, bb_ref, rev_ref, pinv_ref, b)
    o_ref[...] = y * _bdot(_sigmoid(gh_ref[...]), g2_ref[...])


def _lat_post_specs(t_len, width):
    n_tiles = t_len // LAT_TILE
    return [pl.BlockSpec((LAT_TILE, width), lambda b, i: (i, 0)),
            pl.BlockSpec((LAT_TILE, width), lambda b, i: (n_tiles - 1 - i, 0))]


def _lat_rwkv_post(y, bonus, p_rw, lw, seg, perms, n_seq, t_len):
    _, _, pinv, rev = perms
    width = y.shape[-1]
    n_tiles = t_len // LAT_TILE
    whole = lambda a: pl.BlockSpec(a.shape, lambda b, i: (0,) * a.ndim)
    params = [lw["rw_g2"], lw["rw_lnx_w"], lw["rw_lnx_b"], seg]
    return pl.pallas_call(
        _lat_rwkv_post_kernel,
        grid=(n_seq, n_tiles),
        in_specs=_lat_post_specs(t_len, width) * 2 + [whole(rev), whole(pinv),
                 pl.BlockSpec((LAT_TILE, GATE_RANK), lambda b, i: (b * n_tiles + i, 3 * MIX_W // GATE_RANK))]
        + [whole(a) for a in params],
        out_specs=pl.BlockSpec((LAT_TILE, MIX_W), lambda b, i: (b * n_tiles + i, 0)),
        out_shape=jax.ShapeDtypeStruct((n_seq * t_len, MIX_W), F32),
        compiler_params=_params("parallel", "parallel"),
        name="lat_rwkv_post",
    )(y, y, bonus, bonus, rev, pinv, p_rw, *params)


def _lat_hgrn_pre_kernel(pf_ref, pb_ref, rev_ref, prow_ref, pval_ref, lb_ref, z_ref, v_ref, *, n_seq):
    acc = [None] * 4
    for d in range(2):
        for b in range(n_seq):
            x = pf_ref[b] if d == 0 else _exact_row_select(rev_ref[...], pb_ref[b])
            q = x[:, 0:MIX_W]
            f_raw = x[:, (3 + d) * MIX_W:(4 + d) * MIX_W]
            lb = lb_ref[d]
            f = lb + (1.0 - lb) * _sigmoid(f_raw)
            kv = (1.0 - lb) * _sigmoid(-f_raw)
            placed = [_exact_select_dot(arr, prow_ref[d, b]) for arr in (f, kv, q * _sigmoid(q))]
            placed.append(_exact_select_dot(x[:, MIX_W:2 * MIX_W], pval_ref[d, b]))
            acc = [p if s is None else s + p for s, p in zip(acc, placed)]
    for j in range(3):
        z_ref[j] = acc[j]
    v_ref[...] = acc[3]


def _lat_hgrn_pre(p_hg, lb, perms, n_seq, t_len):
    prow, pval, _, rev = perms
    width = prow.shape[-1]
    p3 = p_hg.reshape(n_seq, t_len, HG_COLS)
    whole = lambda a: pl.BlockSpec(a.shape, lambda i: (0,) * a.ndim)
    return pl.pallas_call(
        functools.partial(_lat_hgrn_pre_kernel, n_seq=n_seq),
        grid=(t_len // LAT_TILE,),
        in_specs=_lat_tile_specs(n_seq, t_len, HG_COLS)[:2] + [whole(a) for a in (rev, prow, pval, lb)],
        out_specs=[pl.BlockSpec((3, LAT_TILE, width), lambda i: (0, i, 0)),
                   pl.BlockSpec((LAT_TILE, width), lambda i: (i, 0))],
        out_shape=[jax.ShapeDtypeStruct((3, t_len, width), F32),
                   jax.ShapeDtypeStruct((t_len, width), F32)],
        compiler_params=_params("parallel"),
        name="lat_hgrn_pre",
    )(p3, p3, rev, prow, pval, lb)


def _lat_hgrn_post_kernel(of_ref, ob_ref, rev_ref, pinv_ref, g_ref, nw_ref, seg_ref, out_ref):
    o = _unplace(of_ref, ob_ref, rev_ref, pinv_ref, pl.program_id(0))
    ms = _seg_sum(o * o, seg_ref[...]) * (1.0 / HEAD_DIM)
    out_ref[...] = o * lax.rsqrt(ms + NORM_EPS) * nw_ref[...] * _sigmoid(g_ref[...])


def _lat_hgrn_post(o, p_hg, hg_norm, seg, perms, n_seq, t_len):
    _, _, pinv, rev = perms
    width = o.shape[-1]
    n_tiles = t_len // LAT_TILE
    whole = lambda a: pl.BlockSpec(a.shape, lambda b, i: (0,) * a.ndim)
    return pl.pallas_call(
        _lat_hgrn_post_kernel,
        grid=(n_seq, n_tiles),
        in_specs=_lat_post_specs(t_len, width) + [whole(rev), whole(pinv),
                 pl.BlockSpec((LAT_TILE, MIX_W), lambda b, i: (b * n_tiles + i, 2)),
                 whole(hg_norm), whole(seg)],
        out_specs=pl.BlockSpec((LAT_TILE, MIX_W), lambda b, i: (b * n_tiles + i, 0)),
        out_shape=jax.ShapeDtypeStruct((n_seq * t_len, MIX_W), F32),
        compiler_params=_params("parallel", "parallel"),
        name="lat_hgrn_post",
    )(o, o, rev, pinv, p_hg, hg_norm, seg)


def _lat_recurrent_mixers(p_rw, p_hg, lw, seg, perms, n_seq, t_len, state_rw, state_hg):
    n_chain = 2 * n_seq * N_HEADS
    as_rows = lambda z: z.reshape(1, z.shape[0], t_len, HEAD_DIM, n_chain)
    as_tiles = lambda v: v.reshape(1, t_len, v.shape[-1] // LANES, LANES)
    z, v, bonus = _lat_rwkv_pre(p_rw, lw, seg, perms, n_seq, t_len)
    y, _ = _scan(as_rows(z), as_tiles(v), _state_to_scan(state_rw, True), delta=True, steps=32)
    ya = _lat_rwkv_post(y.reshape(v.shape), bonus, p_rw, lw, seg, perms, n_seq, t_len)
    z, v = _lat_hgrn_pre(p_hg, lw["hg_lb"], perms, n_seq, t_len)
    o, _ = _scan(as_rows(z), as_tiles(v), _state_to_scan(state_hg, False), delta=False, steps=32)
    yc = _lat_hgrn_post(o.reshape(v.shape), p_hg, lw["hg_norm"], seg, perms, n_seq, t_len)
    return ya, yc


def _recurrent_mixers(p_rw, p_hg, lw, seg, n_seq, t_len, tile0, states):
    ctx = states is None
    seq0 = tile0 * ROW_TILE // t_len
    n_tok = n_seq * t_len
    steps = 32

    scan_in, bonus = _rwkv_pre(p_rw, lw, seg, n_seq, t_len, seq0)
    rows, v = _to_scan_layout(scan_in, n_seq, t_len, ctx)
    s0 = jnp.zeros((2, HEAD_DIM, HEAD_DIM, LANES), F32) if ctx else _state_to_scan(states[0], True)
    y, s_rw = _scan(rows, v, s0, delta=True, steps=steps)
    ya = _rwkv_post(_from_scan_layout(y, n_seq, t_len, ctx), bonus, p_rw, lw, seg, tile0)

    scan_in = _hgrn_pre(p_hg, lw["hg_lb"], n_tok, tile0)
    rows, v = _to_scan_layout(scan_in, n_seq, t_len, ctx)
    s0 = jnp.zeros((2, HEAD_DIM, HEAD_DIM, LANES), F32) if ctx else _state_to_scan(states[1], False)
    o, s_hg = _scan(rows, v, s0, delta=False, steps=steps)
    yc = _hgrn_post(_from_scan_layout(o, n_seq, t_len, ctx), p_hg, lw["hg_norm"], seg, tile0)
    return ya, yc, s_rw, s_hg


def kernel(x_prompt, x_sample, cache_nat_kv, cache_swa_kv, state_rwkv, state_hgrn, c, c_ctx, norm_g, mod_w, mod_b, w_in, w_out, rw_mu_rkv, rw_mu_lora, rw_w0, rw_w2, rw_a0, rw_a2, rw_g2, rw_kk, rw_ka, rw_rk, rw_lnx_w, rw_lnx_b, nat_rpb, hg_lb_logits, hg_norm, swa_sink, ffn_w1, ffn_w2):
    n_ctx, t_ctx, _ = x_prompt.shape
    n_lat, t_lat, _ = x_sample.shape
    past = cache_nat_kv.shape[3]

    cond = jnp.zeros((MOD_ROWS, D_MODEL), F32).at[0].set(c_ctx).at[1:1 + n_lat].set(c)
    mod = _modulation_all(cond, mod_w, mod_b)

    w_in_bf, w_out_bf = w_in.astype(BF16), w_out.astype(BF16)
    w1_bf, w2_bf = ffn_w1.astype(BF16), ffn_w2.astype(BF16)
    cuts = np.cumsum((0,) + SLABS)
    lat_slabs = tuple(w_in_bf[:, :, cuts[j]:cuts[j + 1]] for j in range(4))
    ctx_modes = ("fm", "seq", "fm", "seq")
    ctx_y_modes = ("tok", "seq", "tok", "seq")
    lat_modes = ("tok",) * 4
    ctx_slabs = tuple(w.transpose(0, 2, 1) if m == "fm" else w for w, m in zip(lat_slabs, ctx_modes))

    lb_sm = jax.nn.softmax(hg_lb_logits.astype(F32), axis=1)
    hg_lb = jnp.cumsum(lb_sm, axis=1) - lb_sm[:, :1]

    head_of = np.arange(MIX_W) // HEAD_DIM
    seg = jnp.asarray((head_of[:, None] == head_of[None, :]).astype(np.float32))
    rope = _rope_tables(t_lat)
    lat_perms = _latent_perms(n_lat)
    zeros_lora = jnp.zeros((LORA_RANK, MIX_W), F32)
    cache_nat = cache_nat_kv.reshape(n_lat, DEPTH, 2, past, MIX_W)
    cache_swa = cache_swa_kv.reshape(n_lat, DEPTH, 2, past, SWA_KV_HEADS * HEAD_DIM)

    xp = x_prompt.transpose(1, 0, 2).reshape(t_ctx * n_ctx, D_MODEL)
    xs = x_sample.reshape(n_lat * t_lat, D_MODEL)
    ctx_row, lat_row = _mod_row(t_ctx, 0), _mod_row(t_lat, 1)
    nat_out, swa_out, rw_out, hg_out = [], [], [], []
    for l in range(DEPTH):
        lw = {
            "mu_rkv": rw_mu_rkv[l].reshape(2, 1, 3 * MIX_W),
            "mu_lo": rw_mu_lora[l].reshape(2, 1, 2 * LORA_RANK),
            "w_lora": jnp.stack([jnp.concatenate(
                [jnp.concatenate([rw_w2[l, d], zeros_lora], axis=1),
                 jnp.concatenate([zeros_lora, rw_a2[l, d]], axis=1)], axis=0) for d in range(2)]),
            "b_lora": jnp.concatenate([rw_w0[l], rw_a0[l]], axis=-1).reshape(2, 1, 2 * MIX_W),
            "rw_kk": rw_kk[l].reshape(1, MIX_W), "rw_ka": rw_ka[l].reshape(1, MIX_W),
            "rw_rk": rw_rk[l].reshape(1, MIX_W), "rw_g2": rw_g2[l],
            "rw_lnx_w": rw_lnx_w[l].reshape(1, MIX_W), "rw_lnx_b": rw_lnx_b[l].reshape(1, MIX_W),
            "hg_lb": hg_lb[:, l].reshape(2, 1, MIX_W), "hg_norm": hg_norm[l].reshape(1, MIX_W),
            "mu_rkv_col": rw_mu_rkv[l].reshape(2, 3 * MIX_W, 1), "mu_lo_col": rw_mu_lora[l].reshape(2, 2 * LORA_RANK, 1),
            "w2t": rw_w2[l].transpose(0, 2, 1), "a2t": rw_a2[l].transpose(0, 2, 1),
            "w0_col": rw_w0[l].reshape(2, MIX_W, 1), "a0_col": rw_a0[l].reshape(2, MIX_W, 1),
            "kk_col": rw_kk[l].reshape(MIX_W, 1), "ka_col": rw_ka[l].reshape(MIX_W, 1),
            "rk_col": rw_rk[l].reshape(MIX_W, 1), "g2t": rw_g2[l].T,
            "lnw_col": rw_lnx_w[l].reshape(MIX_W, 1), "lnb_col": rw_lnx_b[l].reshape(MIX_W, 1),
            "hg_lb_col": hg_lb[:, l].reshape(2, MIX_W, 1), "hg_norm_col": hg_norm[l].reshape(MIX_W, 1),
        }
        sink = swa_sink[l]

        pt_rw, p_nat, pt_hg, p_swa = _in_projection(xp, norm_g, mod, ctx_slabs, l, ctx_row, ctx_modes, n_ctx)
        ya, yc, s_rw, s_hg = _ctx_recurrent_mixers(pt_rw, pt_hg, lw, n_ctx, t_ctx)
        yb, cnat = _ctx_attention(p_nat, sink, N_HEADS, False)
        yd, cswa = _ctx_attention(p_swa, sink, SWA_KV_HEADS, True)
        xp = _out_projection_ffn(xp, (ya, yb, yc, yd), norm_g, mod, w_out_bf, w1_bf, w2_bf, l, ctx_row,
                                 ctx_y_modes, n_ctx)
        nat_out.append(cnat.reshape(n_ctx, 2, t_ctx, N_HEADS, HEAD_DIM))
        swa_out.append(cswa.reshape(n_ctx, 2, t_ctx, SWA_KV_HEADS, HEAD_DIM))
        rw_out.append(_state_from_scan(s_rw, n_ctx, True))
        hg_out.append(_state_from_scan(s_hg, n_ctx, False))

        p_rw, p_nat, p_hg, p_swa = _in_projection(xs, norm_g, mod, lat_slabs, l, lat_row, lat_modes)
        ya, yc = _lat_recurrent_mixers(p_rw, p_hg, lw, seg, lat_perms, n_lat, t_lat,
                                       state_rwkv[:, l], state_hgrn[:, l])
        yb = _nat_latent(p_nat, cache_nat, _nat_bias_table(nat_rpb[l]), l, n_lat, t_lat, 0)
        yd = _swa_latent(p_swa, cache_swa, sink, rope, l, n_lat, t_lat, 0)
        xs = _out_projection_ffn(xs, (ya, yb, yc, yd), norm_g, mod, w_out_bf, w1_bf, w2_bf, l, lat_row,
                                 lat_modes)

    return (xp.reshape(t_ctx, n_ctx, D_MODEL).transpose(1, 0, 2), xs.reshape(x_sample.shape),
            jnp.stack(nat_out, axis=1), jnp.stack(swa_out, axis=1),
            jnp.stack(rw_out, axis=1), jnp.stack(hg_out, axis=1))
```

```python
import functools

import numpy as np
import jax
import jax.numpy as jnp
from jax import lax
from jax.experimental import pallas as pl
from jax.experimental.pallas import tpu as pltpu

F32 = jnp.float32
BF16 = jnp.bfloat16
HIGHEST = lax.Precision.HIGHEST

D_MODEL = 1024
DEPTH = 4
GRID_W = 64
HEAD_DIM = 64
N_HEADS = 4
MIX_W = N_HEADS * HEAD_DIM
SWA_KV_HEADS = 2
LORA_RANK = 64
GATE_RANK = 128
RWKV_DECAY_SCALE = 0.6065306597126334
RWKV_LN_EPS = 64e-5
NAT_KH = 8
NAT_KW = 16
SWA_WINDOW = 128
SWA_BLOCK = 128
ROPE_BASE = 10000.0
FFN_HIDDEN = 4 * D_MODEL
NORM_EPS = 1e-6
MASK_VALUE = -1e30
N_MOD = 6
ATTN_SCALE = HEAD_DIM ** -0.5

RW_COLS = 3 * MIX_W + GATE_RANK + 4 * LORA_RANK
NAT_COLS = 3 * MIX_W
HG_COLS = 5 * MIX_W
SWA_COLS = MIX_W + 2 * SWA_KV_HEADS * HEAD_DIM
SLABS = (RW_COLS, NAT_COLS, HG_COLS, SWA_COLS)
D_IN = sum(SLABS)

LANES = 128
SUBLANES = 8
ROW_TILE = 256
MOD_ROWS = 8
SCAN_KEY_CHUNK = 16


def _sigmoid(x):
    return 1.0 / (1.0 + jnp.exp(-x))


def _rms_rows(x, g):
    ms = jnp.mean(x * x, axis=-1, keepdims=True)
    return x * lax.rsqrt(ms + NORM_EPS) * g


def _bdot(a, b):
    return jnp.dot(a.astype(BF16), b.astype(BF16), preferred_element_type=F32)


def _bdot_nt(a, b):
    return lax.dot_general(a.astype(BF16), b.astype(BF16), (((1,), (1,)), ((), ())),
                           preferred_element_type=F32)


def _seg_sum(x, seg):
    return jnp.dot(x, seg, preferred_element_type=F32, precision=HIGHEST)


def _exact_select_dot(x, sel):
    return _select_dot3(_split3(x), sel)


def _split3(x):
    hi = x.astype(BF16)
    r1 = x - hi.astype(F32)
    mid = r1.astype(BF16)
    lo = (r1 - mid.astype(F32)).astype(BF16)
    return hi, mid, lo


def _select_dot3(parts, sel):
    hi, mid, lo = (jnp.dot(a, sel, preferred_element_type=F32) for a in parts)
    return hi + mid + lo


def _params(*sem):
    return pltpu.CompilerParams(dimension_semantics=sem)


def _resident(shape, index_map):
    return pl.BlockSpec(shape, index_map, pipeline_mode=pl.Buffered(1))


def _mod_kernel(cond_ref, w_ref, b_ref, o_ref):
    cnd = cond_ref[...]
    s = cnd * _sigmoid(cnd)
    o_ref[...] = _bdot(s, w_ref[...]) + b_ref[...]


def _modulation_all(cond, mod_w, mod_b):
    tn = 1536
    n = N_MOD * D_MODEL
    return pl.pallas_call(
        _mod_kernel,
        grid=(DEPTH, n // tn),
        in_specs=[pl.BlockSpec((MOD_ROWS, D_MODEL), lambda l, j: (0, 0)),
                  pl.BlockSpec((None, D_MODEL, tn), lambda l, j: (l, 0, j)),
                  pl.BlockSpec((None, 1, tn), lambda l, j: (l, 0, j))],
        out_specs=pl.BlockSpec((None, MOD_ROWS, tn), lambda l, j: (l, 0, j)),
        out_shape=jax.ShapeDtypeStruct((DEPTH, MOD_ROWS, n), F32),
        compiler_params=_params("parallel", "parallel"),
        name="modulation",
    )(cond, mod_w, mod_b.reshape(DEPTH, 1, n))


def _mod_row(seq_len, first_row):
    if first_row == 0:
        return lambda i: 0
    return lambda i: first_row + (i * ROW_TILE) // seq_len


SEQ_TILE_STEPS = SUBLANES


def _tile_perm(n_seq, to_seq_major):
    r = np.arange(ROW_TILE)
    src = (r % SEQ_TILE_STEPS) * n_seq + r // SEQ_TILE_STEPS
    p = (src[:, None] == r[None, :])
    return jnp.asarray(p if to_seq_major else p.T, dtype=BF16)


def _permute_rows(perm_ref, xb):
    return jnp.dot(perm_ref[...], xb, preferred_element_type=F32).astype(BF16)


def _inproj_kernel(*refs, row_fn, modes):
    x_ref, g_ref, mod_ref = refs[0:3]
    w_refs = refs[3:7]
    perm_ref = refs[7] if "seq" in modes else None
    o_refs = refs[-4:]
    row = row_fn(pl.program_id(0))
    shift = mod_ref[pl.ds(row, 1), 0:D_MODEL]
    scale = mod_ref[pl.ds(row, 1), D_MODEL:2 * D_MODEL]
    h = _rms_rows(x_ref[...], g_ref[0:1, :]) * (1.0 + scale) + shift
    hb = h.astype(BF16)
    hb_seq = _permute_rows(perm_ref, hb) if perm_ref is not None else None
    for w_ref, o_ref, mode in zip(w_refs, o_refs, modes):
        if mode == "fm":
            o_ref[...] = lax.dot_general(w_ref[...], hb, (((1,), (1,)), ((), ())),
                                         preferred_element_type=F32)
        elif mode == "seq":
            o_ref[...] = jnp.dot(hb_seq, w_ref[...], preferred_element_type=F32).reshape(o_ref.shape)
        else:
            o_ref[...] = jnp.dot(hb, w_ref[...], preferred_element_type=F32)


def _in_projection(x, norm_g, mod, w_slabs, layer, row_fn, modes, n_seq=None):
    n = x.shape[0]
    out_specs, out_shape = [], []
    for width, mode in zip(SLABS, modes):
        if mode == "fm":
            out_specs.append(pl.BlockSpec((width, ROW_TILE), lambda i: (0, i)))
            out_shape.append(jax.ShapeDtypeStruct((width, n), F32))
        elif mode == "seq":
            out_specs.append(pl.BlockSpec((n_seq, SEQ_TILE_STEPS, width), lambda i: (0, i, 0)))
            out_shape.append(jax.ShapeDtypeStruct((n_seq, n // n_seq, width), F32))
        else:
            out_specs.append(pl.BlockSpec((ROW_TILE, width), lambda i: (i, 0)))
            out_shape.append(jax.ShapeDtypeStruct((n, width), F32))
    in_specs = [pl.BlockSpec((ROW_TILE, D_MODEL), lambda i: (i, 0)),
                pl.BlockSpec((None, 4, D_MODEL), lambda i: (layer, 0, 0)),
                pl.BlockSpec((None, MOD_ROWS, N_MOD * D_MODEL), lambda i: (layer, 0, 0))]
    in_specs += [_resident((None,) + w.shape[1:], lambda i: (layer, 0, 0)) for w in w_slabs]
    args = [x, norm_g, mod, *w_slabs]
    if "seq" in modes:
        assert n_seq * SEQ_TILE_STEPS == ROW_TILE
        in_specs.append(pl.BlockSpec((ROW_TILE, ROW_TILE), lambda i: (0, 0)))
        args.append(_tile_perm(n_seq, True))
    return pl.pallas_call(
        functools.partial(_inproj_kernel, row_fn=row_fn, modes=modes),
        grid=(n // ROW_TILE,),
        in_specs=in_specs,
        out_specs=out_specs,
        out_shape=out_shape,
        compiler_params=_params("parallel"),
        name="in_projection",
    )(*args)


def _outproj_ffn_kernel(*refs, row_fn, modes):
    x_ref = refs[0]
    y_refs = refs[1:5]
    g_ref, mod_ref, wo_ref, w1_ref, w2_ref = refs[5:10]
    perm_ref = refs[10] if "seq" in modes else None
    o_ref = refs[-1]
    row = row_fn(pl.program_id(0))
    mod = lambda j: mod_ref[pl.ds(row, 1), j * D_MODEL:(j + 1) * D_MODEL]
    x = x_ref[...]
    acc = None
    for j, (y_ref, mode) in enumerate(zip(y_refs, modes)):
        yb = y_ref[...].reshape(ROW_TILE, MIX_W).astype(BF16)
        if mode == "seq":
            yb = _permute_rows(perm_ref, yb)
        part = jnp.dot(yb, wo_ref[j * MIX_W:(j + 1) * MIX_W, :], preferred_element_type=F32)
        acc = part if acc is None else acc + part
    x = x + mod(2) * _rms_rows(acc, g_ref[1:2, :])
    h = (_rms_rows(x, g_ref[2:3, :]) * (1.0 + mod(4)) + mod(3)).astype(BF16)
    f = None
    for j in range(FFN_HIDDEN // D_MODEL):
        u = jnp.dot(h, w1_ref[:, j * D_MODEL:(j + 1) * D_MODEL], preferred_element_type=F32)
        u = jnp.square(jnp.maximum(u, 0.0)).astype(BF16)
        part = jnp.dot(u, w2_ref[j * D_MODEL:(j + 1) * D_MODEL, :], preferred_element_type=F32)
        f = part if f is None else f + part
    o_ref[...] = x + mod(5) * _rms_rows(f, g_ref[3:4, :])


def _out_projection_ffn(x, ys, norm_g, mod, w_out_bf, w1_bf, w2_bf, layer, row_fn, modes, n_seq=None):
    n = x.shape[0]
    tile = lambda w: pl.BlockSpec((ROW_TILE, w), lambda i: (i, 0))
    y_specs = [pl.BlockSpec((n_seq, SEQ_TILE_STEPS, MIX_W), lambda i: (0, i, 0)) if m == "seq" else tile(MIX_W)
               for m in modes]
    in_specs = [tile(D_MODEL)] + y_specs + [
        pl.BlockSpec((None, 4, D_MODEL), lambda i: (layer, 0, 0)),
        pl.BlockSpec((None, MOD_ROWS, N_MOD * D_MODEL), lambda i: (layer, 0, 0)),
        _resident((None, D_MODEL, D_MODEL), lambda i: (layer, 0, 0)),
        _resident((None, D_MODEL, FFN_HIDDEN), lambda i: (layer, 0, 0)),
        _resident((None, FFN_HIDDEN, D_MODEL), lambda i: (layer, 0, 0))]
    args = [x, *ys, norm_g, mod, w_out_bf, w1_bf, w2_bf]
    if "seq" in modes:
        in_specs.append(pl.BlockSpec((ROW_TILE, ROW_TILE), lambda i: (0, 0)))
        args.append(_tile_perm(n_seq, False))
    return pl.pallas_call(
        functools.partial(_outproj_ffn_kernel, row_fn=row_fn, modes=modes),
        grid=(n // ROW_TILE,),
        in_specs=in_specs,
        out_specs=tile(D_MODEL),
        out_shape=jax.ShapeDtypeStruct((n, D_MODEL), F32),
        compiler_params=_params("parallel"),
        name="out_projection_ffn",
    )(*args)


def _scan_kernel(*refs, n_rows, delta, steps, v_tiles, packed, reverse_odd):
    if packed:
        rows_ref, v_ref, s0_ref, e_ref, y_ref, sfin_ref, s_scr, x_scr, t_scr, v_scr, y_scr = refs
    else:
        rows_ref, v_ref, s0_ref, y_ref, sfin_ref, s_scr = refs
    tb = pl.program_id(1)

    @pl.when(tb == 0)
    def _():
        s_scr[...] = s0_ref[...]

    if packed:
        per_col = e_ref.shape[0]
        n_col = HEAD_DIM // per_col
        for j in range(n_rows):
            for col in range(n_col):
                t_scr[:, col, :] = rows_ref[j, :, col * LANES:(col + 1) * LANES]
            parts = _split3(t_scr[...].reshape(steps * n_col, LANES))
            for s in range(per_col):
                x_scr[j, :, s * n_col:(s + 1) * n_col, :] = _select_dot3(parts, e_ref[s]).reshape(
                    steps, n_col, LANES)
        for s in range(SUBLANES):
            v_scr[:, s, :] = v_ref[:, s * LANES:(s + 1) * LANES]

    n_part = max(1, 4 // v_tiles)
    tiles = [slice(i * SUBLANES, (i + 1) * SUBLANES) for i in range(v_tiles)]
    if delta:
        i_nkk, i_w, i_kka, i_kv, i_r = range(5)
    else:
        i_w, i_kv, i_r = range(3)

    k_chunk = SCAN_KEY_CHUNK if v_tiles > 1 else HEAD_DIM
    zero = jnp.zeros((SUBLANES, LANES), F32)
    acc0 = tuple(zero for _ in range(v_tiles * n_part))

    def over_keys(body):
        def chunk(c, acc):
            acc = list(acc)
            for kk in range(k_chunk):
                acc = body(c * k_chunk + kk, kk, acc)
            return tuple(acc)
        if k_chunk == HEAD_DIM:
            acc = chunk(0, acc0)
        else:
            acc = lax.fori_loop(0, HEAD_DIM // k_chunk, chunk, acc0)
        sums = []
        for i in range(v_tiles):
            parts = list(acc[i * n_part:(i + 1) * n_part])
            while len(parts) > 1:
                parts = [a + b for a, b in zip(parts[0::2], parts[1::2])]
            sums.append(parts[0])
        return sums

    backward = (pl.program_id(0) % 2 == 1) if reverse_odd else None

    def step(i_step, carry):
        t = jnp.where(backward, steps - 1 - i_step, i_step) if reverse_odd else i_step
        if packed:
            row = lambda j, k: x_scr[j, t, pl.ds(k, 1), :]
            vt = [v_scr[t]]
        else:
            row = lambda j, k: rows_ref[j, t, pl.ds(k, 1), :]
            vt = [v_ref[t, tl, :] for tl in tiles]
        if delta:
            def sa_body(k, kk, acc):
                nk = row(i_nkk, k)
                for i, tl in enumerate(tiles):
                    p = i * n_part + kk % n_part
                    acc[p] = acc[p] + s_scr[k, tl, :] * nk
                return acc
            sa = over_keys(sa_body)

        def update_body(k, kk, acc):
            wk, kvk, rk = row(i_w, k), row(i_kv, k), row(i_r, k)
            if delta:
                kkak = row(i_kka, k)
            for i, tl in enumerate(tiles):
                s = s_scr[k, tl, :] * wk + vt[i] * kvk
                if delta:
                    s = s + sa[i] * kkak
                s_scr[k, tl, :] = s
                p = i * n_part + kk % n_part
                acc[p] = acc[p] + s * rk
            return acc
        y = over_keys(update_body)
        if packed:
            y_scr[t] = y[0]
        else:
            for i, tl in enumerate(tiles):
                y_ref[t, tl, :] = y[i]
        return carry

    lax.fori_loop(0, steps, step, 0)
    if packed:
        for s in range(SUBLANES):
            y_ref[:, s * LANES:(s + 1) * LANES] = y_scr[:, s, :]

    @pl.when(tb == pl.num_programs(1) - 1)
    def _():
        sfin_ref[...] = s_scr[...]


def _scan(rows, v, s0, *, delta, steps, reverse_odd=False):
    groups, _, t_len = rows.shape[:3]
    n_rows = 5 if delta else 3
    n_blocks = t_len // steps
    packed = v is not None
    tblk = (lambda g, t: jnp.where(g % 2 == 1, n_blocks - 1 - t, t)) if reverse_odd else (lambda g, t: t)
    vrows = s0.shape[2]
    state_spec = pl.BlockSpec((None, HEAD_DIM, vrows, LANES), lambda g, t: (g, 0, 0, 0))
    scratch = [pltpu.VMEM((HEAD_DIM, vrows, LANES), F32)]
    if packed:
        width = rows.shape[-1]
        n_chain = width // HEAD_DIM
        per_col = LANES // n_chain
        src = np.arange(LANES)
        e = np.stack([(src[:, None] // n_chain == s) & (src[:, None] % n_chain == src[None, :] % n_chain)
                      for s in range(per_col)])
        assert HEAD_DIM // per_col == SUBLANES and vrows == SUBLANES
        io_spec = pl.BlockSpec((None, steps, width), lambda g, t: (g, tblk(g, t), 0))
        in_specs = [pl.BlockSpec((None, n_rows, steps, width), lambda g, t: (g, 0, tblk(g, t), 0)),
                    io_spec, state_spec,
                    pl.BlockSpec((per_col, LANES, LANES), lambda g, t: (0, 0, 0))]
        s0 = s0.reshape(groups, SUBLANES, per_col, vrows, LANES).transpose(0, 2, 1, 3, 4).reshape(s0.shape)
        args = [rows, v, s0, jnp.asarray(e, dtype=BF16)]
        scratch += [pltpu.VMEM((n_rows, steps, HEAD_DIM, LANES), F32)] + [pltpu.VMEM((steps, SUBLANES, LANES), F32)] * 3
        y_spec, y_shape = io_spec, jax.ShapeDtypeStruct((groups, t_len, width), F32)
    else:
        in_specs = [pl.BlockSpec((None, n_rows, steps, HEAD_DIM, LANES), lambda g, t: (g, 0, tblk(g, t), 0, 0)),
                    pl.BlockSpec((None, None, steps, vrows, LANES), lambda g, t: (g, n_rows, tblk(g, t), 0, 0)),
                    state_spec]
        args = [rows, rows, s0]
        y_spec = pl.BlockSpec((None, steps, vrows, LANES), lambda g, t: (g, tblk(g, t), 0, 0))
        y_shape = jax.ShapeDtypeStruct((groups, t_len, vrows, LANES), F32)
    return pl.pallas_call(
        functools.partial(_scan_kernel, n_rows=n_rows, delta=delta, steps=steps, v_tiles=vrows // SUBLANES,
                          packed=packed, reverse_odd=reverse_odd),
        grid=(groups, n_blocks),
        in_specs=in_specs,
        out_specs=[y_spec, state_spec],
        out_shape=[y_shape, jax.ShapeDtypeStruct((groups, HEAD_DIM, vrows, LANES), F32)],
        scratch_shapes=scratch,
        compiler_params=_params("parallel", "arbitrary"),
        name="rwkv_scan" if delta else "hgrn_scan",
    )(*args)


def _to_scan_layout(a, n_seq, t_len, per_dir_groups):
    n_arr = a.shape[1]
    a = a.reshape(2, n_arr, n_seq, t_len, N_HEADS, HEAD_DIM)
    a = jnp.stack([a[0], a[1][:, :, ::-1]])
    if per_dir_groups:
        r = a.transpose(0, 1, 3, 5, 2, 4).reshape(2, n_arr, t_len, HEAD_DIM, n_seq * N_HEADS)
        return r[:, :n_arr - 1], r[:, n_arr - 1]
    n_chain = 2 * n_seq * N_HEADS
    vg = LANES // n_chain
    rows = a[:, :n_arr - 1].transpose(1, 3, 5, 0, 2, 4).reshape(1, n_arr - 1, t_len, HEAD_DIM, n_chain)
    v = a[:, n_arr - 1].reshape(2, n_seq, t_len, N_HEADS, vg, HEAD_DIM // vg)
    v = v.transpose(2, 5, 4, 0, 1, 3).reshape(1, t_len, HEAD_DIM // vg, LANES)
    return rows, v


def _from_scan_layout(y, n_seq, t_len, per_dir_groups):
    if per_dir_groups:
        y = y.reshape(2, t_len, HEAD_DIM, n_seq, N_HEADS)
        y = y[0] + y[1][::-1]
        return y.transpose(2, 0, 3, 1).reshape(n_seq * t_len, MIX_W)
    n_chain = 2 * n_seq * N_HEADS
    vg = LANES // n_chain
    y = y.reshape(t_len, HEAD_DIM // vg, vg, 2, n_seq, N_HEADS)
    y = y[:, :, :, 0] + y[::-1, :, :, 1]
    return y.transpose(3, 0, 4, 2, 1).reshape(n_seq * t_len, MIX_W)


def _state_to_scan(s, key_last):
    n_seq = s.shape[0]
    vg = LANES // (2 * n_seq * N_HEADS)
    if key_last:
        s = s.reshape(n_seq, 2, N_HEADS, vg, HEAD_DIM // vg, HEAD_DIM).transpose(5, 4, 3, 1, 0, 2)
    else:
        s = s.reshape(n_seq, 2, N_HEADS, HEAD_DIM, vg, HEAD_DIM // vg).transpose(3, 5, 4, 1, 0, 2)
    return s.reshape(1, HEAD_DIM, HEAD_DIM // vg, LANES)


def _state_from_scan(s, n_seq, key_last):
    s = s.reshape(2, HEAD_DIM, HEAD_DIM, N_HEADS, n_seq)
    return s.transpose(4, 0, 3, 2, 1) if key_last else s.transpose(4, 0, 3, 1, 2)


def _shifted(x, d, t_len):
    row = lax.broadcasted_iota(jnp.int32, (t_len, 1), 0)
    prev_f = jnp.where(row == 0, 0.0, pltpu.roll(x, 1, 0))
    prev_b = jnp.where(row == t_len - 1, 0.0, pltpu.roll(x, t_len - 1, 0))
    return jnp.where(d == 0, prev_f, prev_b)


def _rwkv_pre_kernel(rkv_ref, lora_ref, mu_rkv_ref, mu_lo_ref, wl_ref, b0_ref, kk_ref, ka_ref, rk_ref,
                     seg_ref, scan_ref, bonus_ref, *, t_len):
    d = pl.program_id(1)
    x = rkv_ref[...]
    x = x + (_shifted(x, d, t_len) - x) * mu_rkv_ref[...]
    r, k, v = x[:, 0:MIX_W], x[:, MIX_W:2 * MIX_W], x[:, 2 * MIX_W:3 * MIX_W]
    lo = lora_ref[...]
    lo = lo + (_shifted(lo, d, t_len) - lo) * mu_lo_ref[...]
    lane = lax.broadcasted_iota(jnp.int32, lo.shape, 1)
    lo = jnp.where(lane < LORA_RANK, jnp.tanh(lo), lo)
    pre = _bdot(lo, wl_ref[...]) + b0_ref[...]
    w = jnp.exp(-RWKV_DECAY_SCALE * _sigmoid(pre[:, 0:MIX_W]))
    a = _sigmoid(pre[:, MIX_W:2 * MIX_W])
    seg = seg_ref[...]
    kk = k * kk_ref[...]
    kk = kk / jnp.maximum(jnp.sqrt(_seg_sum(kk * kk, seg)), 1e-12)
    k = k * (1.0 + (a - 1.0) * ka_ref[...])
    scan_ref[0] = -kk
    scan_ref[1] = w
    scan_ref[2] = kk * a
    scan_ref[3] = k
    scan_ref[4] = r
    scan_ref[5] = v
    bonus_ref[...] = _seg_sum(r * k * rk_ref[...], seg) * v


def _rwkv_pre(p_rw, lw, seg, n_seq, t_len, seq0):
    n = n_seq * t_len
    lora_blk = (3 * MIX_W + GATE_RANK) // (2 * LORA_RANK)
    vec = lambda w: pl.BlockSpec((None, 1, w), lambda i, d: (d, 0, 0))
    shared = lambda w: pl.BlockSpec((1, w), lambda i, d: (0, 0))
    return pl.pallas_call(
        functools.partial(_rwkv_pre_kernel, t_len=t_len),
        grid=(n_seq, 2),
        in_specs=[pl.BlockSpec((t_len, 3 * MIX_W), lambda i, d: (seq0 + i, 0)),
                  pl.BlockSpec((t_len, 2 * LORA_RANK), lambda i, d: (seq0 + i, lora_blk + d)),
                  vec(3 * MIX_W), vec(2 * LORA_RANK),
                  pl.BlockSpec((None, 2 * LORA_RANK, 2 * MIX_W), lambda i, d: (d, 0, 0)),
                  vec(2 * MIX_W), shared(MIX_W), shared(MIX_W), shared(MIX_W),
                  pl.BlockSpec((MIX_W, MIX_W), lambda i, d: (0, 0))],
        out_specs=[pl.BlockSpec((None, 6, t_len, MIX_W), lambda i, d: (d, 0, i, 0)),
                   pl.BlockSpec((None, t_len, MIX_W), lambda i, d: (d, i, 0))],
        out_shape=[jax.ShapeDtypeStruct((2, 6, n, MIX_W), F32),
                   jax.ShapeDtypeStruct((2, n, MIX_W), F32)],
        compiler_params=_params("parallel", "parallel"),
        name="rwkv_pre",
    )(p_rw, p_rw, lw["mu_rkv"], lw["mu_lo"], lw["w_lora"], lw["b_lora"], lw["rw_kk"], lw["rw_ka"],
      lw["rw_rk"], seg)


def _rwkv_post_kernel(y_ref, bonus_ref, gh_ref, g2_ref, lnw_ref, lnb_ref, seg_ref, o_ref):
    seg = seg_ref[...]
    y = y_ref[...]
    mu = _seg_sum(y, seg) * (1.0 / HEAD_DIM)
    yc = y - mu
    var = _seg_sum(yc * yc, seg) * (1.0 / HEAD_DIM)
    y = yc * lax.rsqrt(var + RWKV_LN_EPS) * lnw_ref[...] + lnb_ref[...]
    y = y + bonus_ref[0] + bonus_ref[1]
    o_ref[...] = y * _bdot(_sigmoid(gh_ref[...]), g2_ref[...])


def _rwkv_post(y, bonus, p_rw, lw, seg, tile0):
    n = y.shape[0]
    gh_blk = 3 * MIX_W // GATE_RANK
    shared = lambda w: pl.BlockSpec((1, w), lambda i: (0, 0))
    return pl.pallas_call(
        _rwkv_post_kernel,
        grid=(n // ROW_TILE,),
        in_specs=[pl.BlockSpec((ROW_TILE, MIX_W), lambda i: (i, 0)),
                  pl.BlockSpec((2, ROW_TILE, MIX_W), lambda i: (0, i, 0)),
                  pl.BlockSpec((ROW_TILE, GATE_RANK), lambda i: (tile0 + i, gh_blk)),
                  pl.BlockSpec((GATE_RANK, MIX_W), lambda i: (0, 0)),
                  shared(MIX_W), shared(MIX_W),
                  pl.BlockSpec((MIX_W, MIX_W), lambda i: (0, 0))],
        out_specs=pl.BlockSpec((ROW_TILE, MIX_W), lambda i: (i, 0)),
        out_shape=jax.ShapeDtypeStruct((n, MIX_W), F32),
        compiler_params=_params("parallel"),
        name="rwkv_post",
    )(y, bonus, p_rw, lw["rw_g2"], lw["rw_lnx_w"], lw["rw_lnx_b"], seg)


def _hgrn_pre_kernel(qv_ref, f_ref, lb_ref, scan_ref):
    q = qv_ref[:, 0:MIX_W]
    f_raw = f_ref[...]
    lb = lb_ref[...]
    scan_ref[0] = lb + (1.0 - lb) * _sigmoid(f_raw)
    scan_ref[1] = (1.0 - lb) * _sigmoid(-f_raw)
    scan_ref[2] = q * _sigmoid(q)
    scan_ref[3] = qv_ref[:, MIX_W:2 * MIX_W]


def _hgrn_pre(p_hg, lb, n_tok, tile0):
    return pl.pallas_call(
        _hgrn_pre_kernel,
        grid=(n_tok // ROW_TILE, 2),
        in_specs=[pl.BlockSpec((ROW_TILE, 2 * MIX_W), lambda i, d: (tile0 + i, 0)),
                  pl.BlockSpec((ROW_TILE, MIX_W), lambda i, d: (tile0 + i, 3 + d)),
                  pl.BlockSpec((None, 1, MIX_W), lambda i, d: (d, 0, 0))],
        out_specs=pl.BlockSpec((None, 4, ROW_TILE, MIX_W), lambda i, d: (d, 0, i, 0)),
        out_shape=jax.ShapeDtypeStruct((2, 4, n_tok, MIX_W), F32),
        compiler_params=_params("parallel", "parallel"),
        name="hgrn_pre",
    )(p_hg, p_hg, lb)


def _hgrn_post_kernel(o_ref, g_ref, nw_ref, seg_ref, out_ref):
    o = o_ref[...]
    ms = _seg_sum(o * o, seg_ref[...]) * (1.0 / HEAD_DIM)
    out_ref[...] = o * lax.rsqrt(ms + NORM_EPS) * nw_ref[...] * _sigmoid(g_ref[...])


def _hgrn_post(o, p_hg, hg_norm, seg, tile0):
    n = o.shape[0]
    return pl.pallas_call(
        _hgrn_post_kernel,
        grid=(n // ROW_TILE,),
        in_specs=[pl.BlockSpec((ROW_TILE, MIX_W), lambda i: (i, 0)),
                  pl.BlockSpec((ROW_TILE, MIX_W), lambda i: (tile0 + i, 2)),
                  pl.BlockSpec((1, MIX_W), lambda i: (0, 0)),
                  pl.BlockSpec((MIX_W, MIX_W), lambda i: (0, 0))],
        out_specs=pl.BlockSpec((ROW_TILE, MIX_W), lambda i: (i, 0)),
        out_shape=jax.ShapeDtypeStruct((n, MIX_W), F32),
        compiler_params=_params("parallel"),
        name="hgrn_post",
    )(o, p_hg, hg_norm, seg)


CTX_TILE_STEPS = ROW_TILE // 32


def _fm_to_scan(a, n_seq):
    grp = lax.broadcasted_iota(jnp.int32, (HEAD_DIM, LANES), 1) // n_seq
    rolled = []
    for h in range(N_HEADS):
        blk = a[h * HEAD_DIM:(h + 1) * HEAD_DIM]
        rolled.append([blk if s == 0 else pltpu.roll(blk, s * n_seq, 1) for s in range(N_HEADS)])
    tiles = []
    for t in range(N_HEADS):
        out = rolled[0][(0 - t) % N_HEADS]
        for h in range(1, N_HEADS):
            out = jnp.where(grp == h, rolled[h][(h - t) % N_HEADS], out)
        tiles.append(out)
    return tiles


def _scan_to_fm(tiles, n_seq):
    grp = lax.broadcasted_iota(jnp.int32, (HEAD_DIM, LANES), 1) // n_seq
    rolled = [[tl if s == 0 else pltpu.roll(tl, s * n_seq, 1) for s in range(N_HEADS)] for tl in tiles]
    blocks = []
    for h in range(N_HEADS):
        out = rolled[0][(0 - h) % N_HEADS]
        for t in range(1, N_HEADS):
            out = jnp.where(grp == t, rolled[t][(t - h) % N_HEADS], out)
        blocks.append(out)
    return jnp.concatenate(blocks, axis=0)


def _store_scan_tiles(scan_ref, d, j, a, n_seq):
    for c in range(a.shape[1] // LANES):
        for t, tile in enumerate(_fm_to_scan(a[:, c * LANES:(c + 1) * LANES], n_seq)):
            scan_ref[d, j, c * N_HEADS + t] = tile


def _load_scan_tiles(y_ref, n_seq):
    cols = []
    for c in range(y_ref.shape[1] // N_HEADS):
        tiles = [y_ref[0, c * N_HEADS + t] + y_ref[1, c * N_HEADS + t] for t in range(N_HEADS)]
        cols.append(_scan_to_fm(tiles, n_seq))
    return jnp.concatenate(cols, axis=1)


def _head_sum_rows(x):
    w = x.shape[1]
    s = jnp.sum(x.reshape(N_HEADS, HEAD_DIM, w), axis=1, keepdims=True)
    return jnp.broadcast_to(s, (N_HEADS, HEAD_DIM, w)).reshape(N_HEADS * HEAD_DIM, w)


def _lane_halo(halo, shift, valid, width):
    h = jnp.where(valid, pltpu.roll(halo, shift, 1), 0.0)
    return jnp.concatenate([h] * (width // LANES), axis=1)


def _prev_step(x, halo, valid, n_seq):
    width = x.shape[1]
    lane = lax.broadcasted_iota(jnp.int32, (1, width), 1)
    return jnp.where(lane < n_seq, _lane_halo(halo, n_seq, valid, width), pltpu.roll(x, n_seq, 1))


def _next_step(x, halo, valid, n_seq):
    width = x.shape[1]
    lane = lax.broadcasted_iota(jnp.int32, (1, width), 1)
    return jnp.where(lane >= width - n_seq, _lane_halo(halo, LANES - n_seq, valid, width),
                     pltpu.roll(x, width - n_seq, 1))


def _rwkv_pre_fm_kernel(x_ref, hp_ref, hn_ref, mu_rkv_ref, mu_lo_ref, w2t_ref, a2t_ref, w0_ref, a0_ref,
                        kk_ref, ka_ref, rk_ref, scan_ref, bonus_ref, *, n_seq):
    i = pl.program_id(0)
    last = pl.num_programs(0) - 1
    rkv_rows = slice(0, 3 * MIX_W)
    x = x_ref[rkv_rows, :]
    bonus = None
    for d in range(2):
        if d == 0:
            shifted = lambda a, rows: _prev_step(a, hp_ref[rows, :], i > 0, n_seq)
        else:
            shifted = lambda a, rows: _next_step(a, hn_ref[rows, :], i < last, n_seq)
        x3 = x + (shifted(x, rkv_rows) - x) * mu_rkv_ref[d]
        r, k, v = x3[0:MIX_W], x3[MIX_W:2 * MIX_W], x3[2 * MIX_W:3 * MIX_W]
        lo_rows = slice(3 * MIX_W + GATE_RANK + 2 * LORA_RANK * d, 3 * MIX_W + GATE_RANK + 2 * LORA_RANK * (d + 1))
        lo = x_ref[lo_rows, :]
        lo = lo + (shifted(lo, lo_rows) - lo) * mu_lo_ref[d]
        w_pre = _bdot(w2t_ref[d], jnp.tanh(lo[0:LORA_RANK])) + w0_ref[d]
        a_pre = _bdot(a2t_ref[d], lo[LORA_RANK:2 * LORA_RANK]) + a0_ref[d]
        w = jnp.exp(-RWKV_DECAY_SCALE * _sigmoid(w_pre))
        a = _sigmoid(a_pre)
        kk = k * kk_ref[...]
        kk = kk / jnp.maximum(jnp.sqrt(_head_sum_rows(kk * kk)), 1e-12)
        k = k * (1.0 + (a - 1.0) * ka_ref[...])
        for j, arr in enumerate((-kk, w, kk * a, k, r, v)):
            _store_scan_tiles(scan_ref, d, j, arr, n_seq)
        b = _head_sum_rows(r * k * rk_ref[...]) * v
        bonus = b if bonus is None else bonus + b
    bonus_ref[...] = bonus


def _rwkv_pre_fm(pt_rw, lw, n_seq, t_len):
    n = pt_rw.shape[1]
    n_tiles = n // ROW_TILE
    per_tile = ROW_TILE // LANES
    col = lambda rows: pl.BlockSpec((rows, 1), lambda i: (0, 0))
    dcol = lambda rows: pl.BlockSpec((2, rows, 1), lambda i: (0, 0, 0))
    return pl.pallas_call(
        functools.partial(_rwkv_pre_fm_kernel, n_seq=n_seq),
        grid=(n_tiles,),
        in_specs=[pl.BlockSpec((RW_COLS, ROW_TILE), lambda i: (0, i)),
                  pl.BlockSpec((RW_COLS, LANES), lambda i: (0, jnp.maximum(i * per_tile - 1, 0))),
                  pl.BlockSpec((RW_COLS, LANES),
                               lambda i: (0, jnp.minimum((i + 1) * per_tile, n_tiles * per_tile - 1))),
                  dcol(3 * MIX_W), dcol(2 * LORA_RANK),
                  pl.BlockSpec((2, MIX_W, LORA_RANK), lambda i: (0, 0, 0)),
                  pl.BlockSpec((2, MIX_W, LORA_RANK), lambda i: (0, 0, 0)),
                  dcol(MIX_W), dcol(MIX_W), col(MIX_W), col(MIX_W), col(MIX_W)],
        out_specs=[pl.BlockSpec((2, 6, CTX_TILE_STEPS, HEAD_DIM, LANES), lambda i: (0, 0, i, 0, 0)),
                   pl.BlockSpec((MIX_W, ROW_TILE), lambda i: (0, i))],
        out_shape=[jax.ShapeDtypeStruct((2, 6, t_len, HEAD_DIM, LANES), F32),
                   jax.ShapeDtypeStruct((MIX_W, n), F32)],
        compiler_params=_params("parallel"),
        name="rwkv_pre_fm",
    )(pt_rw, pt_rw, pt_rw, lw["mu_rkv_col"], lw["mu_lo_col"], lw["w2t"], lw["a2t"], lw["w0_col"],
      lw["a0_col"], lw["kk_col"], lw["ka_col"], lw["rk_col"])


def _rwkv_post_fm_kernel(y_ref, bonus_ref, gh_ref, g2t_ref, lnw_ref, lnb_ref, o_ref, *, n_seq):
    y = _load_scan_tiles(y_ref, n_seq)
    mu = _head_sum_rows(y) * (1.0 / HEAD_DIM)
    yc = y - mu
    var = _head_sum_rows(yc * yc) * (1.0 / HEAD_DIM)
    y = yc * lax.rsqrt(var + RWKV_LN_EPS) * lnw_ref[...] + lnb_ref[...] + bonus_ref[...]
    g = _bdot(g2t_ref[...], _sigmoid(gh_ref[...]))
    o_ref[...] = (y * g).T


def _rwkv_post_fm(y, bonus, pt_rw, lw, n_seq):
    n = bonus.shape[1]
    col = pl.BlockSpec((MIX_W, 1), lambda i: (0, 0))
    return pl.pallas_call(
        functools.partial(_rwkv_post_fm_kernel, n_seq=n_seq),
        grid=(n // ROW_TILE,),
        in_specs=[pl.BlockSpec((2, CTX_TILE_STEPS, HEAD_DIM, LANES), lambda i: (0, i, 0, 0)),
                  pl.BlockSpec((MIX_W, ROW_TILE), lambda i: (0, i)),
                  pl.BlockSpec((GATE_RANK, ROW_TILE), lambda i: (3 * MIX_W // GATE_RANK, i)),
                  pl.BlockSpec((MIX_W, GATE_RANK), lambda i: (0, 0)),
                  col, col],
        out_specs=pl.BlockSpec((ROW_TILE, MIX_W), lambda i: (i, 0)),
        out_shape=jax.ShapeDtypeStruct((n, MIX_W), F32),
        compiler_params=_params("parallel"),
        name="rwkv_post_fm",
    )(y, bonus, pt_rw, lw["g2t"], lw["lnw_col"], lw["lnb_col"])


def _hgrn_pre_fm_kernel(qv_ref, ff_ref, fb_ref, lb_ref, scan_ref, *, n_seq):
    q = qv_ref[0:MIX_W, :]
    q = q * _sigmoid(q)
    v = qv_ref[MIX_W:2 * MIX_W, :]
    for d, f_ref in enumerate((ff_ref, fb_ref)):
        f_raw = f_ref[...]
        lb = lb_ref[d]
        f = lb + (1.0 - lb) * _sigmoid(f_raw)
        kv = (1.0 - lb) * _sigmoid(-f_raw)
        for j, arr in enumerate((f, kv, q, v)):
            _store_scan_tiles(scan_ref, d, j, arr, n_seq)


def _hgrn_pre_fm(pt_hg, lb_col, n_seq, t_len):
    n = pt_hg.shape[1]
    return pl.pallas_call(
        functools.partial(_hgrn_pre_fm_kernel, n_seq=n_seq),
        grid=(n // ROW_TILE,),
        in_specs=[pl.BlockSpec((2 * MIX_W, ROW_TILE), lambda i: (0, i)),
                  pl.BlockSpec((MIX_W, ROW_TILE), lambda i: (3, i)),
                  pl.BlockSpec((MIX_W, ROW_TILE), lambda i: (4, i)),
                  pl.BlockSpec((2, MIX_W, 1), lambda i: (0, 0, 0))],
        out_specs=pl.BlockSpec((2, 4, CTX_TILE_STEPS, HEAD_DIM, LANES), lambda i: (0, 0, i, 0, 0)),
        out_shape=jax.ShapeDtypeStruct((2, 4, t_len, HEAD_DIM, LANES), F32),
        compiler_params=_params("parallel"),
        name="hgrn_pre_fm",
    )(pt_hg, pt_hg, pt_hg, lb_col)


def _hgrn_post_fm_kernel(o_ref, g_ref, nw_ref, out_ref, *, n_seq):
    o = _load_scan_tiles(o_ref, n_seq)
    ms = _head_sum_rows(o * o) * (1.0 / HEAD_DIM)
    out_ref[...] = (o * lax.rsqrt(ms + NORM_EPS) * nw_ref[...] * _sigmoid(g_ref[...])).T


def _hgrn_post_fm(o, pt_hg, nw_col, n_seq):
    n = pt_hg.shape[1]
    return pl.pallas_call(
        functools.partial(_hgrn_post_fm_kernel, n_seq=n_seq),
        grid=(n // ROW_TILE,),
        in_specs=[pl.BlockSpec((2, CTX_TILE_STEPS, HEAD_DIM, LANES), lambda i: (0, i, 0, 0)),
                  pl.BlockSpec((MIX_W, ROW_TILE), lambda i: (2, i)),
                  pl.BlockSpec((MIX_W, 1), lambda i: (0, 0))],
        out_specs=pl.BlockSpec((ROW_TILE, MIX_W), lambda i: (i, 0)),
        out_shape=jax.ShapeDtypeStruct((n, MIX_W), F32),
        compiler_params=_params("parallel"),
        name="hgrn_post_fm",
    )(o, pt_hg, nw_col)


def _ctx_recurrent_mixers(pt_rw, pt_hg, lw, n_seq, t_len):
    assert n_seq * N_HEADS == LANES
    s0 = jnp.zeros((2, HEAD_DIM, HEAD_DIM, LANES), F32)
    scan_in, bonus = _rwkv_pre_fm(pt_rw, lw, n_seq, t_len)
    y, s_rw = _scan(scan_in, None, s0, delta=True, steps=32, reverse_odd=True)
    ya = _rwkv_post_fm(y, bonus, pt_rw, lw, n_seq)
    scan_in = _hgrn_pre_fm(pt_hg, lw["hg_lb_col"], n_seq, t_len)
    o, s_hg = _scan(scan_in, None, s0, delta=False, steps=32, reverse_odd=True)
    yc = _hgrn_post_fm(o, pt_hg, lw["hg_norm_col"], n_seq)
    return ya, yc, s_rw, s_hg


def _head(x, h):
    return x[:, h * HEAD_DIM:(h + 1) * HEAD_DIM]


def _softmax_av(scores, values, sink=None):
    m = None
    for s in scores:
        mi = jnp.max(s, axis=-1, keepdims=True)
        m = mi if m is None else jnp.maximum(m, mi)
    if sink is not None:
        m = jnp.maximum(m, sink)
    den = None if sink is None else jnp.exp(sink - m)
    out = None
    for s, v in zip(scores, values):
        e = jnp.exp(s - m)
        di = jnp.sum(e, axis=-1, keepdims=True)
        den = di if den is None else den + di
        oi = _bdot(e, v)
        out = oi if out is None else out + oi
    return out / den


def _ctx_attn_kernel(sink_ref, p_ref, y_ref, cache_ref, *, kv_heads, use_sink):
    group = N_HEADS // kv_heads
    kv_w = kv_heads * HEAD_DIM
    q_all = p_ref[:, 0:MIX_W] * ATTN_SCALE
    k_all = p_ref[:, MIX_W:MIX_W + kv_w]
    v_all = p_ref[:, MIX_W + kv_w:MIX_W + 2 * kv_w]
    cache_ref[0] = k_all
    cache_ref[1] = v_all
    outs = []
    for h in range(N_HEADS):
        kvh = h // group
        outs.append(_softmax_av([_bdot_nt(_head(q_all, h), _head(k_all, kvh))], [_head(v_all, kvh)],
                                sink=sink_ref[h] if use_sink else None))
    y_ref[...] = jnp.concatenate(outs, axis=-1)


def _ctx_attention(p, sink, kv_heads, use_sink):
    n_seq, t_len, cols = p.shape
    kv_w = kv_heads * HEAD_DIM
    return pl.pallas_call(
        functools.partial(_ctx_attn_kernel, kv_heads=kv_heads, use_sink=use_sink),
        grid=(n_seq,),
        in_specs=[pl.BlockSpec(memory_space=pltpu.SMEM),
                  pl.BlockSpec((None, t_len, cols), lambda i: (i, 0, 0))],
        out_specs=[pl.BlockSpec((None, t_len, MIX_W), lambda i: (i, 0, 0)),
                   pl.BlockSpec((None, 2, t_len, kv_w), lambda i: (i, 0, 0, 0))],
        out_shape=[jax.ShapeDtypeStruct((n_seq, t_len, MIX_W), F32),
                   jax.ShapeDtypeStruct((n_seq, 2, t_len, kv_w), F32)],
        compiler_params=_params("parallel"),
        name="ctx_attention",
    )(sink, p)


def _nat_latent_kernel(p_ref, ckv_ref, bias_ref, o_ref, *, rows):
    kh = min(NAT_KH, rows)
    n_loc = kh * GRID_W
    ck_all = ckv_ref[0]
    cv_all = ckv_ref[1]
    for r in range(rows):
        rs = min(max(r - kh // 2, 0), rows - kh)
        roff0 = rs - r + NAT_KH - 1
        q_all = p_ref[r * GRID_W:(r + 1) * GRID_W, 0:MIX_W] * ATTN_SCALE
        k_all = p_ref[rs * GRID_W:rs * GRID_W + n_loc, MIX_W:2 * MIX_W]
        v_all = p_ref[rs * GRID_W:rs * GRID_W + n_loc, 2 * MIX_W:3 * MIX_W]
        outs = []
        for h in range(N_HEADS):
            q = _head(q_all, h)
            bias = jnp.concatenate([bias_ref[h, roff0 + 2 * j] for j in range(kh // 2)], axis=-1)
            s_loc = _bdot_nt(q, _head(k_all, h)) + bias
            outs.append(_softmax_av([s_loc, _bdot_nt(q, _head(ck_all, h))],
                                    [_head(v_all, h), _head(cv_all, h)]))
        o_ref[r * GRID_W:(r + 1) * GRID_W, :] = jnp.concatenate(outs, axis=-1)


def _nat_bias_table(rpb):
    q = np.arange(GRID_W)[:, None]
    kc = np.arange(GRID_W)[None, :]
    win = np.clip(q - NAT_KW // 2, 0, GRID_W - NAT_KW)
    ok = (kc >= win) & (kc < win + NAT_KW)
    coff = np.clip(kc - q, -(NAT_KW - 1), NAT_KW - 1) + NAT_KW - 1
    t = jnp.where(ok[None, None], rpb[:, :, coff], MASK_VALUE)
    return jnp.concatenate([t[:, :-1], t[:, 1:]], axis=-1)


def _nat_latent(p_nat, cache, bias, layer, n_seq, t_len, tile0):
    rows = t_len // GRID_W
    return pl.pallas_call(
        functools.partial(_nat_latent_kernel, rows=rows),
        grid=(n_seq,),
        in_specs=[pl.BlockSpec((t_len, NAT_COLS), lambda i: (tile0 + i, 0)),
                  pl.BlockSpec((None, None, 2, cache.shape[3], MIX_W), lambda i: (i, layer, 0, 0, 0)),
                  pl.BlockSpec(bias.shape, lambda i: (0, 0, 0, 0))],
        out_specs=pl.BlockSpec((t_len, MIX_W), lambda i: (i, 0)),
        out_shape=jax.ShapeDtypeStruct((n_seq * t_len, MIX_W), F32),
        compiler_params=_params("parallel"),
        name="nat_latent",
    )(p_nat, cache, bias)


def _rope(x, cos, sin_signed):
    width = x.shape[-1]
    lane = lax.broadcasted_iota(jnp.int32, x.shape, 1)
    half = HEAD_DIM // 4
    partner = jnp.where(lane % (2 * half) < half, pltpu.roll(x, width - half, 1), pltpu.roll(x, half, 1))
    return x * cos + partner * sin_signed


def _swa_latent_kernel(sink_ref, p_ref, ckv_ref, cos_ref, sin_ref, o_ref, *, t_len):
    hd = HEAD_DIM
    kv_w = SWA_KV_HEADS * hd
    group = N_HEADS // SWA_KV_HEADS
    blk = SWA_BLOCK
    nb = t_len // blk
    q = _rope(p_ref[:, 0:MIX_W], cos_ref[...], sin_ref[...]) * ATTN_SCALE
    k = _rope(p_ref[:, MIX_W:MIX_W + kv_w], cos_ref[:, 0:kv_w], sin_ref[:, 0:kv_w])
    v = p_ref[:, MIX_W + kv_w:MIX_W + 2 * kv_w]
    row = lax.broadcasted_iota(jnp.int32, (group * blk, 1), 0)
    ck_all = ckv_ref[0]
    cv_all = ckv_ref[1]
    for n in range(nb):
        lo = max(n - 1, 0) * blk
        hi = min(n + 2, nb) * blk
        qpos = n * blk + row % blk
        kpos = lo + lax.broadcasted_iota(jnp.int32, (group * blk, hi - lo), 1)
        ok = jnp.abs(kpos - qpos) <= SWA_WINDOW
        heads = [None] * N_HEADS
        for kvh in range(SWA_KV_HEADS):
            qs = jnp.concatenate([_head(q[n * blk:(n + 1) * blk], kvh * group + g) for g in range(group)],
                                 axis=0)
            sink = jnp.concatenate([jnp.full((blk, 1), sink_ref[kvh * group + g], F32)
                                    for g in range(group)], axis=0)
            s_loc = jnp.where(ok, _bdot_nt(qs, _head(k[lo:hi], kvh)), MASK_VALUE)
            s_ctx = _bdot_nt(qs, _head(ck_all, kvh))
            o = _softmax_av([s_loc, s_ctx], [_head(v[lo:hi], kvh), _head(cv_all, kvh)], sink=sink)
            for g in range(group):
                heads[kvh * group + g] = o[g * blk:(g + 1) * blk]
        o_ref[n * blk:(n + 1) * blk, :] = jnp.concatenate(heads, axis=-1)


def _rope_tables(t_len):
    t = jnp.arange(t_len)
    half = HEAD_DIM // 2
    nf = half // 2
    inv = 1.0 / (ROPE_BASE ** (jnp.arange(nf, dtype=F32) / nf))
    cols = []
    for pos in ((t // GRID_W).astype(F32), (t % GRID_W).astype(F32)):
        ang = pos[:, None] * inv[None, :]
        cols.append((jnp.cos(ang), jnp.sin(ang)))
    cos = jnp.concatenate([cols[0][0], cols[0][0], cols[1][0], cols[1][0]], axis=-1)
    sin = jnp.concatenate([-cols[0][1], cols[0][1], -cols[1][1], cols[1][1]], axis=-1)
    return jnp.tile(cos, (1, N_HEADS)), jnp.tile(sin, (1, N_HEADS))


def _swa_latent(p_swa, cache, sink, rope, layer, n_seq, t_len, tile0):
    kv_w = SWA_KV_HEADS * HEAD_DIM
    return pl.pallas_call(
        functools.partial(_swa_latent_kernel, t_len=t_len),
        grid=(n_seq,),
        in_specs=[pl.BlockSpec(memory_space=pltpu.SMEM),
                  pl.BlockSpec((t_len, SWA_COLS), lambda i: (tile0 + i, 0)),
                  pl.BlockSpec((None, None, 2, cache.shape[3], kv_w), lambda i: (i, layer, 0, 0, 0)),
                  pl.BlockSpec((t_len, MIX_W), lambda i: (0, 0)),
                  pl.BlockSpec((t_len, MIX_W), lambda i: (0, 0))],
        out_specs=pl.BlockSpec((t_len, MIX_W), lambda i: (i, 0)),
        out_shape=jax.ShapeDtypeStruct((n_seq * t_len, MIX_W), F32),
        compiler_params=_params("parallel"),
        name="swa_latent",
    )(sink, p_swa, cache, rope[0], rope[1])


LAT_TILE = 256


def _exact_row_select(sel, x):
    hi = x.astype(BF16)
    r1 = x - hi.astype(F32)
    mid = r1.astype(BF16)
    lo = (r1 - mid.astype(F32)).astype(BF16)
    dot = lambda a: jnp.dot(sel, a, preferred_element_type=F32)
    return dot(hi) + dot(mid) + dot(lo)


def _latent_perms(n_seq):
    n_chain = 2 * n_seq * N_HEADS
    vs_n = HEAD_DIM // (LANES // n_chain)
    rows = np.zeros((2, n_seq, MIX_W, HEAD_DIM * n_chain), np.float32)
    vals = np.zeros((2, n_seq, MIX_W, HEAD_DIM * n_chain), np.float32)
    for d in range(2):
        for b in range(n_seq):
            for h in range(N_HEADS):
                c = (d * n_seq + b) * N_HEADS + h
                for e in range(HEAD_DIM):
                    rows[d, b, h * HEAD_DIM + e, e * n_chain + c] = 1.0
                    vals[d, b, h * HEAD_DIM + e, (e % vs_n) * LANES + (e // vs_n) * n_chain + c] = 1.0
    rev = np.eye(LAT_TILE, dtype=np.float32)[::-1]
    return (jnp.asarray(rows, dtype=BF16), jnp.asarray(vals, dtype=BF16),
            jnp.asarray(vals.transpose(0, 1, 3, 2), dtype=BF16), jnp.asarray(rev, dtype=BF16))


def _scan_order_tile(pf_ref, pb_ref, hf_ref, hb_ref, rev_ref, d, b):
    first = pl.program_id(0) == 0
    if d == 0:
        x, halo = pf_ref[b], hf_ref[b, SUBLANES - 1:SUBLANES, :]
    else:
        x, halo = _exact_row_select(rev_ref[...], pb_ref[b]), hb_ref[b, 0:1, :]
    return x, jnp.where(first, 0.0, halo)


def _lat_rwkv_pre_kernel(pf_ref, pb_ref, hf_ref, hb_ref, rev_ref, prow_ref, pval_ref, mu_rkv_ref, mu_lo_ref,
                         wl_ref, b0_ref, kk_ref, ka_ref, rk_ref, seg_ref, z_ref, v_ref, bn_ref, *, n_seq):
    seg = seg_ref[...]
    row0 = lax.broadcasted_iota(jnp.int32, (LAT_TILE, 1), 0) == 0
    acc = [None] * 7
    for d in range(2):
        for b in range(n_seq):
            x, halo = _scan_order_tile(pf_ref, pb_ref, hf_ref, hb_ref, rev_ref, d, b)
            prev = jnp.where(row0, halo, pltpu.roll(x, 1, 0))
            rkv, rkv_prev = x[:, 0:3 * MIX_W], prev[:, 0:3 * MIX_W]
            rkv = rkv + (rkv_prev - rkv) * mu_rkv_ref[d]
            r, k, v = rkv[:, 0:MIX_W], rkv[:, MIX_W:2 * MIX_W], rkv[:, 2 * MIX_W:3 * MIX_W]
            c0 = 3 * MIX_W + GATE_RANK + 2 * LORA_RANK * d
            lo, lo_prev = x[:, c0:c0 + 2 * LORA_RANK], prev[:, c0:c0 + 2 * LORA_RANK]
            lo = lo + (lo_prev - lo) * mu_lo_ref[d]
            lane = lax.broadcasted_iota(jnp.int32, lo.shape, 1)
            lo = jnp.where(lane < LORA_RANK, jnp.tanh(lo), lo)
            pre = _bdot(lo, wl_ref[d]) + b0_ref[d]
            w = jnp.exp(-RWKV_DECAY_SCALE * _sigmoid(pre[:, 0:MIX_W]))
            a = _sigmoid(pre[:, MIX_W:2 * MIX_W])
            kk = k * kk_ref[...]
            kk = kk / jnp.maximum(jnp.sqrt(_seg_sum(kk * kk, seg)), 1e-12)
            k = k * (1.0 + (a - 1.0) * ka_ref[...])
            bonus = _seg_sum(r * k * rk_ref[...], seg) * v
            placed = [_exact_select_dot(arr, prow_ref[d, b]) for arr in (-kk, w, kk * a, k, r)]
            placed += [_exact_select_dot(arr, pval_ref[d, b]) for arr in (v, bonus)]
            acc = [p if s is None else s + p for s, p in zip(acc, placed)]
    for j in range(5):
        z_ref[j] = acc[j]
    v_ref[...] = acc[5]
    bn_ref[...] = acc[6]


def _lat_tile_specs(n_seq, t_len, width):
    n_tiles = t_len // LAT_TILE
    per = LAT_TILE // SUBLANES
    return [pl.BlockSpec((n_seq, LAT_TILE, width), lambda i: (0, i, 0)),
            pl.BlockSpec((n_seq, LAT_TILE, width), lambda i: (0, n_tiles - 1 - i, 0)),
            pl.BlockSpec((n_seq, SUBLANES, width), lambda i: (0, jnp.maximum(i * per - 1, 0), 0)),
            pl.BlockSpec((n_seq, SUBLANES, width),
                         lambda i: (0, jnp.minimum((n_tiles - i) * per, n_tiles * per - 1), 0))]


def _lat_rwkv_pre(p_rw, lw, seg, perms, n_seq, t_len):
    prow, pval, _, rev = perms
    width = prow.shape[-1]
    p3 = p_rw.reshape(n_seq, t_len, RW_COLS)
    whole = lambda a: pl.BlockSpec(a.shape, lambda i: (0,) * a.ndim)
    params = [lw["mu_rkv"], lw["mu_lo"], lw["w_lora"], lw["b_lora"], lw["rw_kk"], lw["rw_ka"], lw["rw_rk"], seg]
    return pl.pallas_call(
        functools.partial(_lat_rwkv_pre_kernel, n_seq=n_seq),
        grid=(t_len // LAT_TILE,),
        in_specs=_lat_tile_specs(n_seq, t_len, RW_COLS) + [whole(a) for a in (rev, prow, pval, *params)],
        out_specs=[pl.BlockSpec((5, LAT_TILE, width), lambda i: (0, i, 0)),
                   pl.BlockSpec((LAT_TILE, width), lambda i: (i, 0)),
                   pl.BlockSpec((LAT_TILE, width), lambda i: (i, 0))],
        out_shape=[jax.ShapeDtypeStruct((5, t_len, width), F32),
                   jax.ShapeDtypeStruct((t_len, width), F32),
                   jax.ShapeDtypeStruct((t_len, width), F32)],
        compiler_params=_params("parallel"),
        name="lat_rwkv_pre",
    )(p3, p3, p3, p3, rev, prow, pval, *params)


def _unplace(yf_ref, yb_ref, rev_ref, pinv_ref, b):
    fwd = _exact_select_dot(yf_ref[...], pinv_ref[0, b])
    bwd = _exact_select_dot(yb_ref[...], pinv_ref[1, b])
    return fwd + _exact_row_select(rev_ref[...], bwd)


def _lat_rwkv_post_kernel(yf_ref, yb_ref, bf_ref, bb_ref, rev_ref, pinv_ref, gh_ref, g2_ref, lnw_ref, lnb_ref,
                          seg_ref, o_ref):
    b = pl.program_id(0)
    seg = seg_ref[...]
    y = _unplace(yf_ref, yb_ref, rev_ref, pinv_ref, b)
    mu = _seg_sum(y, seg) * (1.0 / HEAD_DIM)
    yc = y - mu
    var = _seg_sum(yc * yc, seg) * (1.0 / HEAD_DIM)
    y = yc * lax.rsqrt(var + RWKV_LN_EPS) * lnw_ref[...] + lnb_ref[...]
    y = y + _unplace(bf_ref, bb_ref, rev_ref, pinv_ref, b)
    o_ref[...] = y * _bdot(_sigmoid(gh_ref[...]), g2_ref[...])


def _lat_post_specs(t_len, width):
    n_tiles = t_len // LAT_TILE
    return [pl.BlockSpec((LAT_TILE, width), lambda b, i: (i, 0)),
            pl.BlockSpec((LAT_TILE, width), lambda b, i: (n_tiles - 1 - i, 0))]


def _lat_rwkv_post(y, bonus, p_rw, lw, seg, perms, n_seq, t_len):
    _, _, pinv, rev = perms
    width = y.shape[-1]
    n_tiles = t_len // LAT_TILE
    whole = lambda a: pl.BlockSpec(a.shape, lambda b, i: (0,) * a.ndim)
    params = [lw["rw_g2"], lw["rw_lnx_w"], lw["rw_lnx_b"], seg]
    return pl.pallas_call(
        _lat_rwkv_post_kernel,
        grid=(n_seq, n_tiles),
        in_specs=_lat_post_specs(t_len, width) * 2 + [whole(rev), whole(pinv),
                 pl.BlockSpec((LAT_TILE, GATE_RANK), lambda b, i: (b * n_tiles + i, 3 * MIX_W // GATE_RANK))]
        + [whole(a) for a in params],
        out_specs=pl.BlockSpec((LAT_TILE, MIX_W), lambda b, i: (b * n_tiles + i, 0)),
        out_shape=jax.ShapeDtypeStruct((n_seq * t_len, MIX_W), F32),
        compiler_params=_params("parallel", "parallel"),
        name="lat_rwkv_post",
    )(y, y, bonus, bonus, rev, pinv, p_rw, *params)


def _lat_hgrn_pre_kernel(pf_ref, pb_ref, rev_ref, prow_ref, pval_ref, lb_ref, z_ref, v_ref, *, n_seq):
    acc = [None] * 4
    for d in range(2):
        for b in range(n_seq):
            x = pf_ref[b] if d == 0 else _exact_row_select(rev_ref[...], pb_ref[b])
            q = x[:, 0:MIX_W]
            f_raw = x[:, (3 + d) * MIX_W:(4 + d) * MIX_W]
            lb = lb_ref[d]
            f = lb + (1.0 - lb) * _sigmoid(f_raw)
            kv = (1.0 - lb) * _sigmoid(-f_raw)
            placed = [_exact_select_dot(arr, prow_ref[d, b]) for arr in (f, kv, q * _sigmoid(q))]
            placed.append(_exact_select_dot(x[:, MIX_W:2 * MIX_W], pval_ref[d, b]))
            acc = [p if s is None else s + p for s, p in zip(acc, placed)]
    for j in range(3):
        z_ref[j] = acc[j]
    v_ref[...] = acc[3]


def _lat_hgrn_pre(p_hg, lb, perms, n_seq, t_len):
    prow, pval, _, rev = perms
    width = prow.shape[-1]
    p3 = p_hg.reshape(n_seq, t_len, HG_COLS)
    whole = lambda a: pl.BlockSpec(a.shape, lambda i: (0,) * a.ndim)
    return pl.pallas_call(
        functools.partial(_lat_hgrn_pre_kernel, n_seq=n_seq),
        grid=(t_len // LAT_TILE,),
        in_specs=_lat_tile_specs(n_seq, t_len, HG_COLS)[:2] + [whole(a) for a in (rev, prow, pval, lb)],
        out_specs=[pl.BlockSpec((3, LAT_TILE, width), lambda i: (0, i, 0)),
                   pl.BlockSpec((LAT_TILE, width), lambda i: (i, 0))],
        out_shape=[jax.ShapeDtypeStruct((3, t_len, width), F32),
                   jax.ShapeDtypeStruct((t_len, width), F32)],
        compiler_params=_params("parallel"),
        name="lat_hgrn_pre",
    )(p3, p3, rev, prow, pval, lb)


def _lat_hgrn_post_kernel(of_ref, ob_ref, rev_ref, pinv_ref, g_ref, nw_ref, seg_ref, out_ref):
    o = _unplace(of_ref, ob_ref, rev_ref, pinv_ref, pl.program_id(0))
    ms = _seg_sum(o * o, seg_ref[...]) * (1.0 / HEAD_DIM)
    out_ref[...] = o * lax.rsqrt(ms + NORM_EPS) * nw_ref[...] * _sigmoid(g_ref[...])


def _lat_hgrn_post(o, p_hg, hg_norm, seg, perms, n_seq, t_len):
    _, _, pinv, rev = perms
    width = o.shape[-1]
    n_tiles = t_len // LAT_TILE
    whole = lambda a: pl.BlockSpec(a.shape, lambda b, i: (0,) * a.ndim)
    return pl.pallas_call(
        _lat_hgrn_post_kernel,
        grid=(n_seq, n_tiles),
        in_specs=_lat_post_specs(t_len, width) + [whole(rev), whole(pinv),
                 pl.BlockSpec((LAT_TILE, MIX_W), lambda b, i: (b * n_tiles + i, 2)),
                 whole(hg_norm), whole(seg)],
        out_specs=pl.BlockSpec((LAT_TILE, MIX_W), lambda b, i: (b * n_tiles + i, 0)),
        out_shape=jax.ShapeDtypeStruct((n_seq * t_len, MIX_W), F32),
        compiler_params=_params("parallel", "parallel"),
        name="lat_hgrn_post",
    )(o, o, rev, pinv, p_hg, hg_norm, seg)


def _lat_recurrent_mixers(p_rw, p_hg, lw, seg, perms, n_seq, t_len, state_rw, state_hg):
    z, v, bonus = _lat_rwkv_pre(p_rw, lw, seg, perms, n_seq, t_len)
    y, _ = _scan(z[None], v[None], _state_to_scan(state_rw, True), delta=True, steps=32)
    ya = _lat_rwkv_post(y.reshape(v.shape), bonus, p_rw, lw, seg, perms, n_seq, t_len)
    z, v = _lat_hgrn_pre(p_hg, lw["hg_lb"], perms, n_seq, t_len)
    o, _ = _scan(z[None], v[None], _state_to_scan(state_hg, False), delta=False, steps=32)
    yc = _lat_hgrn_post(o.reshape(v.shape), p_hg, lw["hg_norm"], seg, perms, n_seq, t_len)
    return ya, yc


def _recurrent_mixers(p_rw, p_hg, lw, seg, n_seq, t_len, tile0, states):
    ctx = states is None
    seq0 = tile0 * ROW_TILE // t_len
    n_tok = n_seq * t_len
    steps = 32

    scan_in, bonus = _rwkv_pre(p_rw, lw, seg, n_seq, t_len, seq0)
    rows, v = _to_scan_layout(scan_in, n_seq, t_len, ctx)
    s0 = jnp.zeros((2, HEAD_DIM, HEAD_DIM, LANES), F32) if ctx else _state_to_scan(states[0], True)
    y, s_rw = _scan(rows, v, s0, delta=True, steps=steps)
    ya = _rwkv_post(_from_scan_layout(y, n_seq, t_len, ctx), bonus, p_rw, lw, seg, tile0)

    scan_in = _hgrn_pre(p_hg, lw["hg_lb"], n_tok, tile0)
    rows, v = _to_scan_layout(scan_in, n_seq, t_len, ctx)
    s0 = jnp.zeros((2, HEAD_DIM, HEAD_DIM, LANES), F32) if ctx else _state_to_scan(states[1], False)
    o, s_hg = _scan(rows, v, s0, delta=False, steps=steps)
    yc = _hgrn_post(_from_scan_layout(o, n_seq, t_len, ctx), p_hg, lw["hg_norm"], seg, tile0)
    return ya, yc, s_rw, s_hg


def kernel(x_prompt, x_sample, cache_nat_kv, cache_swa_kv, state_rwkv, state_hgrn, c, c_ctx, norm_g, mod_w, mod_b, w_in, w_out, rw_mu_rkv, rw_mu_lora, rw_w0, rw_w2, rw_a0, rw_a2, rw_g2, rw_kk, rw_ka, rw_rk, rw_lnx_w, rw_lnx_b, nat_rpb, hg_lb_logits, hg_norm, swa_sink, ffn_w1, ffn_w2):
    n_ctx, t_ctx, _ = x_prompt.shape
    n_lat, t_lat, _ = x_sample.shape
    past = cache_nat_kv.shape[3]

    cond = jnp.zeros((MOD_ROWS, D_MODEL), F32).at[0].set(c_ctx).at[1:1 + n_lat].set(c)
    mod = _modulation_all(cond, mod_w, mod_b)

    w_in_bf, w_out_bf = w_in.astype(BF16), w_out.astype(BF16)
    w1_bf, w2_bf = ffn_w1.astype(BF16), ffn_w2.astype(BF16)
    cuts = np.cumsum((0,) + SLABS)
    lat_slabs = tuple(w_in_bf[:, :, cuts[j]:cuts[j + 1]] for j in range(4))
    ctx_modes = ("fm", "seq", "fm", "seq")
    ctx_y_modes = ("tok", "seq", "tok", "seq")
    lat_modes = ("tok",) * 4
    ctx_slabs = tuple(w.transpose(0, 2, 1) if m == "fm" else w for w, m in zip(lat_slabs, ctx_modes))

    lb_sm = jax.nn.softmax(hg_lb_logits.astype(F32), axis=1)
    hg_lb = jnp.cumsum(lb_sm, axis=1) - lb_sm[:, :1]

    head_of = np.arange(MIX_W) // HEAD_DIM
    seg = jnp.asarray((head_of[:, None] == head_of[None, :]).astype(np.float32))
    rope = _rope_tables(t_lat)
    lat_perms = _latent_perms(n_lat)
    zeros_lora = jnp.zeros((LORA_RANK, MIX_W), F32)
    cache_nat = cache_nat_kv.reshape(n_lat, DEPTH, 2, past, MIX_W)
    cache_swa = cache_swa_kv.reshape(n_lat, DEPTH, 2, past, SWA_KV_HEADS * HEAD_DIM)

    xp = x_prompt.transpose(1, 0, 2).reshape(t_ctx * n_ctx, D_MODEL)
    xs = x_sample.reshape(n_lat * t_lat, D_MODEL)
    ctx_row, lat_row = _mod_row(t_ctx, 0), _mod_row(t_lat, 1)
    nat_out, swa_out, rw_out, hg_out = [], [], [], []
    for l in range(DEPTH):
        lw = {
            "mu_rkv": rw_mu_rkv[l].reshape(2, 1, 3 * MIX_W),
            "mu_lo": rw_mu_lora[l].reshape(2, 1, 2 * LORA_RANK),
            "w_lora": jnp.stack([jnp.concatenate(
                [jnp.concatenate([rw_w2[l, d], zeros_lora], axis=1),
                 jnp.concatenate([zeros_lora, rw_a2[l, d]], axis=1)], axis=0) for d in range(2)]),
            "b_lora": jnp.concatenate([rw_w0[l], rw_a0[l]], axis=-1).reshape(2, 1, 2 * MIX_W),
            "rw_kk": rw_kk[l].reshape(1, MIX_W), "rw_ka": rw_ka[l].reshape(1, MIX_W),
            "rw_rk": rw_rk[l].reshape(1, MIX_W), "rw_g2": rw_g2[l],
            "rw_lnx_w": rw_lnx_w[l].reshape(1, MIX_W), "rw_lnx_b": rw_lnx_b[l].reshape(1, MIX_W),
            "hg_lb": hg_lb[:, l].reshape(2, 1, MIX_W), "hg_norm": hg_norm[l].reshape(1, MIX_W),
            "mu_rkv_col": rw_mu_rkv[l].reshape(2, 3 * MIX_W, 1), "mu_lo_col": rw_mu_lora[l].reshape(2, 2 * LORA_RANK, 1),
            "w2t": rw_w2[l].transpose(0, 2, 1), "a2t": rw_a2[l].transpose(0, 2, 1),
            "w0_col": rw_w0[l].reshape(2, MIX_W, 1), "a0_col": rw_a0[l].reshape(2, MIX_W, 1),
            "kk_col": rw_kk[l].reshape(MIX_W, 1), "ka_col": rw_ka[l].reshape(MIX_W, 1),
            "rk_col": rw_rk[l].reshape(MIX_W, 1), "g2t": rw_g2[l].T,
            "lnw_col": rw_lnx_w[l].reshape(MIX_W, 1), "lnb_col": rw_lnx_b[l].reshape(MIX_W, 1),
            "hg_lb_col": hg_lb[:, l].reshape(2, MIX_W, 1), "hg_norm_col": hg_norm[l].reshape(MIX_W, 1),
        }
        sink = swa_sink[l]

        pt_rw, p_nat, pt_hg, p_swa = _in_projection(xp, norm_g, mod, ctx_slabs, l, ctx_row, ctx_modes, n_ctx)
        ya, yc, s_rw, s_hg = _ctx_recurrent_mixers(pt_rw, pt_hg, lw, n_ctx, t_ctx)
        yb, cnat = _ctx_attention(p_nat, sink, N_HEADS, False)
        yd, cswa = _ctx_attention(p_swa, sink, SWA_KV_HEADS, True)
        xp = _out_projection_ffn(xp, (ya, yb, yc, yd), norm_g, mod, w_out_bf, w1_bf, w2_bf, l, ctx_row,
                                 ctx_y_modes, n_ctx)
        nat_out.append(cnat.reshape(n_ctx, 2, t_ctx, N_HEADS, HEAD_DIM))
        swa_out.append(cswa.reshape(n_ctx, 2, t_ctx, SWA_KV_HEADS, HEAD_DIM))
        rw_out.append(_state_from_scan(s_rw, n_ctx, True))
        hg_out.append(_state_from_scan(s_hg, n_ctx, False))

        p_rw, p_nat, p_hg, p_swa = _in_projection(xs, norm_g, mod, lat_slabs, l, lat_row, lat_modes)
        ya, yc = _lat_recurrent_mixers(p_rw, p_hg, lw, seg, lat_perms, n_lat, t_lat,
                                       state_rwkv[:, l], state_hgrn[:, l])
        yb = _nat_latent(p_nat, cache_nat, _nat_bias_table(nat_rpb[l]), l, n_lat, t_lat, 0)
        yd = _swa_latent(p_swa, cache_swa, sink, rope, l, n_lat, t_lat, 0)
        xs = _out_projection_ffn(xs, (ya, yb, yc, yd), norm_g, mod, w_out_bf, w1_bf, w2_bf, l, lat_row,
                                 lat_modes)

    return (xp.reshape(t_ctx, n_ctx, D_MODEL).transpose(1, 0, 2), xs.reshape(x_sample.shape),
            jnp.stack(nat_out, axis=1), jnp.stack(swa_out, axis=1),
            jnp.stack(rw_out, axis=1), jnp.stack(hg_out, axis=1))
```

```python
import functools

import numpy as np
import jax
import jax.numpy as jnp
from jax import lax
from jax.experimental import pallas as pl
from jax.experimental.pallas import tpu as pltpu

F32 = jnp.float32
BF16 = jnp.bfloat16
HIGHEST = lax.Precision.HIGHEST

D_MODEL = 1024
DEPTH = 4
GRID_W = 64
HEAD_DIM = 64
N_HEADS = 4
MIX_W = N_HEADS * HEAD_DIM
SWA_KV_HEADS = 2
LORA_RANK = 64
GATE_RANK = 128
RWKV_DECAY_SCALE = 0.6065306597126334
RWKV_LN_EPS = 64e-5
NAT_KH = 8
NAT_KW = 16
SWA_WINDOW = 128
SWA_BLOCK = 128
ROPE_BASE = 10000.0
FFN_HIDDEN = 4 * D_MODEL
NORM_EPS = 1e-6
MASK_VALUE = -1e30
N_MOD = 6
ATTN_SCALE = HEAD_DIM ** -0.5

RW_COLS = 3 * MIX_W + GATE_RANK + 4 * LORA_RANK
NAT_COLS = 3 * MIX_W
HG_COLS = 5 * MIX_W
SWA_COLS = MIX_W + 2 * SWA_KV_HEADS * HEAD_DIM
SLABS = (RW_COLS, NAT_COLS, HG_COLS, SWA_COLS)
D_IN = sum(SLABS)

LANES = 128
SUBLANES = 8
ROW_TILE = 256
FFN_ROW_TILE = 512
MOD_ROWS = 8
SCAN_KEY_CHUNK = 16


def _sigmoid(x):
    return 1.0 / (1.0 + jnp.exp(-x))


def _rms_rows(x, g):
    ms = jnp.mean(x * x, axis=-1, keepdims=True)
    return x * lax.rsqrt(ms + NORM_EPS) * g


def _bdot(a, b):
    return jnp.dot(a.astype(BF16), b.astype(BF16), preferred_element_type=F32)


def _bdot_nt(a, b):
    return lax.dot_general(a.astype(BF16), b.astype(BF16), (((1,), (1,)), ((), ())),
                           preferred_element_type=F32)


def _seg_sum(x, seg):
    return jnp.dot(x, seg, preferred_element_type=F32, precision=HIGHEST)


def _exact_select_dot(x, sel):
    return _select_dot3(_split3(x), sel)


def _split3(x):
    hi = x.astype(BF16)
    r1 = x - hi.astype(F32)
    mid = r1.astype(BF16)
    lo = (r1 - mid.astype(F32)).astype(BF16)
    return hi, mid, lo


def _select_dot3(parts, sel):
    hi, mid, lo = (jnp.dot(a, sel, preferred_element_type=F32) for a in parts)
    return hi + mid + lo


def _params(*sem):
    return pltpu.CompilerParams(dimension_semantics=sem)


def _resident(shape, index_map):
    return pl.BlockSpec(shape, index_map, pipeline_mode=pl.Buffered(1))


def _mod_kernel(cond_ref, w_ref, b_ref, o_ref):
    cnd = cond_ref[...]
    s = cnd * _sigmoid(cnd)
    o_ref[...] = _bdot(s, w_ref[...]) + b_ref[...]


def _modulation_all(cond, mod_w, mod_b):
    tn = 1536
    n = N_MOD * D_MODEL
    return pl.pallas_call(
        _mod_kernel,
        grid=(DEPTH, n // tn),
        in_specs=[pl.BlockSpec((MOD_ROWS, D_MODEL), lambda l, j: (0, 0)),
                  pl.BlockSpec((None, D_MODEL, tn), lambda l, j: (l, 0, j)),
                  pl.BlockSpec((None, 1, tn), lambda l, j: (l, 0, j))],
        out_specs=pl.BlockSpec((None, MOD_ROWS, tn), lambda l, j: (l, 0, j)),
        out_shape=jax.ShapeDtypeStruct((DEPTH, MOD_ROWS, n), F32),
        compiler_params=_params("parallel", "parallel"),
        name="modulation",
    )(cond, mod_w, mod_b.reshape(DEPTH, 1, n))


def _mod_row(seq_len, first_row):
    if first_row == 0:
        return lambda start: 0
    return lambda start: first_row + start // seq_len


SEQ_TILE_STEPS = SUBLANES


def _tile_perm(n_seq, to_seq_major):
    r = np.arange(ROW_TILE)
    src = (r % SEQ_TILE_STEPS) * n_seq + r // SEQ_TILE_STEPS
    p = (src[:, None] == r[None, :])
    return jnp.asarray(p if to_seq_major else p.T, dtype=BF16)


def _permute_rows(perm_ref, xb):
    return jnp.dot(perm_ref[...], xb, preferred_element_type=F32).astype(BF16)


def _inproj_kernel(*refs, row_fn, modes):
    x_ref, g_ref, mod_ref = refs[0:3]
    w_refs = refs[3:7]
    perm_ref = refs[7] if "seq" in modes else None
    o_refs = refs[-4:]
    row = row_fn(pl.program_id(0) * ROW_TILE)
    shift = mod_ref[pl.ds(row, 1), 0:D_MODEL]
    scale = mod_ref[pl.ds(row, 1), D_MODEL:2 * D_MODEL]
    h = _rms_rows(x_ref[...], g_ref[0:1, :]) * (1.0 + scale) + shift
    hb = h.astype(BF16)
    hb_seq = _permute_rows(perm_ref, hb) if perm_ref is not None else None
    for w_ref, o_ref, mode in zip(w_refs, o_refs, modes):
        if mode == "fm":
            o_ref[...] = lax.dot_general(w_ref[...], hb, (((1,), (1,)), ((), ())),
                                         preferred_element_type=F32)
        elif mode == "seq":
            o_ref[...] = jnp.dot(hb_seq, w_ref[...], preferred_element_type=F32).reshape(o_ref.shape)
        else:
            o_ref[...] = jnp.dot(hb, w_ref[...], preferred_element_type=F32)


def _in_projection(x, norm_g, mod, w_slabs, layer, row_fn, modes, n_seq=None):
    n = x.shape[0]
    out_specs, out_shape = [], []
    for width, mode in zip(SLABS, modes):
        if mode == "fm":
            out_specs.append(pl.BlockSpec((width, ROW_TILE), lambda i: (0, i)))
            out_shape.append(jax.ShapeDtypeStruct((width, n), F32))
        elif mode == "seq":
            out_specs.append(pl.BlockSpec((n_seq, SEQ_TILE_STEPS, width), lambda i: (0, i, 0)))
            out_shape.append(jax.ShapeDtypeStruct((n_seq, n // n_seq, width), F32))
        else:
            out_specs.append(pl.BlockSpec((ROW_TILE, width), lambda i: (i, 0)))
            out_shape.append(jax.ShapeDtypeStruct((n, width), F32))
    in_specs = [pl.BlockSpec((ROW_TILE, D_MODEL), lambda i: (i, 0)),
                pl.BlockSpec((None, 4, D_MODEL), lambda i: (layer, 0, 0)),
                pl.BlockSpec((None, MOD_ROWS, N_MOD * D_MODEL), lambda i: (layer, 0, 0))]
    in_specs += [_resident((None,) + w.shape[1:], lambda i: (layer, 0, 0)) for w in w_slabs]
    args = [x, norm_g, mod, *w_slabs]
    if "seq" in modes:
        assert n_seq * SEQ_TILE_STEPS == ROW_TILE
        in_specs.append(pl.BlockSpec((ROW_TILE, ROW_TILE), lambda i: (0, 0)))
        args.append(_tile_perm(n_seq, True))
    return pl.pallas_call(
        functools.partial(_inproj_kernel, row_fn=row_fn, modes=modes),
        grid=(n // ROW_TILE,),
        in_specs=in_specs,
        out_specs=out_specs,
        out_shape=out_shape,
        compiler_params=_params("parallel"),
        name="in_projection",
    )(*args)


def _outproj_ffn_kernel(*refs, row_fn, modes):
    x_ref = refs[0]
    y_refs = refs[1:5]
    g_ref, mod_ref, wo_ref, w1_ref, w2_ref = refs[5:10]
    perm_ref = refs[10] if "seq" in modes else None
    o_ref = refs[-1]
    row = row_fn(pl.program_id(0) * FFN_ROW_TILE)
    mod = lambda j: mod_ref[pl.ds(row, 1), j * D_MODEL:(j + 1) * D_MODEL]
    x = x_ref[...]
    acc = None
    for j, (y_ref, mode) in enumerate(zip(y_refs, modes)):
        if mode == "seq":
            yb = jnp.concatenate(
                [_permute_rows(perm_ref, y_ref[:, sub * SEQ_TILE_STEPS:(sub + 1) * SEQ_TILE_STEPS, :]
                               .reshape(ROW_TILE, MIX_W).astype(BF16))
                 for sub in range(FFN_ROW_TILE // ROW_TILE)], axis=0)
        else:
            yb = y_ref[...].astype(BF16)
        part = jnp.dot(yb, wo_ref[j * MIX_W:(j + 1) * MIX_W, :], preferred_element_type=F32)
        acc = part if acc is None else acc + part
    x = x + mod(2) * _rms_rows(acc, g_ref[1:2, :])
    h = (_rms_rows(x, g_ref[2:3, :]) * (1.0 + mod(4)) + mod(3)).astype(BF16)
    f = None
    for j in range(FFN_HIDDEN // D_MODEL):
        u = jnp.dot(h, w1_ref[:, j * D_MODEL:(j + 1) * D_MODEL], preferred_element_type=F32)
        u = jnp.square(jnp.maximum(u, 0.0)).astype(BF16)
        part = jnp.dot(u, w2_ref[j * D_MODEL:(j + 1) * D_MODEL, :], preferred_element_type=F32)
        f = part if f is None else f + part
    o_ref[...] = x + mod(5) * _rms_rows(f, g_ref[3:4, :])


def _out_projection_ffn(x, ys, norm_g, mod, w_out_bf, w1_bf, w2_bf, layer, row_fn, modes, n_seq=None):
    n = x.shape[0]
    sub = FFN_ROW_TILE // ROW_TILE
    tile = lambda w: pl.BlockSpec((FFN_ROW_TILE, w), lambda i: (i, 0))
    y_specs = [pl.BlockSpec((n_seq, sub * SEQ_TILE_STEPS, MIX_W), lambda i: (0, i, 0)) if m == "seq"
               else tile(MIX_W) for m in modes]
    in_specs = [tile(D_MODEL)] + y_specs + [
        pl.BlockSpec((None, 4, D_MODEL), lambda i: (layer, 0, 0)),
        pl.BlockSpec((None, MOD_ROWS, N_MOD * D_MODEL), lambda i: (layer, 0, 0)),
        _resident((None, D_MODEL, D_MODEL), lambda i: (layer, 0, 0)),
        _resident((None, D_MODEL, FFN_HIDDEN), lambda i: (layer, 0, 0)),
        _resident((None, FFN_HIDDEN, D_MODEL), lambda i: (layer, 0, 0))]
    args = [x, *ys, norm_g, mod, w_out_bf, w1_bf, w2_bf]
    if "seq" in modes:
        in_specs.append(pl.BlockSpec((ROW_TILE, ROW_TILE), lambda i: (0, 0)))
        args.append(_tile_perm(n_seq, False))
    return pl.pallas_call(
        functools.partial(_outproj_ffn_kernel, row_fn=row_fn, modes=modes),
        grid=(n // FFN_ROW_TILE,),
        in_specs=in_specs,
        out_specs=tile(D_MODEL),
        out_shape=jax.ShapeDtypeStruct((n, D_MODEL), F32),
        compiler_params=_params("parallel"),
        name="out_projection_ffn",
    )(*args)


def _scan_kernel(*refs, n_rows, delta, steps, v_tiles, packed, reverse_odd):
    if packed:
        rows_ref, v_ref, s0_ref, e_ref, y_ref, sfin_ref, s_scr, x_scr, t_scr, v_scr, y_scr = refs
    else:
        rows_ref, v_ref, s0_ref, y_ref, sfin_ref, s_scr = refs
    tb = pl.program_id(1)

    @pl.when(tb == 0)
    def _():
        s_scr[...] = s0_ref[...]

    if packed:
        per_col = e_ref.shape[0]
        n_col = HEAD_DIM // per_col
        for j in range(n_rows):
            for col in range(n_col):
                t_scr[:, col, :] = rows_ref[j, :, col * LANES:(col + 1) * LANES]
            parts = _split3(t_scr[...].reshape(steps * n_col, LANES))
            for s in range(per_col):
                x_scr[j, :, s * n_col:(s + 1) * n_col, :] = _select_dot3(parts, e_ref[s]).reshape(
                    steps, n_col, LANES)
        for s in range(SUBLANES):
            v_scr[:, s, :] = v_ref[:, s * LANES:(s + 1) * LANES]

    n_part = max(1, 4 // v_tiles)
    tiles = [slice(i * SUBLANES, (i + 1) * SUBLANES) for i in range(v_tiles)]
    if delta:
        i_nkk, i_w, i_kka, i_kv, i_r = range(5)
    else:
        i_w, i_kv, i_r = range(3)

    k_chunk = SCAN_KEY_CHUNK if v_tiles > 1 else HEAD_DIM
    zero = jnp.zeros((SUBLANES, LANES), F32)
    acc0 = tuple(zero for _ in range(v_tiles * n_part))

    def over_keys(body):
        def chunk(c, acc):
            acc = list(acc)
            for kk in range(k_chunk):
                acc = body(c * k_chunk + kk, kk, acc)
            return tuple(acc)
        if k_chunk == HEAD_DIM:
            acc = chunk(0, acc0)
        else:
            acc = lax.fori_loop(0, HEAD_DIM // k_chunk, chunk, acc0)
        sums = []
        for i in range(v_tiles):
            parts = list(acc[i * n_part:(i + 1) * n_part])
            while len(parts) > 1:
                parts = [a + b for a, b in zip(parts[0::2], parts[1::2])]
            sums.append(parts[0])
        return sums

    backward = (pl.program_id(0) % 2 == 1) if reverse_odd else None

    def step(i_step, carry):
        t = jnp.where(backward, steps - 1 - i_step, i_step) if reverse_odd else i_step
        if packed:
            row = lambda j, k: x_scr[j, t, pl.ds(k, 1), :]
            vt = [v_scr[t]]
        else:
            row = lambda j, k: rows_ref[j, t, pl.ds(k, 1), :]
            vt = [v_ref[t, tl, :] for tl in tiles]
        if delta:
            def sa_body(k, kk, acc):
                nk = row(i_nkk, k)
                for i, tl in enumerate(tiles):
                    p = i * n_part + kk % n_part
                    acc[p] = acc[p] + s_scr[k, tl, :] * nk
                return acc
            sa = over_keys(sa_body)

        def update_body(k, kk, acc):
            wk, kvk, rk = row(i_w, k), row(i_kv, k), row(i_r, k)
            if delta:
                kkak = row(i_kka, k)
            for i, tl in enumerate(tiles):
                s = s_scr[k, tl, :] * wk + vt[i] * kvk
                if delta:
                    s = s + sa[i] * kkak
                s_scr[k, tl, :] = s
                p = i * n_part + kk % n_part
                acc[p] = acc[p] + s * rk
            return acc
        y = over_keys(update_body)
        if packed:
            y_scr[t] = y[0]
        else:
            for i, tl in enumerate(tiles):
                y_ref[t, tl, :] = y[i]
        return carry

    lax.fori_loop(0, steps, step, 0)
    if packed:
        for s in range(SUBLANES):
            y_ref[:, s * LANES:(s + 1) * LANES] = y_scr[:, s, :]

    @pl.when(tb == pl.num_programs(1) - 1)
    def _():
        sfin_ref[...] = s_scr[...]


def _scan(rows, v, s0, *, delta, steps, reverse_odd=False):
    groups, _, t_len = rows.shape[:3]
    n_rows = 5 if delta else 3
    n_blocks = t_len // steps
    packed = v is not None
    tblk = (lambda g, t: jnp.where(g % 2 == 1, n_blocks - 1 - t, t)) if reverse_odd else (lambda g, t: t)
    vrows = s0.shape[2]
    state_spec = pl.BlockSpec((None, HEAD_DIM, vrows, LANES), lambda g, t: (g, 0, 0, 0))
    scratch = [pltpu.VMEM((HEAD_DIM, vrows, LANES), F32)]
    if packed:
        width = rows.shape[-1]
        n_chain = width // HEAD_DIM
        per_col = LANES // n_chain
        src = np.arange(LANES)
        e = np.stack([(src[:, None] // n_chain == s) & (src[:, None] % n_chain == src[None, :] % n_chain)
                      for s in range(per_col)])
        assert HEAD_DIM // per_col == SUBLANES and vrows == SUBLANES
        io_spec = pl.BlockSpec((None, steps, width), lambda g, t: (g, tblk(g, t), 0))
        in_specs = [pl.BlockSpec((None, n_rows, steps, width), lambda g, t: (g, 0, tblk(g, t), 0)),
                    io_spec, state_spec,
                    pl.BlockSpec((per_col, LANES, LANES), lambda g, t: (0, 0, 0))]
        s0 = s0.reshape(groups, SUBLANES, per_col, vrows, LANES).transpose(0, 2, 1, 3, 4).reshape(s0.shape)
        args = [rows, v, s0, jnp.asarray(e, dtype=BF16)]
        scratch += [pltpu.VMEM((n_rows, steps, HEAD_DIM, LANES), F32)] + [pltpu.VMEM((steps, SUBLANES, LANES), F32)] * 3
        y_spec, y_shape = io_spec, jax.ShapeDtypeStruct((groups, t_len, width), F32)
    else:
        in_specs = [pl.BlockSpec((None, n_rows, steps, HEAD_DIM, LANES), lambda g, t: (g, 0, tblk(g, t), 0, 0)),
                    pl.BlockSpec((None, None, steps, vrows, LANES), lambda g, t: (g, n_rows, tblk(g, t), 0, 0)),
                    state_spec]
        args = [rows, rows, s0]
        y_spec = pl.BlockSpec((None, steps, vrows, LANES), lambda g, t: (g, tblk(g, t), 0, 0))
        y_shape = jax.ShapeDtypeStruct((groups, t_len, vrows, LANES), F32)
    return pl.pallas_call(
        functools.partial(_scan_kernel, n_rows=n_rows, delta=delta, steps=steps, v_tiles=vrows // SUBLANES,
                          packed=packed, reverse_odd=reverse_odd),
        grid=(groups, n_blocks),
        in_specs=in_specs,
        out_specs=[y_spec, state_spec],
        out_shape=[y_shape, jax.ShapeDtypeStruct((groups, HEAD_DIM, vrows, LANES), F32)],
        scratch_shapes=scratch,
        compiler_params=_params("parallel", "arbitrary"),
        name="rwkv_scan" if delta else "hgrn_scan",
    )(*args)


def _to_scan_layout(a, n_seq, t_len, per_dir_groups):
    n_arr = a.shape[1]
    a = a.reshape(2, n_arr, n_seq, t_len, N_HEADS, HEAD_DIM)
    a = jnp.stack([a[0], a[1][:, :, ::-1]])
    if per_dir_groups:
        r = a.transpose(0, 1, 3, 5, 2, 4).reshape(2, n_arr, t_len, HEAD_DIM, n_seq * N_HEADS)
        return r[:, :n_arr - 1], r[:, n_arr - 1]
    n_chain = 2 * n_seq * N_HEADS
    vg = LANES // n_chain
    rows = a[:, :n_arr - 1].transpose(1, 3, 5, 0, 2, 4).reshape(1, n_arr - 1, t_len, HEAD_DIM, n_chain)
    v = a[:, n_arr - 1].reshape(2, n_seq, t_len, N_HEADS, vg, HEAD_DIM // vg)
    v = v.transpose(2, 5, 4, 0, 1, 3).reshape(1, t_len, HEAD_DIM // vg, LANES)
    return rows, v


def _from_scan_layout(y, n_seq, t_len, per_dir_groups):
    if per_dir_groups:
        y = y.reshape(2, t_len, HEAD_DIM, n_seq, N_HEADS)
        y = y[0] + y[1][::-1]
        return y.transpose(2, 0, 3, 1).reshape(n_seq * t_len, MIX_W)
    n_chain = 2 * n_seq * N_HEADS
    vg = LANES // n_chain
    y = y.reshape(t_len, HEAD_DIM // vg, vg, 2, n_seq, N_HEADS)
    y = y[:, :, :, 0] + y[::-1, :, :, 1]
    return y.transpose(3, 0, 4, 2, 1).reshape(n_seq * t_len, MIX_W)


def _state_to_scan(s, key_last):
    n_seq = s.shape[0]
    vg = LANES // (2 * n_seq * N_HEADS)
    if key_last:
        s = s.reshape(n_seq, 2, N_HEADS, vg, HEAD_DIM // vg, HEAD_DIM).transpose(5, 4, 3, 1, 0, 2)
    else:
        s = s.reshape(n_seq, 2, N_HEADS, HEAD_DIM, vg, HEAD_DIM // vg).transpose(3, 5, 4, 1, 0, 2)
    return s.reshape(1, HEAD_DIM, HEAD_DIM // vg, LANES)


def _state_from_scan(s, n_seq, key_last):
    s = s.reshape(2, HEAD_DIM, HEAD_DIM, N_HEADS, n_seq)
    return s.transpose(4, 0, 3, 2, 1) if key_last else s.transpose(4, 0, 3, 1, 2)


def _shifted(x, d, t_len):
    row = lax.broadcasted_iota(jnp.int32, (t_len, 1), 0)
    prev_f = jnp.where(row == 0, 0.0, pltpu.roll(x, 1, 0))
    prev_b = jnp.where(row == t_len - 1, 0.0, pltpu.roll(x, t_len - 1, 0))
    return jnp.where(d == 0, prev_f, prev_b)


def _rwkv_pre_kernel(rkv_ref, lora_ref, mu_rkv_ref, mu_lo_ref, wl_ref, b0_ref, kk_ref, ka_ref, rk_ref,
                     seg_ref, scan_ref, bonus_ref, *, t_len):
    d = pl.program_id(1)
    x = rkv_ref[...]
    x = x + (_shifted(x, d, t_len) - x) * mu_rkv_ref[...]
    r, k, v = x[:, 0:MIX_W], x[:, MIX_W:2 * MIX_W], x[:, 2 * MIX_W:3 * MIX_W]
    lo = lora_ref[...]
    lo = lo + (_shifted(lo, d, t_len) - lo) * mu_lo_ref[...]
    lane = lax.broadcasted_iota(jnp.int32, lo.shape, 1)
    lo = jnp.where(lane < LORA_RANK, jnp.tanh(lo), lo)
    pre = _bdot(lo, wl_ref[...]) + b0_ref[...]
    w = jnp.exp(-RWKV_DECAY_SCALE * _sigmoid(pre[:, 0:MIX_W]))
    a = _sigmoid(pre[:, MIX_W:2 * MIX_W])
    seg = seg_ref[...]
    kk = k * kk_ref[...]
    kk = kk / jnp.maximum(jnp.sqrt(_seg_sum(kk * kk, seg)), 1e-12)
    k = k * (1.0 + (a - 1.0) * ka_ref[...])
    scan_ref[0] = -kk
    scan_ref[1] = w
    scan_ref[2] = kk * a
    scan_ref[3] = k
    scan_ref[4] = r
    scan_ref[5] = v
    bonus_ref[...] = _seg_sum(r * k * rk_ref[...], seg) * v


def _rwkv_pre(p_rw, lw, seg, n_seq, t_len, seq0):
    n = n_seq * t_len
    lora_blk = (3 * MIX_W + GATE_RANK) // (2 * LORA_RANK)
    vec = lambda w: pl.BlockSpec((None, 1, w), lambda i, d: (d, 0, 0))
    shared = lambda w: pl.BlockSpec((1, w), lambda i, d: (0, 0))
    return pl.pallas_call(
        functools.partial(_rwkv_pre_kernel, t_len=t_len),
        grid=(n_seq, 2),
        in_specs=[pl.BlockSpec((t_len, 3 * MIX_W), lambda i, d: (seq0 + i, 0)),
                  pl.BlockSpec((t_len, 2 * LORA_RANK), lambda i, d: (seq0 + i, lora_blk + d)),
                  vec(3 * MIX_W), vec(2 * LORA_RANK),
                  pl.BlockSpec((None, 2 * LORA_RANK, 2 * MIX_W), lambda i, d: (d, 0, 0)),
                  vec(2 * MIX_W), shared(MIX_W), shared(MIX_W), shared(MIX_W),
                  pl.BlockSpec((MIX_W, MIX_W), lambda i, d: (0, 0))],
        out_specs=[pl.BlockSpec((None, 6, t_len, MIX_W), lambda i, d: (d, 0, i, 0)),
                   pl.BlockSpec((None, t_len, MIX_W), lambda i, d: (d, i, 0))],
        out_shape=[jax.ShapeDtypeStruct((2, 6, n, MIX_W), F32),
                   jax.ShapeDtypeStruct((2, n, MIX_W), F32)],
        compiler_params=_params("parallel", "parallel"),
        name="rwkv_pre",
    )(p_rw, p_rw, lw["mu_rkv"], lw["mu_lo"], lw["w_lora"], lw["b_lora"], lw["rw_kk"], lw["rw_ka"],
      lw["rw_rk"], seg)


def _rwkv_post_kernel(y_ref, bonus_ref, gh_ref, g2_ref, lnw_ref, lnb_ref, seg_ref, o_ref):
    seg = seg_ref[...]
    y = y_ref[...]
    mu = _seg_sum(y, seg) * (1.0 / HEAD_DIM)
    yc = y - mu
    var = _seg_sum(yc * yc, seg) * (1.0 / HEAD_DIM)
    y = yc * lax.rsqrt(var + RWKV_LN_EPS) * lnw_ref[...] + lnb_ref[...]
    y = y + bonus_ref[0] + bonus_ref[1]
    o_ref[...] = y * _bdot(_sigmoid(gh_ref[...]), g2_ref[...])


def _rwkv_post(y, bonus, p_rw, lw, seg, tile0):
    n = y.shape[0]
    gh_blk = 3 * MIX_W // GATE_RANK
    shared = lambda w: pl.BlockSpec((1, w), lambda i: (0, 0))
    return pl.pallas_call(
        _rwkv_post_kernel,
        grid=(n // ROW_TILE,),
        in_specs=[pl.BlockSpec((ROW_TILE, MIX_W), lambda i: (i, 0)),
                  pl.BlockSpec((2, ROW_TILE, MIX_W), lambda i: (0, i, 0)),
                  pl.BlockSpec((ROW_TILE, GATE_RANK), lambda i: (tile0 + i, gh_blk)),
                  pl.BlockSpec((GATE_RANK, MIX_W), lambda i: (0, 0)),
                  shared(MIX_W), shared(MIX_W),
                  pl.BlockSpec((MIX_W, MIX_W), lambda i: (0, 0))],
        out_specs=pl.BlockSpec((ROW_TILE, MIX_W), lambda i: (i, 0)),
        out_shape=jax.ShapeDtypeStruct((n, MIX_W), F32),
        compiler_params=_params("parallel"),
        name="rwkv_post",
    )(y, bonus, p_rw, lw["rw_g2"], lw["rw_lnx_w"], lw["rw_lnx_b"], seg)


def _hgrn_pre_kernel(qv_ref, f_ref, lb_ref, scan_ref):
    q = qv_ref[:, 0:MIX_W]
    f_raw = f_ref[...]
    lb = lb_ref[...]
    scan_ref[0] = lb + (1.0 - lb) * _sigmoid(f_raw)
    scan_ref[1] = (1.0 - lb) * _sigmoid(-f_raw)
    scan_ref[2] = q * _sigmoid(q)
    scan_ref[3] = qv_ref[:, MIX_W:2 * MIX_W]


def _hgrn_pre(p_hg, lb, n_tok, tile0):
    return pl.pallas_call(
        _hgrn_pre_kernel,
        grid=(n_tok // ROW_TILE, 2),
        in_specs=[pl.BlockSpec((ROW_TILE, 2 * MIX_W), lambda i, d: (tile0 + i, 0)),
                  pl.BlockSpec((ROW_TILE, MIX_W), lambda i, d: (tile0 + i, 3 + d)),
                  pl.BlockSpec((None, 1, MIX_W), lambda i, d: (d, 0, 0))],
        out_specs=pl.BlockSpec((None, 4, ROW_TILE, MIX_W), lambda i, d: (d, 0, i, 0)),
        out_shape=jax.ShapeDtypeStruct((2, 4, n_tok, MIX_W), F32),
        compiler_params=_params("parallel", "parallel"),
        name="hgrn_pre",
    )(p_hg, p_hg, lb)


def _hgrn_post_kernel(o_ref, g_ref, nw_ref, seg_ref, out_ref):
    o = o_ref[...]
    ms = _seg_sum(o * o, seg_ref[...]) * (1.0 / HEAD_DIM)
    out_ref[...] = o * lax.rsqrt(ms + NORM_EPS) * nw_ref[...] * _sigmoid(g_ref[...])


def _hgrn_post(o, p_hg, hg_norm, seg, tile0):
    n = o.shape[0]
    return pl.pallas_call(
        _hgrn_post_kernel,
        grid=(n // ROW_TILE,),
        in_specs=[pl.BlockSpec((ROW_TILE, MIX_W), lambda i: (i, 0)),
                  pl.BlockSpec((ROW_TILE, MIX_W), lambda i: (tile0 + i, 2)),
                  pl.BlockSpec((1, MIX_W), lambda i: (0, 0)),
                  pl.BlockSpec((MIX_W, MIX_W), lambda i: (0, 0))],
        out_specs=pl.BlockSpec((ROW_TILE, MIX_W), lambda i: (i, 0)),
        out_shape=jax.ShapeDtypeStruct((n, MIX_W), F32),
        compiler_params=_params("parallel"),
        name="hgrn_post",
    )(o, p_hg, hg_norm, seg)


CTX_TILE_STEPS = ROW_TILE // 32
RWKV_MATMUL_REGROUPS = (5, 4)
HGRN_MATMUL_REGROUPS = 2


def _swap_lead4(n_seq):
    i = np.arange(N_HEADS * N_HEADS * n_seq)
    a, b, seq = i // (N_HEADS * n_seq), (i // n_seq) % N_HEADS, i % n_seq
    p = np.zeros((i.size, i.size), np.float32)
    p[i, (b * N_HEADS + a) * n_seq + seq] = 1.0
    return jnp.asarray(p, dtype=BF16)


def _fm_to_scan_rolls(a, n_seq):
    grp = lax.broadcasted_iota(jnp.int32, (HEAD_DIM, LANES), 1) // n_seq
    rolled = []
    for h in range(N_HEADS):
        blk = a[h * HEAD_DIM:(h + 1) * HEAD_DIM]
        rolled.append([blk if s == 0 else pltpu.roll(blk, s * n_seq, 1) for s in range(N_HEADS)])
    tiles = []
    for t in range(N_HEADS):
        out = rolled[0][(0 - t) % N_HEADS]
        for h in range(1, N_HEADS):
            out = jnp.where(grp == h, rolled[h][(h - t) % N_HEADS], out)
        tiles.append(out)
    return tiles


def _store_scan_tiles(scan_ref, d, arrays, swap_ref, n_seq, n_matmul):
    n_col = arrays[0].shape[1] // LANES
    lhs = jnp.concatenate(
        [jnp.concatenate([a[h * HEAD_DIM:(h + 1) * HEAD_DIM, c * LANES:(c + 1) * LANES] for h in range(N_HEADS)],
                         axis=1) for a in arrays[:n_matmul] for c in range(n_col)], axis=0)
    out = _exact_select_dot(lhs, swap_ref[...])
    for j, a in enumerate(arrays):
        for c in range(n_col):
            if j < n_matmul:
                r0 = (j * n_col + c) * HEAD_DIM
                tiles = [out[r0:r0 + HEAD_DIM, t * LANES:(t + 1) * LANES] for t in range(N_HEADS)]
            else:
                tiles = _fm_to_scan_rolls(a[:, c * LANES:(c + 1) * LANES], n_seq)
            for t in range(N_HEADS):
                scan_ref[d, j, c * N_HEADS + t] = tiles[t]


def _load_scan_tiles(y_ref, swap_ref):
    n_col = y_ref.shape[1] // N_HEADS
    lhs = jnp.concatenate(
        [jnp.concatenate([y_ref[0, c * N_HEADS + t] + y_ref[1, c * N_HEADS + t] for t in range(N_HEADS)], axis=1)
         for c in range(n_col)], axis=0)
    out = _exact_select_dot(lhs, swap_ref[...])
    return jnp.concatenate(
        [jnp.concatenate([out[c * HEAD_DIM:(c + 1) * HEAD_DIM, h * LANES:(h + 1) * LANES] for h in range(N_HEADS)],
                         axis=0) for c in range(n_col)], axis=1)


def _head_sum_rows(x):
    w = x.shape[1]
    s = jnp.sum(x.reshape(N_HEADS, HEAD_DIM, w), axis=1, keepdims=True)
    return jnp.broadcast_to(s, (N_HEADS, HEAD_DIM, w)).reshape(N_HEADS * HEAD_DIM, w)


def _lane_halo(halo, shift, valid, width):
    h = jnp.where(valid, pltpu.roll(halo, shift, 1), 0.0)
    return jnp.concatenate([h] * (width // LANES), axis=1)


def _prev_step(x, halo, valid, n_seq):
    width = x.shape[1]
    lane = lax.broadcasted_iota(jnp.int32, (1, width), 1)
    return jnp.where(lane < n_seq, _lane_halo(halo, n_seq, valid, width), pltpu.roll(x, n_seq, 1))


def _next_step(x, halo, valid, n_seq):
    width = x.shape[1]
    lane = lax.broadcasted_iota(jnp.int32, (1, width), 1)
    return jnp.where(lane >= width - n_seq, _lane_halo(halo, LANES - n_seq, valid, width),
                     pltpu.roll(x, width - n_seq, 1))


def _rwkv_pre_fm_kernel(x_ref, hp_ref, hn_ref, mu_rkv_ref, mu_lo_ref, w2t_ref, a2t_ref, w0_ref, a0_ref,
                        kk_ref, ka_ref, rk_ref, swap_ref, scan_ref, bonus_ref, *, n_seq):
    i = pl.program_id(0)
    last = pl.num_programs(0) - 1
    rkv_rows = slice(0, 3 * MIX_W)
    x = x_ref[rkv_rows, :]
    bonus = None
    for d in range(2):
        if d == 0:
            shifted = lambda a, rows: _prev_step(a, hp_ref[rows, :], i > 0, n_seq)
        else:
            shifted = lambda a, rows: _next_step(a, hn_ref[rows, :], i < last, n_seq)
        x3 = x + (shifted(x, rkv_rows) - x) * mu_rkv_ref[d]
        r, k, v = x3[0:MIX_W], x3[MIX_W:2 * MIX_W], x3[2 * MIX_W:3 * MIX_W]
        lo_rows = slice(3 * MIX_W + GATE_RANK + 2 * LORA_RANK * d, 3 * MIX_W + GATE_RANK + 2 * LORA_RANK * (d + 1))
        lo = x_ref[lo_rows, :]
        lo = lo + (shifted(lo, lo_rows) - lo) * mu_lo_ref[d]
        w_pre = _bdot(w2t_ref[d], jnp.tanh(lo[0:LORA_RANK])) + w0_ref[d]
        a_pre = _bdot(a2t_ref[d], lo[LORA_RANK:2 * LORA_RANK]) + a0_ref[d]
        w = jnp.exp(-RWKV_DECAY_SCALE * _sigmoid(w_pre))
        a = _sigmoid(a_pre)
        kk = k * kk_ref[...]
        kk = kk / jnp.maximum(jnp.sqrt(_head_sum_rows(kk * kk)), 1e-12)
        k = k * (1.0 + (a - 1.0) * ka_ref[...])
        _store_scan_tiles(scan_ref, d, (-kk, w, kk * a, k, r, v), swap_ref, n_seq, RWKV_MATMUL_REGROUPS[d])
        b = _head_sum_rows(r * k * rk_ref[...]) * v
        bonus = b if bonus is None else bonus + b
    bonus_ref[...] = bonus


def _rwkv_pre_fm(pt_rw, lw, swap, n_seq, t_len):
    n = pt_rw.shape[1]
    n_tiles = n // ROW_TILE
    per_tile = ROW_TILE // LANES
    col = lambda rows: pl.BlockSpec((rows, 1), lambda i: (0, 0))
    dcol = lambda rows: pl.BlockSpec((2, rows, 1), lambda i: (0, 0, 0))
    return pl.pallas_call(
        functools.partial(_rwkv_pre_fm_kernel, n_seq=n_seq),
        grid=(n_tiles,),
        in_specs=[pl.BlockSpec((RW_COLS, ROW_TILE), lambda i: (0, i)),
                  pl.BlockSpec((RW_COLS, LANES), lambda i: (0, jnp.maximum(i * per_tile - 1, 0))),
                  pl.BlockSpec((RW_COLS, LANES),
                               lambda i: (0, jnp.minimum((i + 1) * per_tile, n_tiles * per_tile - 1))),
                  dcol(3 * MIX_W), dcol(2 * LORA_RANK),
                  pl.BlockSpec((2, MIX_W, LORA_RANK), lambda i: (0, 0, 0)),
                  pl.BlockSpec((2, MIX_W, LORA_RANK), lambda i: (0, 0, 0)),
                  dcol(MIX_W), dcol(MIX_W), col(MIX_W), col(MIX_W), col(MIX_W),
                  pl.BlockSpec(swap.shape, lambda i: (0, 0))],
        out_specs=[pl.BlockSpec((2, 6, CTX_TILE_STEPS, HEAD_DIM, LANES), lambda i: (0, 0, i, 0, 0)),
                   pl.BlockSpec((MIX_W, ROW_TILE), lambda i: (0, i))],
        out_shape=[jax.ShapeDtypeStruct((2, 6, t_len, HEAD_DIM, LANES), F32),
                   jax.ShapeDtypeStruct((MIX_W, n), F32)],
        compiler_params=_params("parallel"),
        name="rwkv_pre_fm",
    )(pt_rw, pt_rw, pt_rw, lw["mu_rkv_col"], lw["mu_lo_col"], lw["w2t"], lw["a2t"], lw["w0_col"],
      lw["a0_col"], lw["kk_col"], lw["ka_col"], lw["rk_col"], swap)


def _rwkv_post_fm_kernel(y_ref, bonus_ref, gh_ref, g2t_ref, lnw_ref, lnb_ref, swap_ref, o_ref):
    y = _load_scan_tiles(y_ref, swap_ref)
    mu = _head_sum_rows(y) * (1.0 / HEAD_DIM)
    yc = y - mu
    var = _head_sum_rows(yc * yc) * (1.0 / HEAD_DIM)
    y = yc * lax.rsqrt(var + RWKV_LN_EPS) * lnw_ref[...] + lnb_ref[...] + bonus_ref[...]
    g = _bdot(g2t_ref[...], _sigmoid(gh_ref[...]))
    o_ref[...] = (y * g).T


def _rwkv_post_fm(y, bonus, pt_rw, lw, swap):
    n = bonus.shape[1]
    col = pl.BlockSpec((MIX_W, 1), lambda i: (0, 0))
    return pl.pallas_call(
        _rwkv_post_fm_kernel,
        grid=(n // ROW_TILE,),
        in_specs=[pl.BlockSpec((2, CTX_TILE_STEPS, HEAD_DIM, LANES), lambda i: (0, i, 0, 0)),
                  pl.BlockSpec((MIX_W, ROW_TILE), lambda i: (0, i)),
                  pl.BlockSpec((GATE_RANK, ROW_TILE), lambda i: (3 * MIX_W // GATE_RANK, i)),
                  pl.BlockSpec((MIX_W, GATE_RANK), lambda i: (0, 0)),
                  col, col, pl.BlockSpec(swap.shape, lambda i: (0, 0))],
        out_specs=pl.BlockSpec((ROW_TILE, MIX_W), lambda i: (i, 0)),
        out_shape=jax.ShapeDtypeStruct((n, MIX_W), F32),
        compiler_params=_params("parallel"),
        name="rwkv_post_fm",
    )(y, bonus, pt_rw, lw["g2t"], lw["lnw_col"], lw["lnb_col"], swap)


def _hgrn_pre_fm_kernel(qv_ref, ff_ref, fb_ref, lb_ref, swap_ref, scan_ref, *, n_seq):
    q = qv_ref[0:MIX_W, :]
    q = q * _sigmoid(q)
    v = qv_ref[MIX_W:2 * MIX_W, :]
    for d, f_ref in enumerate((ff_ref, fb_ref)):
        f_raw = f_ref[...]
        lb = lb_ref[d]
        f = lb + (1.0 - lb) * _sigmoid(f_raw)
        kv = (1.0 - lb) * _sigmoid(-f_raw)
        _store_scan_tiles(scan_ref, d, (f, kv, q, v), swap_ref, n_seq, HGRN_MATMUL_REGROUPS)


def _hgrn_pre_fm(pt_hg, lb_col, swap, n_seq, t_len):
    n = pt_hg.shape[1]
    return pl.pallas_call(
        functools.partial(_hgrn_pre_fm_kernel, n_seq=n_seq),
        grid=(n // ROW_TILE,),
        in_specs=[pl.BlockSpec((2 * MIX_W, ROW_TILE), lambda i: (0, i)),
                  pl.BlockSpec((MIX_W, ROW_TILE), lambda i: (3, i)),
                  pl.BlockSpec((MIX_W, ROW_TILE), lambda i: (4, i)),
                  pl.BlockSpec((2, MIX_W, 1), lambda i: (0, 0, 0)),
                  pl.BlockSpec(swap.shape, lambda i: (0, 0))],
        out_specs=pl.BlockSpec((2, 4, CTX_TILE_STEPS, HEAD_DIM, LANES), lambda i: (0, 0, i, 0, 0)),
        out_shape=jax.ShapeDtypeStruct((2, 4, t_len, HEAD_DIM, LANES), F32),
        compiler_params=_params("parallel"),
        name="hgrn_pre_fm",
    )(pt_hg, pt_hg, pt_hg, lb_col, swap)


def _hgrn_post_fm_kernel(o_ref, g_ref, nw_ref, swap_ref, out_ref):
    o = _load_scan_tiles(o_ref, swap_ref)
    ms = _head_sum_rows(o * o) * (1.0 / HEAD_DIM)
    out_ref[...] = (o * lax.rsqrt(ms + NORM_EPS) * nw_ref[...] * _sigmoid(g_ref[...])).T


def _hgrn_post_fm(o, pt_hg, nw_col, swap):
    n = pt_hg.shape[1]
    return pl.pallas_call(
        _hgrn_post_fm_kernel,
        grid=(n // ROW_TILE,),
        in_specs=[pl.BlockSpec((2, CTX_TILE_STEPS, HEAD_DIM, LANES), lambda i: (0, i, 0, 0)),
                  pl.BlockSpec((MIX_W, ROW_TILE), lambda i: (2, i)),
                  pl.BlockSpec((MIX_W, 1), lambda i: (0, 0)),
                  pl.BlockSpec(swap.shape, lambda i: (0, 0))],
        out_specs=pl.BlockSpec((ROW_TILE, MIX_W), lambda i: (i, 0)),
        out_shape=jax.ShapeDtypeStruct((n, MIX_W), F32),
        compiler_params=_params("parallel"),
        name="hgrn_post_fm",
    )(o, pt_hg, nw_col, swap)


def _ctx_recurrent_mixers(pt_rw, pt_hg, lw, n_seq, t_len):
    assert n_seq * N_HEADS == LANES
    s0 = jnp.zeros((2, HEAD_DIM, HEAD_DIM, LANES), F32)
    swap = _swap_lead4(n_seq)
    scan_in, bonus = _rwkv_pre_fm(pt_rw, lw, swap, n_seq, t_len)
    y, s_rw = _scan(scan_in, None, s0, delta=True, steps=32, reverse_odd=True)
    ya = _rwkv_post_fm(y, bonus, pt_rw, lw, swap)
    scan_in = _hgrn_pre_fm(pt_hg, lw["hg_lb_col"], swap, n_seq, t_len)
    o, s_hg = _scan(scan_in, None, s0, delta=False, steps=32, reverse_odd=True)
    yc = _hgrn_post_fm(o, pt_hg, lw["hg_norm_col"], swap)
    return ya, yc, s_rw, s_hg


def _head(x, h):
    return x[:, h * HEAD_DIM:(h + 1) * HEAD_DIM]


def _softmax_av(scores, values, sink=None):
    m = None
    for s in scores:
        mi = jnp.max(s, axis=-1, keepdims=True)
        m = mi if m is None else jnp.maximum(m, mi)
    if sink is not None:
        m = jnp.maximum(m, sink)
    den = None if sink is None else jnp.exp(sink - m)
    out = None
    for s, v in zip(scores, values):
        e = jnp.exp(s - m)
        di = jnp.sum(e, axis=-1, keepdims=True)
        den = di if den is None else den + di
        oi = _bdot(e, v)
        out = oi if out is None else out + oi
    return out / den


def _ctx_attn_kernel(sink_ref, p_ref, y_ref, cache_ref, *, kv_heads, use_sink):
    group = N_HEADS // kv_heads
    kv_w = kv_heads * HEAD_DIM
    q_all = p_ref[:, 0:MIX_W] * ATTN_SCALE
    k_all = p_ref[:, MIX_W:MIX_W + kv_w]
    v_all = p_ref[:, MIX_W + kv_w:MIX_W + 2 * kv_w]
    cache_ref[0] = k_all
    cache_ref[1] = v_all
    outs = []
    for h in range(N_HEADS):
        kvh = h // group
        outs.append(_softmax_av([_bdot_nt(_head(q_all, h), _head(k_all, kvh))], [_head(v_all, kvh)],
                                sink=sink_ref[h] if use_sink else None))
    y_ref[...] = jnp.concatenate(outs, axis=-1)


def _ctx_attention(p, sink, kv_heads, use_sink):
    n_seq, t_len, cols = p.shape
    kv_w = kv_heads * HEAD_DIM
    return pl.pallas_call(
        functools.partial(_ctx_attn_kernel, kv_heads=kv_heads, use_sink=use_sink),
        grid=(n_seq,),
        in_specs=[pl.BlockSpec(memory_space=pltpu.SMEM),
                  pl.BlockSpec((None, t_len, cols), lambda i: (i, 0, 0))],
        out_specs=[pl.BlockSpec((None, t_len, MIX_W), lambda i: (i, 0, 0)),
                   pl.BlockSpec((None, 2, t_len, kv_w), lambda i: (i, 0, 0, 0))],
        out_shape=[jax.ShapeDtypeStruct((n_seq, t_len, MIX_W), F32),
                   jax.ShapeDtypeStruct((n_seq, 2, t_len, kv_w), F32)],
        compiler_params=_params("parallel"),
        name="ctx_attention",
    )(sink, p)


def _nat_latent_kernel(p_ref, ckv_ref, bias_ref, o_ref, *, rows):
    kh = min(NAT_KH, rows)
    n_loc = kh * GRID_W
    ck_all = ckv_ref[0]
    cv_all = ckv_ref[1]
    for r in range(rows):
        rs = min(max(r - kh // 2, 0), rows - kh)
        roff0 = rs - r + NAT_KH - 1
        q_all = p_ref[r * GRID_W:(r + 1) * GRID_W, 0:MIX_W] * ATTN_SCALE
        k_all = p_ref[rs * GRID_W:rs * GRID_W + n_loc, MIX_W:2 * MIX_W]
        v_all = p_ref[rs * GRID_W:rs * GRID_W + n_loc, 2 * MIX_W:3 * MIX_W]
        outs = []
        for h in range(N_HEADS):
            q = _head(q_all, h)
            bias = jnp.concatenate([bias_ref[h, roff0 + 2 * j] for j in range(kh // 2)], axis=-1)
            s_loc = _bdot_nt(q, _head(k_all, h)) + bias
            outs.append(_softmax_av([s_loc, _bdot_nt(q, _head(ck_all, h))],
                                    [_head(v_all, h), _head(cv_all, h)]))
        o_ref[r * GRID_W:(r + 1) * GRID_W, :] = jnp.concatenate(outs, axis=-1)


def _nat_bias_table(rpb):
    q = np.arange(GRID_W)[:, None]
    kc = np.arange(GRID_W)[None, :]
    win = np.clip(q - NAT_KW // 2, 0, GRID_W - NAT_KW)
    ok = (kc >= win) & (kc < win + NAT_KW)
    coff = np.clip(kc - q, -(NAT_KW - 1), NAT_KW - 1) + NAT_KW - 1
    t = jnp.where(ok[None, None], rpb[:, :, coff], MASK_VALUE)
    return jnp.concatenate([t[:, :-1], t[:, 1:]], axis=-1)


def _nat_latent(p_nat, cache, bias, layer, n_seq, t_len, tile0):
    rows = t_len // GRID_W
    return pl.pallas_call(
        functools.partial(_nat_latent_kernel, rows=rows),
        grid=(n_seq,),
        in_specs=[pl.BlockSpec((t_len, NAT_COLS), lambda i: (tile0 + i, 0)),
                  pl.BlockSpec((None, None, 2, cache.shape[3], MIX_W), lambda i: (i, layer, 0, 0, 0)),
                  pl.BlockSpec(bias.shape, lambda i: (0, 0, 0, 0))],
        out_specs=pl.BlockSpec((t_len, MIX_W), lambda i: (i, 0)),
        out_shape=jax.ShapeDtypeStruct((n_seq * t_len, MIX_W), F32),
        compiler_params=_params("parallel"),
        name="nat_latent",
    )(p_nat, cache, bias)


def _rope(x, cos, sin_signed):
    width = x.shape[-1]
    lane = lax.broadcasted_iota(jnp.int32, x.shape, 1)
    half = HEAD_DIM // 4
    partner = jnp.where(lane % (2 * half) < half, pltpu.roll(x, width - half, 1), pltpu.roll(x, half, 1))
    return x * cos + partner * sin_signed


def _swa_latent_kernel(sink_ref, p_ref, ckv_ref, cos_ref, sin_ref, o_ref, *, t_len):
    hd = HEAD_DIM
    kv_w = SWA_KV_HEADS * hd
    group = N_HEADS // SWA_KV_HEADS
    blk = SWA_BLOCK
    nb = t_len // blk
    q = _rope(p_ref[:, 0:MIX_W], cos_ref[...], sin_ref[...]) * ATTN_SCALE
    k = _rope(p_ref[:, MIX_W:MIX_W + kv_w], cos_ref[:, 0:kv_w], sin_ref[:, 0:kv_w])
    v = p_ref[:, MIX_W + kv_w:MIX_W + 2 * kv_w]
    row = lax.broadcasted_iota(jnp.int32, (group * blk, 1), 0)
    ck_all = ckv_ref[0]
    cv_all = ckv_ref[1]
    for n in range(nb):
        lo = max(n - 1, 0) * blk
        hi = min(n + 2, nb) * blk
        qpos = n * blk + row % blk
        kpos = lo + lax.broadcasted_iota(jnp.int32, (group * blk, hi - lo), 1)
        ok = jnp.abs(kpos - qpos) <= SWA_WINDOW
        heads = [None] * N_HEADS
        for kvh in range(SWA_KV_HEADS):
            qs = jnp.concatenate([_head(q[n * blk:(n + 1) * blk], kvh * group + g) for g in range(group)],
                                 axis=0)
            sink = jnp.concatenate([jnp.full((blk, 1), sink_ref[kvh * group + g], F32)
                                    for g in range(group)], axis=0)
            s_loc = jnp.where(ok, _bdot_nt(qs, _head(k[lo:hi], kvh)), MASK_VALUE)
            s_ctx = _bdot_nt(qs, _head(ck_all, kvh))
            o = _softmax_av([s_loc, s_ctx], [_head(v[lo:hi], kvh), _head(cv_all, kvh)], sink=sink)
            for g in range(group):
                heads[kvh * group + g] = o[g * blk:(g + 1) * blk]
        o_ref[n * blk:(n + 1) * blk, :] = jnp.concatenate(heads, axis=-1)


def _rope_tables(t_len):
    t = jnp.arange(t_len)
    half = HEAD_DIM // 2
    nf = half // 2
    inv = 1.0 / (ROPE_BASE ** (jnp.arange(nf, dtype=F32) / nf))
    cols = []
    for pos in ((t // GRID_W).astype(F32), (t % GRID_W).astype(F32)):
        ang = pos[:, None] * inv[None, :]
        cols.append((jnp.cos(ang), jnp.sin(ang)))
    cos = jnp.concatenate([cols[0][0], cols[0][0], cols[1][0], cols[1][0]], axis=-1)
    sin = jnp.concatenate([-cols[0][1], cols[0][1], -cols[1][1], cols[1][1]], axis=-1)
    return jnp.tile(cos, (1, N_HEADS)), jnp.tile(sin, (1, N_HEADS))


def _swa_latent(p_swa, cache, sink, rope, layer, n_seq, t_len, tile0):
    kv_w = SWA_KV_HEADS * HEAD_DIM
    return pl.pallas_call(
        functools.partial(_swa_latent_kernel, t_len=t_len),
        grid=(n_seq,),
        in_specs=[pl.BlockSpec(memory_space=pltpu.SMEM),
                  pl.BlockSpec((t_len, SWA_COLS), lambda i: (tile0 + i, 0)),
                  pl.BlockSpec((None, None, 2, cache.shape[3], kv_w), lambda i: (i, layer, 0, 0, 0)),
                  pl.BlockSpec((t_len, MIX_W), lambda i: (0, 0)),
                  pl.BlockSpec((t_len, MIX_W), lambda i: (0, 0))],
        out_specs=pl.BlockSpec((t_len, MIX_W), lambda i: (i, 0)),
        out_shape=jax.ShapeDtypeStruct((n_seq * t_len, MIX_W), F32),
        compiler_params=_params("parallel"),
        name="swa_latent",
    )(sink, p_swa, cache, rope[0], rope[1])


LAT_TILE = 256
LAT_SCAN_STEPS = 64


def _exact_row_select(sel, x):
    hi = x.astype(BF16)
    r1 = x - hi.astype(F32)
    mid = r1.astype(BF16)
    lo = (r1 - mid.astype(F32)).astype(BF16)
    dot = lambda a: jnp.dot(sel, a, preferred_element_type=F32)
    return dot(hi) + dot(mid) + dot(lo)


def _latent_perms(n_seq):
    n_chain = 2 * n_seq * N_HEADS
    vs_n = HEAD_DIM // (LANES // n_chain)
    rows = np.zeros((2, n_seq, MIX_W, HEAD_DIM * n_chain), np.float32)
    vals = np.zeros((2, n_seq, MIX_W, HEAD_DIM * n_chain), np.float32)
    for d in range(2):
        for b in range(n_seq):
            for h in range(N_HEADS):
                c = (d * n_seq + b) * N_HEADS + h
                for e in range(HEAD_DIM):
                    rows[d, b, h * HEAD_DIM + e, e * n_chain + c] = 1.0
                    vals[d, b, h * HEAD_DIM + e, (e % vs_n) * LANES + (e // vs_n) * n_chain + c] = 1.0
    rev = np.eye(LAT_TILE, dtype=np.float32)[::-1]
    return (jnp.asarray(rows, dtype=BF16), jnp.asarray(vals, dtype=BF16),
            jnp.asarray(vals.transpose(0, 1, 3, 2), dtype=BF16), jnp.asarray(rev, dtype=BF16))


def _scan_order_tile(pf_ref, pb_ref, hf_ref, hb_ref, rev_ref, d, b):
    first = pl.program_id(0) == 0
    if d == 0:
        x, halo = pf_ref[b], hf_ref[b, SUBLANES - 1:SUBLANES, :]
    else:
        x, halo = _exact_row_select(rev_ref[...], pb_ref[b]), hb_ref[b, 0:1, :]
    return x, jnp.where(first, 0.0, halo)


def _lat_rwkv_pre_kernel(pf_ref, pb_ref, hf_ref, hb_ref, rev_ref, prow_ref, pval_ref, mu_rkv_ref, mu_lo_ref,
                         wl_ref, b0_ref, kk_ref, ka_ref, rk_ref, seg_ref, z_ref, v_ref, bn_ref, *, n_seq):
    seg = seg_ref[...]
    row0 = lax.broadcasted_iota(jnp.int32, (LAT_TILE, 1), 0) == 0
    acc = [None] * 7
    for d in range(2):
        for b in range(n_seq):
            x, halo = _scan_order_tile(pf_ref, pb_ref, hf_ref, hb_ref, rev_ref, d, b)
            prev = jnp.where(row0, halo, pltpu.roll(x, 1, 0))
            rkv, rkv_prev = x[:, 0:3 * MIX_W], prev[:, 0:3 * MIX_W]
            rkv = rkv + (rkv_prev - rkv) * mu_rkv_ref[d]
            r, k, v = rkv[:, 0:MIX_W], rkv[:, MIX_W:2 * MIX_W], rkv[:, 2 * MIX_W:3 * MIX_W]
            c0 = 3 * MIX_W + GATE_RANK + 2 * LORA_RANK * d
            lo, lo_prev = x[:, c0:c0 + 2 * LORA_RANK], prev[:, c0:c0 + 2 * LORA_RANK]
            lo = lo + (lo_prev - lo) * mu_lo_ref[d]
            lane = lax.broadcasted_iota(jnp.int32, lo.shape, 1)
            lo = jnp.where(lane < LORA_RANK, jnp.tanh(lo), lo)
            pre = _bdot(lo, wl_ref[d]) + b0_ref[d]
            w = jnp.exp(-RWKV_DECAY_SCALE * _sigmoid(pre[:, 0:MIX_W]))
            a = _sigmoid(pre[:, MIX_W:2 * MIX_W])
            kk = k * kk_ref[...]
            kk = kk / jnp.maximum(jnp.sqrt(_seg_sum(kk * kk, seg)), 1e-12)
            k = k * (1.0 + (a - 1.0) * ka_ref[...])
            bonus = _seg_sum(r * k * rk_ref[...], seg) * v
            placed = [_exact_select_dot(arr, prow_ref[d, b]) for arr in (-kk, w, kk * a, k, r)]
            placed += [_exact_select_dot(arr, pval_ref[d, b]) for arr in (v, bonus)]
            acc = [p if s is None else s + p for s, p in zip(acc, placed)]
    for j in range(5):
        z_ref[j] = acc[j]
    v_ref[...] = acc[5]
    bn_ref[...] = acc[6]


def _lat_tile_specs(n_seq, t_len, width):
    n_tiles = t_len // LAT_TILE
    per = LAT_TILE // SUBLANES
    return [pl.BlockSpec((n_seq, LAT_TILE, width), lambda i: (0, i, 0)),
            pl.BlockSpec((n_seq, LAT_TILE, width), lambda i: (0, n_tiles - 1 - i, 0)),
            pl.BlockSpec((n_seq, SUBLANES, width), lambda i: (0, jnp.maximum(i * per - 1, 0), 0)),
            pl.BlockSpec((n_seq, SUBLANES, width),
                         lambda i: (0, jnp.minimum((n_tiles - i) * per, n_tiles * per - 1), 0))]


def _lat_rwkv_pre(p_rw, lw, seg, perms, n_seq, t_len):
    prow, pval, _, rev = perms
    width = prow.shape[-1]
    p3 = p_rw.reshape(n_seq, t_len, RW_COLS)
    whole = lambda a: pl.BlockSpec(a.shape, lambda i: (0,) * a.ndim)
    params = [lw["mu_rkv"], lw["mu_lo"], lw["w_lora"], lw["b_lora"], lw["rw_kk"], lw["rw_ka"], lw["rw_rk"], seg]
    return pl.pallas_call(
        functools.partial(_lat_rwkv_pre_kernel, n_seq=n_seq),
        grid=(t_len // LAT_TILE,),
        in_specs=_lat_tile_specs(n_seq, t_len, RW_COLS) + [whole(a) for a in (rev, prow, pval, *params)],
        out_specs=[pl.BlockSpec((5, LAT_TILE, width), lambda i: (0, i, 0)),
                   pl.BlockSpec((LAT_TILE, width), lambda i: (i, 0)),
                   pl.BlockSpec((LAT_TILE, width), lambda i: (i, 0))],
        out_shape=[jax.ShapeDtypeStruct((5, t_len, width), F32),
                   jax.ShapeDtypeStruct((t_len, width), F32),
                   jax.ShapeDtypeStruct((t_len, width), F32)],
        compiler_params=_params("parallel"),
        name="lat_rwkv_pre",
    )(p3, p3, p3, p3, rev, prow, pval, *params)


def _unplace(yf_ref, yb_ref, rev_ref, pinv_ref, b):
    fwd = _exact_select_dot(yf_ref[...], pinv_ref[0, b])
    bwd = _exact_select_dot(yb_ref[...], pinv_ref[1, b])
    return fwd + _exact_row_select(rev_ref[...], bwd)


def _lat_rwkv_post_kernel(yf_ref, yb_ref, bf_ref, bb_ref, rev_ref, pinv_ref, gh_ref, g2_ref, lnw_ref, lnb_ref,
                          seg_ref, o_ref):
    b = pl.program_id(0)
    seg = seg_ref[...]
    y = _unplace(yf_ref, yb_ref, rev_ref, pinv_ref, b)
    mu = _seg_sum(y, seg) * (1.0 / HEAD_DIM)
    yc = y - mu
    var = _seg_sum(yc * yc, seg) * (1.0 / HEAD_DIM)
    y = yc * lax.rsqrt(var + RWKV_LN_EPS) * lnw_ref[...] + lnb_ref[...]
    y = y + _unplace(bf_ref, bb_ref, rev_ref, pinv_ref, b)
    o_ref[...] = y * _bdot(_sigmoid(gh_ref[...]), g2_ref[...])


def _lat_post_specs(t_len, width):
    n_tiles = t_len // LAT_TILE
    return [pl.BlockSpec((LAT_TILE, width), lambda b, i: (i, 0)),
            pl.BlockSpec((LAT_TILE, width), lambda b, i: (n_tiles - 1 - i, 0))]


def _lat_rwkv_post(y, bonus, p_rw, lw, seg, perms, n_seq, t_len):
    _, _, pinv, rev = perms
    width = y.shape[-1]
    n_tiles = t_len // LAT_TILE
    whole = lambda a: pl.BlockSpec(a.shape, lambda b, i: (0,) * a.ndim)
    params = [lw["rw_g2"], lw["rw_lnx_w"], lw["rw_lnx_b"], seg]
    return pl.pallas_call(
        _lat_rwkv_post_kernel,
        grid=(n_seq, n_tiles),
        in_specs=_lat_post_specs(t_len, width) * 2 + [whole(rev), whole(pinv),
                 pl.BlockSpec((LAT_TILE, GATE_RANK), lambda b, i: (b * n_tiles + i, 3 * MIX_W // GATE_RANK))]
        + [whole(a) for a in params],
        out_specs=pl.BlockSpec((LAT_TILE, MIX_W), lambda b, i: (b * n_tiles + i, 0)),
        out_shape=jax.ShapeDtypeStruct((n_seq * t_len, MIX_W), F32),
        compiler_params=_params("parallel", "parallel"),
        name="lat_rwkv_post",
    )(y, y, bonus, bonus, rev, pinv, p_rw, *params)


def _lat_hgrn_pre_kernel(pf_ref, pb_ref, rev_ref, prow_ref, pval_ref, lb_ref, z_ref, v_ref, *, n_seq):
    acc = [None] * 4
    for d in range(2):
        for b in range(n_seq):
            x = pf_ref[b] if d == 0 else _exact_row_select(rev_ref[...], pb_ref[b])
            q = x[:, 0:MIX_W]
            f_raw = x[:, (3 + d) * MIX_W:(4 + d) * MIX_W]
            lb = lb_ref[d]
            f = lb + (1.0 - lb) * _sigmoid(f_raw)
            kv = (1.0 - lb) * _sigmoid(-f_raw)
            placed = [_exact_select_dot(arr, prow_ref[d, b]) for arr in (f, kv, q * _sigmoid(q))]
            placed.append(_exact_select_dot(x[:, MIX_W:2 * MIX_W], pval_ref[d, b]))
            acc = [p if s is None else s + p for s, p in zip(acc, placed)]
    for j in range(3):
        z_ref[j] = acc[j]
    v_ref[...] = acc[3]


def _lat_hgrn_pre(p_hg, lb, perms, n_seq, t_len):
    prow, pval, _, rev = perms
    width = prow.shape[-1]
    p3 = p_hg.reshape(n_seq, t_len, HG_COLS)
    whole = lambda a: pl.BlockSpec(a.shape, lambda i: (0,) * a.ndim)
    return pl.pallas_call(
        functools.partial(_lat_hgrn_pre_kernel, n_seq=n_seq),
        grid=(t_len // LAT_TILE,),
        in_specs=_lat_tile_specs(n_seq, t_len, HG_COLS)[:2] + [whole(a) for a in (rev, prow, pval, lb)],
        out_specs=[pl.BlockSpec((3, LAT_TILE, width), lambda i: (0, i, 0)),
                   pl.BlockSpec((LAT_TILE, width), lambda i: (i, 0))],
        out_shape=[jax.ShapeDtypeStruct((3, t_len, width), F32),
                   jax.ShapeDtypeStruct((t_len, width), F32)],
        compiler_params=_params("parallel"),
        name="lat_hgrn_pre",
    )(p3, p3, rev, prow, pval, lb)


def _lat_hgrn_post_kernel(of_ref, ob_ref, rev_ref, pinv_ref, g_ref, nw_ref, seg_ref, out_ref):
    o = _unplace(of_ref, ob_ref, rev_ref, pinv_ref, pl.program_id(0))
    ms = _seg_sum(o * o, seg_ref[...]) * (1.0 / HEAD_DIM)
    out_ref[...] = o * lax.rsqrt(ms + NORM_EPS) * nw_ref[...] * _sigmoid(g_ref[...])


def _lat_hgrn_post(o, p_hg, hg_norm, seg, perms, n_seq, t_len):
    _, _, pinv, rev = perms
    width = o.shape[-1]
    n_tiles = t_len // LAT_TILE
    whole = lambda a: pl.BlockSpec(a.shape, lambda b, i: (0,) * a.ndim)
    return pl.pallas_call(
        _lat_hgrn_post_kernel,
        grid=(n_seq, n_tiles),
        in_specs=_lat_post_specs(t_len, width) + [whole(rev), whole(pinv),
                 pl.BlockSpec((LAT_TILE, MIX_W), lambda b, i: (b * n_tiles + i, 2)),
                 whole(hg_norm), whole(seg)],
        out_specs=pl.BlockSpec((LAT_TILE, MIX_W), lambda b, i: (b * n_tiles + i, 0)),
        out_shape=jax.ShapeDtypeStruct((n_seq * t_len, MIX_W), F32),
        compiler_params=_params("parallel", "parallel"),
        name="lat_hgrn_post",
    )(o, o, rev, pinv, p_hg, hg_norm, seg)


def _lat_recurrent_mixers(p_rw, p_hg, lw, seg, perms, n_seq, t_len, state_rw, state_hg):
    z, v, bonus = _lat_rwkv_pre(p_rw, lw, seg, perms, n_seq, t_len)
    y, _ = _scan(z[None], v[None], _state_to_scan(state_rw, True), delta=True, steps=LAT_SCAN_STEPS)
    ya = _lat_rwkv_post(y.reshape(v.shape), bonus, p_rw, lw, seg, perms, n_seq, t_len)
    z, v = _lat_hgrn_pre(p_hg, lw["hg_lb"], perms, n_seq, t_len)
    o, _ = _scan(z[None], v[None], _state_to_scan(state_hg, False), delta=False, steps=LAT_SCAN_STEPS)
    yc = _lat_hgrn_post(o.reshape(v.shape), p_hg, lw["hg_norm"], seg, perms, n_seq, t_len)
    return ya, yc


def _recurrent_mixers(p_rw, p_hg, lw, seg, n_seq, t_len, tile0, states):
    ctx = states is None
    seq0 = tile0 * ROW_TILE // t_len
    n_tok = n_seq * t_len
    steps = 32

    scan_in, bonus = _rwkv_pre(p_rw, lw, seg, n_seq, t_len, seq0)
    rows, v = _to_scan_layout(scan_in, n_seq, t_len, ctx)
    s0 = jnp.zeros((2, HEAD_DIM, HEAD_DIM, LANES), F32) if ctx else _state_to_scan(states[0], True)
    y, s_rw = _scan(rows, v, s0, delta=True, steps=steps)
    ya = _rwkv_post(_from_scan_layout(y, n_seq, t_len, ctx), bonus, p_rw, lw, seg, tile0)

    scan_in = _hgrn_pre(p_hg, lw["hg_lb"], n_tok, tile0)
    rows, v = _to_scan_layout(scan_in, n_seq, t_len, ctx)
    s0 = jnp.zeros((2, HEAD_DIM, HEAD_DIM, LANES), F32) if ctx else _state_to_scan(states[1], False)
    o, s_hg = _scan(rows, v, s0, delta=False, steps=steps)
    yc = _hgrn_post(_from_scan_layout(o, n_seq, t_len, ctx), p_hg, lw["hg_norm"], seg, tile0)
    return ya, yc, s_rw, s_hg


def kernel(x_prompt, x_sample, cache_nat_kv, cache_swa_kv, state_rwkv, state_hgrn, c, c_ctx, norm_g, mod_w, mod_b, w_in, w_out, rw_mu_rkv, rw_mu_lora, rw_w0, rw_w2, rw_a0, rw_a2, rw_g2, rw_kk, rw_ka, rw_rk, rw_lnx_w, rw_lnx_b, nat_rpb, hg_lb_logits, hg_norm, swa_sink, ffn_w1, ffn_w2):
    n_ctx, t_ctx, _ = x_prompt.shape
    n_lat, t_lat, _ = x_sample.shape
    past = cache_nat_kv.shape[3]

    cond = jnp.zeros((MOD_ROWS, D_MODEL), F32).at[0].set(c_ctx).at[1:1 + n_lat].set(c)
    mod = _modulation_all(cond, mod_w, mod_b)

    w_in_bf, w_out_bf = w_in.astype(BF16), w_out.astype(BF16)
    w1_bf, w2_bf = ffn_w1.astype(BF16), ffn_w2.astype(BF16)
    cuts = np.cumsum((0,) + SLABS)
    lat_slabs = tuple(w_in_bf[:, :, cuts[j]:cuts[j + 1]] for j in range(4))
    ctx_modes = ("fm", "seq", "fm", "seq")
    ctx_y_modes = ("tok", "seq", "tok", "seq")
    lat_modes = ("tok",) * 4
    ctx_slabs = tuple(w.transpose(0, 2, 1) if m == "fm" else w for w, m in zip(lat_slabs, ctx_modes))

    lb_sm = jax.nn.softmax(hg_lb_logits.astype(F32), axis=1)
    hg_lb = jnp.cumsum(lb_sm, axis=1) - lb_sm[:, :1]

    head_of = np.arange(MIX_W) // HEAD_DIM
    seg = jnp.asarray((head_of[:, None] == head_of[None, :]).astype(np.float32))
    rope = _rope_tables(t_lat)
    lat_perms = _latent_perms(n_lat)
    zeros_lora = jnp.zeros((LORA_RANK, MIX_W), F32)
    cache_nat = cache_nat_kv.reshape(n_lat, DEPTH, 2, past, MIX_W)
    cache_swa = cache_swa_kv.reshape(n_lat, DEPTH, 2, past, SWA_KV_HEADS * HEAD_DIM)

    xp = x_prompt.transpose(1, 0, 2).reshape(t_ctx * n_ctx, D_MODEL)
    xs = x_sample.reshape(n_lat * t_lat, D_MODEL)
    ctx_row, lat_row = _mod_row(t_ctx, 0), _mod_row(t_lat, 1)
    nat_out, swa_out, rw_out, hg_out = [], [], [], []
    for l in range(DEPTH):
        lw = {
            "mu_rkv": rw_mu_rkv[l].reshape(2, 1, 3 * MIX_W),
            "mu_lo": rw_mu_lora[l].reshape(2, 1, 2 * LORA_RANK),
            "w_lora": jnp.stack([jnp.concatenate(
                [jnp.concatenate([rw_w2[l, d], zeros_lora], axis=1),
                 jnp.concatenate([zeros_lora, rw_a2[l, d]], axis=1)], axis=0) for d in range(2)]),
            "b_lora": jnp.concatenate([rw_w0[l], rw_a0[l]], axis=-1).reshape(2, 1, 2 * MIX_W),
            "rw_kk": rw_kk[l].reshape(1, MIX_W), "rw_ka": rw_ka[l].reshape(1, MIX_W),
            "rw_rk": rw_rk[l].reshape(1, MIX_W), "rw_g2": rw_g2[l],
            "rw_lnx_w": rw_lnx_w[l].reshape(1, MIX_W), "rw_lnx_b": rw_lnx_b[l].reshape(1, MIX_W),
            "hg_lb": hg_lb[:, l].reshape(2, 1, MIX_W), "hg_norm": hg_norm[l].reshape(1, MIX_W),
            "mu_rkv_col": rw_mu_rkv[l].reshape(2, 3 * MIX_W, 1), "mu_lo_col": rw_mu_lora[l].reshape(2, 2 * LORA_RANK, 1),
            "w2t": rw_w2[l].transpose(0, 2, 1), "a2t": rw_a2[l].transpose(0, 2, 1),
            "w0_col": rw_w0[l].reshape(2, MIX_W, 1), "a0_col": rw_a0[l].reshape(2, MIX_W, 1),
            "kk_col": rw_kk[l].reshape(MIX_W, 1), "ka_col": rw_ka[l].reshape(MIX_W, 1),
            "rk_col": rw_rk[l].reshape(MIX_W, 1), "g2t": rw_g2[l].T,
            "lnw_col": rw_lnx_w[l].reshape(MIX_W, 1), "lnb_col": rw_lnx_b[l].reshape(MIX_W, 1),
            "hg_lb_col": hg_lb[:, l].reshape(2, MIX_W, 1), "hg_norm_col": hg_norm[l].reshape(MIX_W, 1),
        }
        sink = swa_sink[l]

        pt_rw, p_nat, pt_hg, p_swa = _in_projection(xp, norm_g, mod, ctx_slabs, l, ctx_row, ctx_modes, n_ctx)
        ya, yc, s_rw, s_hg = _ctx_recurrent_mixers(pt_rw, pt_hg, lw, n_ctx, t_ctx)
        yb, cnat = _ctx_attention(p_nat, sink, N_HEADS, False)
        yd, cswa = _ctx_attention(p_swa, sink, SWA_KV_HEADS, True)
        xp = _out_projection_ffn(xp, (ya, yb, yc, yd), norm_g, mod, w_out_bf, w1_bf, w2_bf, l, ctx_row,
                                 ctx_y_modes, n_ctx)
        nat_out.append(cnat.reshape(n_ctx, 2, t_ctx, N_HEADS, HEAD_DIM))
        swa_out.append(cswa.reshape(n_ctx, 2, t_ctx, SWA_KV_HEADS, HEAD_DIM))
        rw_out.append(_state_from_scan(s_rw, n_ctx, True))
        hg_out.append(_state_from_scan(s_hg, n_ctx, False))

        p_rw, p_nat, p_hg, p_swa = _in_projection(xs, norm_g, mod, lat_slabs, l, lat_row, lat_modes)
        ya, yc = _lat_recurrent_mixers(p_rw, p_hg, lw, seg, lat_perms, n_lat, t_lat,
                                       state_rwkv[:, l], state_hgrn[:, l])
        yb = _nat_latent(p_nat, cache_nat, _nat_bias_table(nat_rpb[l]), l, n_lat, t_lat, 0)
        yd = _swa_latent(p_swa, cache_swa, sink, rope, l, n_lat, t_lat, 0)
        xs = _out_projection_ffn(xs, (ya, yb, yc, yd), norm_g, mod, w_out_bf, w1_bf, w2_bf, l, lat_row,
                                 lat_modes)

    return (xp.reshape(t_ctx, n_ctx, D_MODEL).transpose(1, 0, 2), xs.reshape(x_sample.shape),
            jnp.stack(nat_out, axis=1), jnp.stack(swa_out, axis=1),
            jnp.stack(rw_out, axis=1), jnp.stack(hg_out, axis=1))
```

```python
import functools

import numpy as np
import jax
import jax.numpy as jnp
from jax import lax
from jax.experimental import pallas as pl
from jax.experimental.pallas import tpu as pltpu

F32 = jnp.float32
BF16 = jnp.bfloat16
HIGHEST = lax.Precision.HIGHEST

D_MODEL = 1024
DEPTH = 4
GRID_W = 64
HEAD_DIM = 64
N_HEADS = 4
MIX_W = N_HEADS * HEAD_DIM
SWA_KV_HEADS = 2
LORA_RANK = 64
GATE_RANK = 128
RWKV_DECAY_SCALE = 0.6065306597126334
RWKV_LN_EPS = 64e-5
NAT_KH = 8
NAT_KW = 16
SWA_WINDOW = 128
SWA_BLOCK = 128
ROPE_BASE = 10000.0
FFN_HIDDEN = 4 * D_MODEL
NORM_EPS = 1e-6
MASK_VALUE = -1e30
N_MOD = 6
ATTN_SCALE = HEAD_DIM ** -0.5

RW_COLS = 3 * MIX_W + GATE_RANK + 4 * LORA_RANK
NAT_COLS = 3 * MIX_W
HG_COLS = 5 * MIX_W
SWA_COLS = MIX_W + 2 * SWA_KV_HEADS * HEAD_DIM
SLABS = (RW_COLS, NAT_COLS, HG_COLS, SWA_COLS)
D_IN = sum(SLABS)

LANES = 128
SUBLANES = 8
ROW_TILE = 256
FFN_ROW_TILE = 512
MOD_ROWS = 8
SCAN_KEY_CHUNK = 32


def _sigmoid(x):
    return 1.0 / (1.0 + jnp.exp(-x))


def _rms_rows(x, g):
    ms = jnp.mean(x * x, axis=-1, keepdims=True)
    return x * lax.rsqrt(ms + NORM_EPS) * g


def _bdot(a, b):
    return jnp.dot(a.astype(BF16), b.astype(BF16), preferred_element_type=F32)


def _bdot_nt(a, b):
    return lax.dot_general(a.astype(BF16), b.astype(BF16), (((1,), (1,)), ((), ())),
                           preferred_element_type=F32)


def _seg_sum(x, seg):
    return jnp.dot(x, seg, preferred_element_type=F32, precision=HIGHEST)


def _exact_select_dot(x, sel):
    return _select_dot3(_split3(x), sel)


def _split3(x):
    hi = x.astype(BF16)
    r1 = x - hi.astype(F32)
    mid = r1.astype(BF16)
    lo = (r1 - mid.astype(F32)).astype(BF16)
    return hi, mid, lo


def _select_dot3(parts, sel):
    hi, mid, lo = (jnp.dot(a, sel, preferred_element_type=F32) for a in parts)
    return hi + mid + lo


def _params(*sem):
    return pltpu.CompilerParams(dimension_semantics=sem)


def _resident(shape, index_map):
    return pl.BlockSpec(shape, index_map, pipeline_mode=pl.Buffered(1))


def _mod_kernel(cond_ref, w_ref, b_ref, o_ref):
    cnd = cond_ref[...]
    s = cnd * _sigmoid(cnd)
    o_ref[...] = _bdot(s, w_ref[...]) + b_ref[...]


def _modulation_all(cond, mod_w, mod_b):
    tn = 1536
    n = N_MOD * D_MODEL
    return pl.pallas_call(
        _mod_kernel,
        grid=(DEPTH, n // tn),
        in_specs=[pl.BlockSpec((MOD_ROWS, D_MODEL), lambda l, j: (0, 0)),
                  pl.BlockSpec((None, D_MODEL, tn), lambda l, j: (l, 0, j)),
                  pl.BlockSpec((None, 1, tn), lambda l, j: (l, 0, j))],
        out_specs=pl.BlockSpec((None, MOD_ROWS, tn), lambda l, j: (l, 0, j)),
        out_shape=jax.ShapeDtypeStruct((DEPTH, MOD_ROWS, n), F32),
        compiler_params=_params("parallel", "parallel"),
        name="modulation",
    )(cond, mod_w, mod_b.reshape(DEPTH, 1, n))


def _mod_row(seq_len, first_row):
    if first_row == 0:
        return lambda start: 0
    return lambda start: first_row + start // seq_len


SEQ_TILE_STEPS = SUBLANES


def _tile_perm(n_seq, to_seq_major):
    r = np.arange(ROW_TILE)
    src = (r % SEQ_TILE_STEPS) * n_seq + r // SEQ_TILE_STEPS
    p = (src[:, None] == r[None, :])
    return jnp.asarray(p if to_seq_major else p.T, dtype=BF16)


def _permute_rows(perm_ref, xb):
    return jnp.dot(perm_ref[...], xb, preferred_element_type=F32).astype(BF16)


def _inproj_kernel(*refs, row_fn, modes):
    x_ref, g_ref, mod_ref = refs[0:3]
    w_refs = refs[3:7]
    perm_ref = refs[7] if "seq" in modes else None
    o_refs = refs[-4:]
    row = row_fn(pl.program_id(0) * ROW_TILE)
    shift = mod_ref[pl.ds(row, 1), 0:D_MODEL]
    scale = mod_ref[pl.ds(row, 1), D_MODEL:2 * D_MODEL]
    h = _rms_rows(x_ref[...], g_ref[0:1, :]) * (1.0 + scale) + shift
    hb = h.astype(BF16)
    hb_seq = _permute_rows(perm_ref, hb) if perm_ref is not None else None
    for w_ref, o_ref, mode in zip(w_refs, o_refs, modes):
        if mode == "fm":
            o_ref[...] = lax.dot_general(w_ref[...], hb, (((1,), (1,)), ((), ())),
                                         preferred_element_type=F32)
        elif mode == "seq":
            o_ref[...] = jnp.dot(hb_seq, w_ref[...], preferred_element_type=F32).reshape(o_ref.shape)
        else:
            o_ref[...] = jnp.dot(hb, w_ref[...], preferred_element_type=F32)


def _in_projection(x, norm_g, mod, w_slabs, layer, row_fn, modes, n_seq=None):
    n = x.shape[0]
    out_specs, out_shape = [], []
    for width, mode in zip(SLABS, modes):
        if mode == "fm":
            out_specs.append(pl.BlockSpec((width, ROW_TILE), lambda i: (0, i)))
            out_shape.append(jax.ShapeDtypeStruct((width, n), F32))
        elif mode == "seq":
            out_specs.append(pl.BlockSpec((n_seq, SEQ_TILE_STEPS, width), lambda i: (0, i, 0)))
            out_shape.append(jax.ShapeDtypeStruct((n_seq, n // n_seq, width), F32))
        else:
            out_specs.append(pl.BlockSpec((ROW_TILE, width), lambda i: (i, 0)))
            out_shape.append(jax.ShapeDtypeStruct((n, width), F32))
    in_specs = [pl.BlockSpec((ROW_TILE, D_MODEL), lambda i: (i, 0)),
                pl.BlockSpec((None, 4, D_MODEL), lambda i: (layer, 0, 0)),
                pl.BlockSpec((None, MOD_ROWS, N_MOD * D_MODEL), lambda i: (layer, 0, 0))]
    in_specs += [_resident((None,) + w.shape[1:], lambda i: (layer, 0, 0)) for w in w_slabs]
    args = [x, norm_g, mod, *w_slabs]
    if "seq" in modes:
        assert n_seq * SEQ_TILE_STEPS == ROW_TILE
        in_specs.append(pl.BlockSpec((ROW_TILE, ROW_TILE), lambda i: (0, 0)))
        args.append(_tile_perm(n_seq, True))
    return pl.pallas_call(
        functools.partial(_inproj_kernel, row_fn=row_fn, modes=modes),
        grid=(n // ROW_TILE,),
        in_specs=in_specs,
        out_specs=out_specs,
        out_shape=out_shape,
        compiler_params=_params("parallel"),
        name="in_projection",
    )(*args)


def _outproj_ffn_kernel(*refs, row_fn, modes):
    x_ref = refs[0]
    y_refs = refs[1:5]
    g_ref, mod_ref, wo_ref, w1_ref, w2_ref = refs[5:10]
    perm_ref = refs[10] if "seq" in modes else None
    o_ref = refs[-1]
    row = row_fn(pl.program_id(0) * FFN_ROW_TILE)
    mod = lambda j: mod_ref[pl.ds(row, 1), j * D_MODEL:(j + 1) * D_MODEL]
    x = x_ref[...]
    acc = None
    for j, (y_ref, mode) in enumerate(zip(y_refs, modes)):
        if mode == "seq":
            yb = jnp.concatenate(
                [_permute_rows(perm_ref, y_ref[:, sub * SEQ_TILE_STEPS:(sub + 1) * SEQ_TILE_STEPS, :]
                               .reshape(ROW_TILE, MIX_W).astype(BF16))
                 for sub in range(FFN_ROW_TILE // ROW_TILE)], axis=0)
        else:
            yb = y_ref[...].astype(BF16)
        part = jnp.dot(yb, wo_ref[j * MIX_W:(j + 1) * MIX_W, :], preferred_element_type=F32)
        acc = part if acc is None else acc + part
    x = x + mod(2) * _rms_rows(acc, g_ref[1:2, :])
    h = (_rms_rows(x, g_ref[2:3, :]) * (1.0 + mod(4)) + mod(3)).astype(BF16)
    f = None
    for j in range(FFN_HIDDEN // D_MODEL):
        u = jnp.dot(h, w1_ref[:, j * D_MODEL:(j + 1) * D_MODEL], preferred_element_type=F32)
        u = jnp.square(jnp.maximum(u, 0.0)).astype(BF16)
        part = jnp.dot(u, w2_ref[j * D_MODEL:(j + 1) * D_MODEL, :], preferred_element_type=F32)
        f = part if f is None else f + part
    o_ref[...] = x + mod(5) * _rms_rows(f, g_ref[3:4, :])


def _out_projection_ffn(x, ys, norm_g, mod, w_out_bf, w1_bf, w2_bf, layer, row_fn, modes, n_seq=None):
    n = x.shape[0]
    sub = FFN_ROW_TILE // ROW_TILE
    tile = lambda w: pl.BlockSpec((FFN_ROW_TILE, w), lambda i: (i, 0))
    y_specs = [pl.BlockSpec((n_seq, sub * SEQ_TILE_STEPS, MIX_W), lambda i: (0, i, 0)) if m == "seq"
               else tile(MIX_W) for m in modes]
    in_specs = [tile(D_MODEL)] + y_specs + [
        pl.BlockSpec((None, 4, D_MODEL), lambda i: (layer, 0, 0)),
        pl.BlockSpec((None, MOD_ROWS, N_MOD * D_MODEL), lambda i: (layer, 0, 0)),
        _resident((None, D_MODEL, D_MODEL), lambda i: (layer, 0, 0)),
        _resident((None, D_MODEL, FFN_HIDDEN), lambda i: (layer, 0, 0)),
        _resident((None, FFN_HIDDEN, D_MODEL), lambda i: (layer, 0, 0))]
    args = [x, *ys, norm_g, mod, w_out_bf, w1_bf, w2_bf]
    if "seq" in modes:
        in_specs.append(pl.BlockSpec((ROW_TILE, ROW_TILE), lambda i: (0, 0)))
        args.append(_tile_perm(n_seq, False))
    return pl.pallas_call(
        functools.partial(_outproj_ffn_kernel, row_fn=row_fn, modes=modes),
        grid=(n // FFN_ROW_TILE,),
        in_specs=in_specs,
        out_specs=tile(D_MODEL),
        out_shape=jax.ShapeDtypeStruct((n, D_MODEL), F32),
        compiler_params=_params("parallel"),
        name="out_projection_ffn",
    )(*args)


def _scan_kernel(*refs, n_rows, delta, steps, v_tiles, packed, reverse_odd):
    if packed:
        rows_ref, v_ref, s0_ref, e_ref, y_ref, sfin_ref, s_scr, x_scr, t_scr, v_scr, y_scr = refs
    else:
        rows_ref, v_ref, s0_ref, y_ref, sfin_ref, s_scr = refs
    tb = pl.program_id(1)

    @pl.when(tb == 0)
    def _():
        s_scr[...] = s0_ref[...]

    if packed:
        per_col = e_ref.shape[0]
        n_col = HEAD_DIM // per_col
        for j in range(n_rows):
            for col in range(n_col):
                t_scr[:, col, :] = rows_ref[j, :, col * LANES:(col + 1) * LANES]
            parts = _split3(t_scr[...].reshape(steps * n_col, LANES))
            for s in range(per_col):
                x_scr[j, :, s * n_col:(s + 1) * n_col, :] = _select_dot3(parts, e_ref[s]).reshape(
                    steps, n_col, LANES)
        for s in range(SUBLANES):
            v_scr[:, s, :] = v_ref[:, s * LANES:(s + 1) * LANES]

    n_part = max(1, 4 // v_tiles)
    tiles = [slice(i * SUBLANES, (i + 1) * SUBLANES) for i in range(v_tiles)]
    if delta:
        i_nkk, i_w, i_kka, i_kv, i_r = range(5)
    else:
        i_w, i_kv, i_r = range(3)

    k_chunk = SCAN_KEY_CHUNK if v_tiles > 1 else HEAD_DIM
    zero = jnp.zeros((SUBLANES, LANES), F32)
    acc0 = tuple(zero for _ in range(v_tiles * n_part))

    def over_keys(body):
        def chunk(c, acc):
            acc = list(acc)
            for kk in range(k_chunk):
                acc = body(c * k_chunk + kk, kk, acc)
            return tuple(acc)
        if k_chunk == HEAD_DIM:
            acc = chunk(0, acc0)
        else:
            acc = lax.fori_loop(0, HEAD_DIM // k_chunk, chunk, acc0)
        sums = []
        for i in range(v_tiles):
            parts = list(acc[i * n_part:(i + 1) * n_part])
            while len(parts) > 1:
                parts = [a + b for a, b in zip(parts[0::2], parts[1::2])]
            sums.append(parts[0])
        return sums

    backward = (pl.program_id(0) % 2 == 1) if reverse_odd else None

    def step(i_step, carry):
        t = jnp.where(backward, steps - 1 - i_step, i_step) if reverse_odd else i_step
        if packed:
            row = lambda j, k: x_scr[j, t, pl.ds(k, 1), :]
            vt = [v_scr[t]]
        else:
            row = lambda j, k: rows_ref[j, t, pl.ds(k, 1), :]
            vt = [v_ref[t, tl, :] for tl in tiles]
        if delta:
            def sa_body(k, kk, acc):
                nk = row(i_nkk, k)
                for i, tl in enumerate(tiles):
                    p = i * n_part + kk % n_part
                    acc[p] = acc[p] + s_scr[k, tl, :] * nk
                return acc
            sa = over_keys(sa_body)

        def update_body(k, kk, acc):
            wk, kvk, rk = row(i_w, k), row(i_kv, k), row(i_r, k)
            if delta:
                kkak = row(i_kka, k)
            for i, tl in enumerate(tiles):
                s = s_scr[k, tl, :] * wk + vt[i] * kvk
                if delta:
                    s = s + sa[i] * kkak
                s_scr[k, tl, :] = s
                p = i * n_part + kk % n_part
                acc[p] = acc[p] + s * rk
            return acc
        y = over_keys(update_body)
        if packed:
            y_scr[t] = y[0]
        else:
            for i, tl in enumerate(tiles):
                y_ref[t, tl, :] = y[i]
        return carry

    lax.fori_loop(0, steps, step, 0)
    if packed:
        for s in range(SUBLANES):
            y_ref[:, s * LANES:(s + 1) * LANES] = y_scr[:, s, :]

    @pl.when(tb == pl.num_programs(1) - 1)
    def _():
        sfin_ref[...] = s_scr[...]


def _scan(rows, v, s0, *, delta, steps, reverse_odd=False):
    groups, _, t_len = rows.shape[:3]
    n_rows = 5 if delta else 3
    n_blocks = t_len // steps
    packed = v is not None
    tblk = (lambda g, t: jnp.where(g % 2 == 1, n_blocks - 1 - t, t)) if reverse_odd else (lambda g, t: t)
    vrows = s0.shape[2]
    state_spec = pl.BlockSpec((None, HEAD_DIM, vrows, LANES), lambda g, t: (g, 0, 0, 0))
    scratch = [pltpu.VMEM((HEAD_DIM, vrows, LANES), F32)]
    if packed:
        width = rows.shape[-1]
        n_chain = width // HEAD_DIM
        per_col = LANES // n_chain
        src = np.arange(LANES)
        e = np.stack([(src[:, None] // n_chain == s) & (src[:, None] % n_chain == src[None, :] % n_chain)
                      for s in range(per_col)])
        assert HEAD_DIM // per_col == SUBLANES and vrows == SUBLANES
        io_spec = pl.BlockSpec((None, steps, width), lambda g, t: (g, tblk(g, t), 0))
        in_specs = [pl.BlockSpec((None, n_rows, steps, width), lambda g, t: (g, 0, tblk(g, t), 0)),
                    io_spec, state_spec,
                    pl.BlockSpec((per_col, LANES, LANES), lambda g, t: (0, 0, 0))]
        s0 = s0.reshape(groups, SUBLANES, per_col, vrows, LANES).transpose(0, 2, 1, 3, 4).reshape(s0.shape)
        args = [rows, v, s0, jnp.asarray(e, dtype=BF16)]
        scratch += [pltpu.VMEM((n_rows, steps, HEAD_DIM, LANES), F32)] + [pltpu.VMEM((steps, SUBLANES, LANES), F32)] * 3
        y_spec, y_shape = io_spec, jax.ShapeDtypeStruct((groups, t_len, width), F32)
    else:
        in_specs = [pl.BlockSpec((None, n_rows, steps, HEAD_DIM, LANES), lambda g, t: (g, 0, tblk(g, t), 0, 0)),
                    pl.BlockSpec((None, None, steps, vrows, LANES), lambda g, t: (g, n_rows, tblk(g, t), 0, 0)),
                    state_spec]
        args = [rows, rows, s0]
        y_spec = pl.BlockSpec((None, steps, vrows, LANES), lambda g, t: (g, tblk(g, t), 0, 0))
        y_shape = jax.ShapeDtypeStruct((groups, t_len, vrows, LANES), F32)
    return pl.pallas_call(
        functools.partial(_scan_kernel, n_rows=n_rows, delta=delta, steps=steps, v_tiles=vrows // SUBLANES,
                          packed=packed, reverse_odd=reverse_odd),
        grid=(groups, n_blocks),
        in_specs=in_specs,
        out_specs=[y_spec, state_spec],
        out_shape=[y_shape, jax.ShapeDtypeStruct((groups, HEAD_DIM, vrows, LANES), F32)],
        scratch_shapes=scratch,
        compiler_params=_params("parallel", "arbitrary"),
        name="rwkv_scan" if delta else "hgrn_scan",
    )(*args)


def _state_to_scan(s, key_last):
    n_seq = s.shape[0]
    vg = LANES // (2 * n_seq * N_HEADS)
    if key_last:
        s = s.reshape(n_seq, 2, N_HEADS, vg, HEAD_DIM // vg, HEAD_DIM).transpose(5, 4, 3, 1, 0, 2)
    else:
        s = s.reshape(n_seq, 2, N_HEADS, HEAD_DIM, vg, HEAD_DIM // vg).transpose(3, 5, 4, 1, 0, 2)
    return s.reshape(1, HEAD_DIM, HEAD_DIM // vg, LANES)


def _state_from_scan(s, n_seq, key_last):
    s = s.reshape(2, HEAD_DIM, HEAD_DIM, N_HEADS, n_seq)
    return s.transpose(4, 0, 3, 2, 1) if key_last else s.transpose(4, 0, 3, 1, 2)


CTX_TILE_STEPS = ROW_TILE // 32
RWKV_MATMUL_REGROUPS = (5, 4)
HGRN_MATMUL_REGROUPS = 2


def _swap_lead4(n_seq):
    i = np.arange(N_HEADS * N_HEADS * n_seq)
    a, b, seq = i // (N_HEADS * n_seq), (i // n_seq) % N_HEADS, i % n_seq
    p = np.zeros((i.size, i.size), np.float32)
    p[i, (b * N_HEADS + a) * n_seq + seq] = 1.0
    return jnp.asarray(p, dtype=BF16)


def _fm_to_scan_rolls(a, n_seq):
    grp = lax.broadcasted_iota(jnp.int32, (HEAD_DIM, LANES), 1) // n_seq
    rolled = []
    for h in range(N_HEADS):
        blk = a[h * HEAD_DIM:(h + 1) * HEAD_DIM]
        rolled.append([blk if s == 0 else pltpu.roll(blk, s * n_seq, 1) for s in range(N_HEADS)])
    tiles = []
    for t in range(N_HEADS):
        out = rolled[0][(0 - t) % N_HEADS]
        for h in range(1, N_HEADS):
            out = jnp.where(grp == h, rolled[h][(h - t) % N_HEADS], out)
        tiles.append(out)
    return tiles


def _store_scan_tiles(scan_ref, d, arrays, swap_ref, n_seq, n_matmul):
    n_col = arrays[0].shape[1] // LANES
    lhs = jnp.concatenate(
        [jnp.concatenate([a[h * HEAD_DIM:(h + 1) * HEAD_DIM, c * LANES:(c + 1) * LANES] for h in range(N_HEADS)],
                         axis=1) for a in arrays[:n_matmul] for c in range(n_col)], axis=0)
    out = _exact_select_dot(lhs, swap_ref[...])
    for j, a in enumerate(arrays):
        for c in range(n_col):
            if j < n_matmul:
                r0 = (j * n_col + c) * HEAD_DIM
                tiles = [out[r0:r0 + HEAD_DIM, t * LANES:(t + 1) * LANES] for t in range(N_HEADS)]
            else:
                tiles = _fm_to_scan_rolls(a[:, c * LANES:(c + 1) * LANES], n_seq)
            for t in range(N_HEADS):
                scan_ref[d, j, c * N_HEADS + t] = tiles[t]


def _load_scan_tiles(y_ref, swap_ref):
    n_col = y_ref.shape[1] // N_HEADS
    lhs = jnp.concatenate(
        [jnp.concatenate([y_ref[0, c * N_HEADS + t] + y_ref[1, c * N_HEADS + t] for t in range(N_HEADS)], axis=1)
         for c in range(n_col)], axis=0)
    out = _exact_select_dot(lhs, swap_ref[...])
    return jnp.concatenate(
        [jnp.concatenate([out[c * HEAD_DIM:(c + 1) * HEAD_DIM, h * LANES:(h + 1) * LANES] for h in range(N_HEADS)],
                         axis=0) for c in range(n_col)], axis=1)


def _head_sum_rows(x):
    w = x.shape[1]
    s = jnp.sum(x.reshape(N_HEADS, HEAD_DIM, w), axis=1, keepdims=True)
    return jnp.broadcast_to(s, (N_HEADS, HEAD_DIM, w)).reshape(N_HEADS * HEAD_DIM, w)


def _lane_halo(halo, shift, valid, width):
    h = jnp.where(valid, pltpu.roll(halo, shift, 1), 0.0)
    return jnp.concatenate([h] * (width // LANES), axis=1)


def _prev_step(x, halo, valid, n_seq):
    width = x.shape[1]
    lane = lax.broadcasted_iota(jnp.int32, (1, width), 1)
    return jnp.where(lane < n_seq, _lane_halo(halo, n_seq, valid, width), pltpu.roll(x, n_seq, 1))


def _next_step(x, halo, valid, n_seq):
    width = x.shape[1]
    lane = lax.broadcasted_iota(jnp.int32, (1, width), 1)
    return jnp.where(lane >= width - n_seq, _lane_halo(halo, LANES - n_seq, valid, width),
                     pltpu.roll(x, width - n_seq, 1))


def _rwkv_pre_fm_kernel(x_ref, hp_ref, hn_ref, mu_rkv_ref, mu_lo_ref, w2t_ref, a2t_ref, w0_ref, a0_ref,
                        kk_ref, ka_ref, rk_ref, swap_ref, scan_ref, bonus_ref, *, n_seq):
    i = pl.program_id(0)
    last = pl.num_programs(0) - 1
    rkv_rows = slice(0, 3 * MIX_W)
    x = x_ref[rkv_rows, :]
    bonus = None
    for d in range(2):
        if d == 0:
            shifted = lambda a, rows: _prev_step(a, hp_ref[rows, :], i > 0, n_seq)
        else:
            shifted = lambda a, rows: _next_step(a, hn_ref[rows, :], i < last, n_seq)
        x3 = x + (shifted(x, rkv_rows) - x) * mu_rkv_ref[d]
        r, k, v = x3[0:MIX_W], x3[MIX_W:2 * MIX_W], x3[2 * MIX_W:3 * MIX_W]
        lo_rows = slice(3 * MIX_W + GATE_RANK + 2 * LORA_RANK * d, 3 * MIX_W + GATE_RANK + 2 * LORA_RANK * (d + 1))
        lo = x_ref[lo_rows, :]
        lo = lo + (shifted(lo, lo_rows) - lo) * mu_lo_ref[d]
        w_pre = _bdot(w2t_ref[d], jnp.tanh(lo[0:LORA_RANK])) + w0_ref[d]
        a_pre = _bdot(a2t_ref[d], lo[LORA_RANK:2 * LORA_RANK]) + a0_ref[d]
        w = jnp.exp(-RWKV_DECAY_SCALE * _sigmoid(w_pre))
        a = _sigmoid(a_pre)
        kk = k * kk_ref[...]
        kk = kk / jnp.maximum(jnp.sqrt(_head_sum_rows(kk * kk)), 1e-12)
        k = k * (1.0 + (a - 1.0) * ka_ref[...])
        _store_scan_tiles(scan_ref, d, (-kk, w, kk * a, k, r, v), swap_ref, n_seq, RWKV_MATMUL_REGROUPS[d])
        b = _head_sum_rows(r * k * rk_ref[...]) * v
        bonus = b if bonus is None else bonus + b
    bonus_ref[...] = bonus


def _rwkv_pre_fm(pt_rw, lw, swap, n_seq, t_len):
    n = pt_rw.shape[1]
    n_tiles = n // ROW_TILE
    per_tile = ROW_TILE // LANES
    col = lambda rows: pl.BlockSpec((rows, 1), lambda i: (0, 0))
    dcol = lambda rows: pl.BlockSpec((2, rows, 1), lambda i: (0, 0, 0))
    return pl.pallas_call(
        functools.partial(_rwkv_pre_fm_kernel, n_seq=n_seq),
        grid=(n_tiles,),
        in_specs=[pl.BlockSpec((RW_COLS, ROW_TILE), lambda i: (0, i)),
                  pl.BlockSpec((RW_COLS, LANES), lambda i: (0, jnp.maximum(i * per_tile - 1, 0))),
                  pl.BlockSpec((RW_COLS, LANES),
                               lambda i: (0, jnp.minimum((i + 1) * per_tile, n_tiles * per_tile - 1))),
                  dcol(3 * MIX_W), dcol(2 * LORA_RANK),
                  pl.BlockSpec((2, MIX_W, LORA_RANK), lambda i: (0, 0, 0)),
                  pl.BlockSpec((2, MIX_W, LORA_RANK), lambda i: (0, 0, 0)),
                  dcol(MIX_W), dcol(MIX_W), col(MIX_W), col(MIX_W), col(MIX_W),
                  pl.BlockSpec(swap.shape, lambda i: (0, 0))],
        out_specs=[pl.BlockSpec((2, 6, CTX_TILE_STEPS, HEAD_DIM, LANES), lambda i: (0, 0, i, 0, 0)),
                   pl.BlockSpec((MIX_W, ROW_TILE), lambda i: (0, i))],
        out_shape=[jax.ShapeDtypeStruct((2, 6, t_len, HEAD_DIM, LANES), F32),
                   jax.ShapeDtypeStruct((MIX_W, n), F32)],
        compiler_params=_params("parallel"),
        name="rwkv_pre_fm",
    )(pt_rw, pt_rw, pt_rw, lw["mu_rkv_col"], lw["mu_lo_col"], lw["w2t"], lw["a2t"], lw["w0_col"],
      lw["a0_col"], lw["kk_col"], lw["ka_col"], lw["rk_col"], swap)


def _rwkv_post_fm_kernel(y_ref, bonus_ref, gh_ref, g2t_ref, lnw_ref, lnb_ref, swap_ref, o_ref):
    y = _load_scan_tiles(y_ref, swap_ref)
    mu = _head_sum_rows(y) * (1.0 / HEAD_DIM)
    yc = y - mu
    var = _head_sum_rows(yc * yc) * (1.0 / HEAD_DIM)
    y = yc * lax.rsqrt(var + RWKV_LN_EPS) * lnw_ref[...] + lnb_ref[...] + bonus_ref[...]
    g = _bdot(g2t_ref[...], _sigmoid(gh_ref[...]))
    o_ref[...] = (y * g).T


def _rwkv_post_fm(y, bonus, pt_rw, lw, swap):
    n = bonus.shape[1]
    col = pl.BlockSpec((MIX_W, 1), lambda i: (0, 0))
    return pl.pallas_call(
        _rwkv_post_fm_kernel,
        grid=(n // ROW_TILE,),
        in_specs=[pl.BlockSpec((2, CTX_TILE_STEPS, HEAD_DIM, LANES), lambda i: (0, i, 0, 0)),
                  pl.BlockSpec((MIX_W, ROW_TILE), lambda i: (0, i)),
                  pl.BlockSpec((GATE_RANK, ROW_TILE), lambda i: (3 * MIX_W // GATE_RANK, i)),
                  pl.BlockSpec((MIX_W, GATE_RANK), lambda i: (0, 0)),
                  col, col, pl.BlockSpec(swap.shape, lambda i: (0, 0))],
        out_specs=pl.BlockSpec((ROW_TILE, MIX_W), lambda i: (i, 0)),
        out_shape=jax.ShapeDtypeStruct((n, MIX_W), F32),
        compiler_params=_params("parallel"),
        name="rwkv_post_fm",
    )(y, bonus, pt_rw, lw["g2t"], lw["lnw_col"], lw["lnb_col"], swap)


def _hgrn_pre_fm_kernel(qv_ref, ff_ref, fb_ref, lb_ref, swap_ref, scan_ref, *, n_seq):
    q = qv_ref[0:MIX_W, :]
    q = q * _sigmoid(q)
    v = qv_ref[MIX_W:2 * MIX_W, :]
    for d, f_ref in enumerate((ff_ref, fb_ref)):
        f_raw = f_ref[...]
        lb = lb_ref[d]
        f = lb + (1.0 - lb) * _sigmoid(f_raw)
        kv = (1.0 - lb) * _sigmoid(-f_raw)
        _store_scan_tiles(scan_ref, d, (f, kv, q, v), swap_ref, n_seq, HGRN_MATMUL_REGROUPS)


def _hgrn_pre_fm(pt_hg, lb_col, swap, n_seq, t_len):
    n = pt_hg.shape[1]
    return pl.pallas_call(
        functools.partial(_hgrn_pre_fm_kernel, n_seq=n_seq),
        grid=(n // ROW_TILE,),
        in_specs=[pl.BlockSpec((2 * MIX_W, ROW_TILE), lambda i: (0, i)),
                  pl.BlockSpec((MIX_W, ROW_TILE), lambda i: (3, i)),
                  pl.BlockSpec((MIX_W, ROW_TILE), lambda i: (4, i)),
                  pl.BlockSpec((2, MIX_W, 1), lambda i: (0, 0, 0)),
                  pl.BlockSpec(swap.shape, lambda i: (0, 0))],
        out_specs=pl.BlockSpec((2, 4, CTX_TILE_STEPS, HEAD_DIM, LANES), lambda i: (0, 0, i, 0, 0)),
        out_shape=jax.ShapeDtypeStruct((2, 4, t_len, HEAD_DIM, LANES), F32),
        compiler_params=_params("parallel"),
        name="hgrn_pre_fm",
    )(pt_hg, pt_hg, pt_hg, lb_col, swap)


def _hgrn_post_fm_kernel(o_ref, g_ref, nw_ref, swap_ref, out_ref):
    o = _load_scan_tiles(o_ref, swap_ref)
    ms = _head_sum_rows(o * o) * (1.0 / HEAD_DIM)
    out_ref[...] = (o * lax.rsqrt(ms + NORM_EPS) * nw_ref[...] * _sigmoid(g_ref[...])).T


def _hgrn_post_fm(o, pt_hg, nw_col, swap):
    n = pt_hg.shape[1]
    return pl.pallas_call(
        _hgrn_post_fm_kernel,
        grid=(n // ROW_TILE,),
        in_specs=[pl.BlockSpec((2, CTX_TILE_STEPS, HEAD_DIM, LANES), lambda i: (0, i, 0, 0)),
                  pl.BlockSpec((MIX_W, ROW_TILE), lambda i: (2, i)),
                  pl.BlockSpec((MIX_W, 1), lambda i: (0, 0)),
                  pl.BlockSpec(swap.shape, lambda i: (0, 0))],
        out_specs=pl.BlockSpec((ROW_TILE, MIX_W), lambda i: (i, 0)),
        out_shape=jax.ShapeDtypeStruct((n, MIX_W), F32),
        compiler_params=_params("parallel"),
        name="hgrn_post_fm",
    )(o, pt_hg, nw_col, swap)


def _ctx_recurrent_mixers(pt_rw, pt_hg, lw, n_seq, t_len):
    assert n_seq * N_HEADS == LANES
    s0 = jnp.zeros((2, HEAD_DIM, HEAD_DIM, LANES), F32)
    swap = _swap_lead4(n_seq)
    scan_in, bonus = _rwkv_pre_fm(pt_rw, lw, swap, n_seq, t_len)
    y, s_rw = _scan(scan_in, None, s0, delta=True, steps=32, reverse_odd=True)
    ya = _rwkv_post_fm(y, bonus, pt_rw, lw, swap)
    scan_in = _hgrn_pre_fm(pt_hg, lw["hg_lb_col"], swap, n_seq, t_len)
    o, s_hg = _scan(scan_in, None, s0, delta=False, steps=32, reverse_odd=True)
    yc = _hgrn_post_fm(o, pt_hg, lw["hg_norm_col"], swap)
    return ya, yc, s_rw, s_hg


def _head(x, h):
    return x[:, h * HEAD_DIM:(h + 1) * HEAD_DIM]


def _softmax_av(scores, values, sink=None):
    m = None
    for s in scores:
        mi = jnp.max(s, axis=-1, keepdims=True)
        m = mi if m is None else jnp.maximum(m, mi)
    if sink is not None:
        m = jnp.maximum(m, sink)
    den = None if sink is None else jnp.exp(sink - m)
    out = None
    for s, v in zip(scores, values):
        e = jnp.exp(s - m)
        di = jnp.sum(e, axis=-1, keepdims=True)
        den = di if den is None else den + di
        oi = _bdot(e, v)
        out = oi if out is None else out + oi
    return out / den


def _ctx_attn_kernel(sink_ref, p_ref, y_ref, cache_ref, *, kv_heads, use_sink):
    group = N_HEADS // kv_heads
    kv_w = kv_heads * HEAD_DIM
    q_all = p_ref[:, 0:MIX_W] * ATTN_SCALE
    k_all = p_ref[:, MIX_W:MIX_W + kv_w]
    v_all = p_ref[:, MIX_W + kv_w:MIX_W + 2 * kv_w]
    cache_ref[0] = k_all
    cache_ref[1] = v_all
    outs = []
    for h in range(N_HEADS):
        kvh = h // group
        outs.append(_softmax_av([_bdot_nt(_head(q_all, h), _head(k_all, kvh))], [_head(v_all, kvh)],
                                sink=sink_ref[h] if use_sink else None))
    y_ref[...] = jnp.concatenate(outs, axis=-1)


def _ctx_attention(p, sink, kv_heads, use_sink):
    n_seq, t_len, cols = p.shape
    kv_w = kv_heads * HEAD_DIM
    return pl.pallas_call(
        functools.partial(_ctx_attn_kernel, kv_heads=kv_heads, use_sink=use_sink),
        grid=(n_seq,),
        in_specs=[pl.BlockSpec(memory_space=pltpu.SMEM),
                  pl.BlockSpec((None, t_len, cols), lambda i: (i, 0, 0))],
        out_specs=[pl.BlockSpec((None, t_len, MIX_W), lambda i: (i, 0, 0)),
                   pl.BlockSpec((None, 2, t_len, kv_w), lambda i: (i, 0, 0, 0))],
        out_shape=[jax.ShapeDtypeStruct((n_seq, t_len, MIX_W), F32),
                   jax.ShapeDtypeStruct((n_seq, 2, t_len, kv_w), F32)],
        compiler_params=_params("parallel"),
        name="ctx_attention",
    )(sink, p)


def _nat_latent_kernel(p_ref, ckv_ref, bias_ref, o_ref, *, rows):
    kh = min(NAT_KH, rows)
    n_loc = kh * GRID_W
    ck_all = ckv_ref[0]
    cv_all = ckv_ref[1]
    for r in range(rows):
        rs = min(max(r - kh // 2, 0), rows - kh)
        roff0 = rs - r + NAT_KH - 1
        q_all = p_ref[r * GRID_W:(r + 1) * GRID_W, 0:MIX_W] * ATTN_SCALE
        k_all = p_ref[rs * GRID_W:rs * GRID_W + n_loc, MIX_W:2 * MIX_W]
        v_all = p_ref[rs * GRID_W:rs * GRID_W + n_loc, 2 * MIX_W:3 * MIX_W]
        outs = []
        for h in range(N_HEADS):
            q = _head(q_all, h)
            bias = jnp.concatenate([bias_ref[h, roff0 + 2 * j] for j in range(kh // 2)], axis=-1)
            s_loc = _bdot_nt(q, _head(k_all, h)) + bias
            outs.append(_softmax_av([s_loc, _bdot_nt(q, _head(ck_all, h))],
                                    [_head(v_all, h), _head(cv_all, h)]))
        o_ref[r * GRID_W:(r + 1) * GRID_W, :] = jnp.concatenate(outs, axis=-1)


def _nat_bias_table(rpb):
    q = np.arange(GRID_W)[:, None]
    kc = np.arange(GRID_W)[None, :]
    win = np.clip(q - NAT_KW // 2, 0, GRID_W - NAT_KW)
    ok = (kc >= win) & (kc < win + NAT_KW)
    coff = np.clip(kc - q, -(NAT_KW - 1), NAT_KW - 1) + NAT_KW - 1
    t = jnp.where(ok[None, None], rpb[:, :, coff], MASK_VALUE)
    return jnp.concatenate([t[:, :-1], t[:, 1:]], axis=-1)


def _nat_latent(p_nat, cache, bias, layer, n_seq, t_len, tile0):
    rows = t_len // GRID_W
    return pl.pallas_call(
        functools.partial(_nat_latent_kernel, rows=rows),
        grid=(n_seq,),
        in_specs=[pl.BlockSpec((t_len, NAT_COLS), lambda i: (tile0 + i, 0)),
                  pl.BlockSpec((None, None, 2, cache.shape[3], MIX_W), lambda i: (i, layer, 0, 0, 0)),
                  pl.BlockSpec(bias.shape, lambda i: (0, 0, 0, 0))],
        out_specs=pl.BlockSpec((t_len, MIX_W), lambda i: (i, 0)),
        out_shape=jax.ShapeDtypeStruct((n_seq * t_len, MIX_W), F32),
        compiler_params=_params("parallel"),
        name="nat_latent",
    )(p_nat, cache, bias)


def _rope(x, cos, sin_signed):
    width = x.shape[-1]
    lane = lax.broadcasted_iota(jnp.int32, x.shape, 1)
    half = HEAD_DIM // 4
    partner = jnp.where(lane % (2 * half) < half, pltpu.roll(x, width - half, 1), pltpu.roll(x, half, 1))
    return x * cos + partner * sin_signed


def _swa_latent_kernel(sink_ref, p_ref, ckv_ref, cos_ref, sin_ref, o_ref, *, t_len):
    hd = HEAD_DIM
    kv_w = SWA_KV_HEADS * hd
    group = N_HEADS // SWA_KV_HEADS
    blk = SWA_BLOCK
    nb = t_len // blk
    q = _rope(p_ref[:, 0:MIX_W], cos_ref[...], sin_ref[...]) * ATTN_SCALE
    k = _rope(p_ref[:, MIX_W:MIX_W + kv_w], cos_ref[:, 0:kv_w], sin_ref[:, 0:kv_w])
    v = p_ref[:, MIX_W + kv_w:MIX_W + 2 * kv_w]
    row = lax.broadcasted_iota(jnp.int32, (group * blk, 1), 0)
    ck_all = ckv_ref[0]
    cv_all = ckv_ref[1]
    for n in range(nb):
        lo = max(n - 1, 0) * blk
        hi = min(n + 2, nb) * blk
        qpos = n * blk + row % blk
        kpos = lo + lax.broadcasted_iota(jnp.int32, (group * blk, hi - lo), 1)
        ok = jnp.abs(kpos - qpos) <= SWA_WINDOW
        heads = [None] * N_HEADS
        for kvh in range(SWA_KV_HEADS):
            qs = jnp.concatenate([_head(q[n * blk:(n + 1) * blk], kvh * group + g) for g in range(group)],
                                 axis=0)
            sink = jnp.concatenate([jnp.full((blk, 1), sink_ref[kvh * group + g], F32)
                                    for g in range(group)], axis=0)
            s_loc = jnp.where(ok, _bdot_nt(qs, _head(k[lo:hi], kvh)), MASK_VALUE)
            s_ctx = _bdot_nt(qs, _head(ck_all, kvh))
            o = _softmax_av([s_loc, s_ctx], [_head(v[lo:hi], kvh), _head(cv_all, kvh)], sink=sink)
            for g in range(group):
                heads[kvh * group + g] = o[g * blk:(g + 1) * blk]
        o_ref[n * blk:(n + 1) * blk, :] = jnp.concatenate(heads, axis=-1)


def _rope_tables(t_len):
    t = jnp.arange(t_len)
    half = HEAD_DIM // 2
    nf = half // 2
    inv = 1.0 / (ROPE_BASE ** (jnp.arange(nf, dtype=F32) / nf))
    cols = []
    for pos in ((t // GRID_W).astype(F32), (t % GRID_W).astype(F32)):
        ang = pos[:, None] * inv[None, :]
        cols.append((jnp.cos(ang), jnp.sin(ang)))
    cos = jnp.concatenate([cols[0][0], cols[0][0], cols[1][0], cols[1][0]], axis=-1)
    sin = jnp.concatenate([-cols[0][1], cols[0][1], -cols[1][1], cols[1][1]], axis=-1)
    return jnp.tile(cos, (1, N_HEADS)), jnp.tile(sin, (1, N_HEADS))


def _swa_latent(p_swa, cache, sink, rope, layer, n_seq, t_len, tile0):
    kv_w = SWA_KV_HEADS * HEAD_DIM
    return pl.pallas_call(
        functools.partial(_swa_latent_kernel, t_len=t_len),
        grid=(n_seq,),
        in_specs=[pl.BlockSpec(memory_space=pltpu.SMEM),
                  pl.BlockSpec((t_len, SWA_COLS), lambda i: (tile0 + i, 0)),
                  pl.BlockSpec((None, None, 2, cache.shape[3], kv_w), lambda i: (i, layer, 0, 0, 0)),
                  pl.BlockSpec((t_len, MIX_W), lambda i: (0, 0)),
                  pl.BlockSpec((t_len, MIX_W), lambda i: (0, 0))],
        out_specs=pl.BlockSpec((t_len, MIX_W), lambda i: (i, 0)),
        out_shape=jax.ShapeDtypeStruct((n_seq * t_len, MIX_W), F32),
        compiler_params=_params("parallel"),
        name="swa_latent",
    )(sink, p_swa, cache, rope[0], rope[1])


LAT_TILE = 256
LAT_SCAN_STEPS = 64


def _exact_row_select(sel, x):
    hi = x.astype(BF16)
    r1 = x - hi.astype(F32)
    mid = r1.astype(BF16)
    lo = (r1 - mid.astype(F32)).astype(BF16)
    dot = lambda a: jnp.dot(sel, a, preferred_element_type=F32)
    return dot(hi) + dot(mid) + dot(lo)


def _latent_perms(n_seq):
    n_chain = 2 * n_seq * N_HEADS
    vs_n = HEAD_DIM // (LANES // n_chain)
    rows = np.zeros((2, n_seq, MIX_W, HEAD_DIM * n_chain), np.float32)
    vals = np.zeros((2, n_seq, MIX_W, HEAD_DIM * n_chain), np.float32)
    for d in range(2):
        for b in range(n_seq):
            for h in range(N_HEADS):
                c = (d * n_seq + b) * N_HEADS + h
                for e in range(HEAD_DIM):
                    rows[d, b, h * HEAD_DIM + e, e * n_chain + c] = 1.0
                    vals[d, b, h * HEAD_DIM + e, (e % vs_n) * LANES + (e // vs_n) * n_chain + c] = 1.0
    rev = np.eye(LAT_TILE, dtype=np.float32)[::-1]
    return (jnp.asarray(rows, dtype=BF16), jnp.asarray(vals, dtype=BF16),
            jnp.asarray(vals.transpose(0, 1, 3, 2), dtype=BF16), jnp.asarray(rev, dtype=BF16))


def _scan_order_tile(pf_ref, pb_ref, hf_ref, hb_ref, rev_ref, d, b):
    first = pl.program_id(0) == 0
    if d == 0:
        x, halo = pf_ref[b], hf_ref[b, SUBLANES - 1:SUBLANES, :]
    else:
        x, halo = _exact_row_select(rev_ref[...], pb_ref[b]), hb_ref[b, 0:1, :]
    return x, jnp.where(first, 0.0, halo)


def _lat_rwkv_pre_kernel(pf_ref, pb_ref, hf_ref, hb_ref, rev_ref, prow_ref, pval_ref, mu_rkv_ref, mu_lo_ref,
                         wl_ref, b0_ref, kk_ref, ka_ref, rk_ref, seg_ref, z_ref, v_ref, bn_ref, *, n_seq):
    seg = seg_ref[...]
    row0 = lax.broadcasted_iota(jnp.int32, (LAT_TILE, 1), 0) == 0
    acc = [None] * 7
    for d in range(2):
        for b in range(n_seq):
            x, halo = _scan_order_tile(pf_ref, pb_ref, hf_ref, hb_ref, rev_ref, d, b)
            prev = jnp.where(row0, halo, pltpu.roll(x, 1, 0))
            rkv, rkv_prev = x[:, 0:3 * MIX_W], prev[:, 0:3 * MIX_W]
            rkv = rkv + (rkv_prev - rkv) * mu_rkv_ref[d]
            r, k, v = rkv[:, 0:MIX_W], rkv[:, MIX_W:2 * MIX_W], rkv[:, 2 * MIX_W:3 * MIX_W]
            c0 = 3 * MIX_W + GATE_RANK + 2 * LORA_RANK * d
            lo, lo_prev = x[:, c0:c0 + 2 * LORA_RANK], prev[:, c0:c0 + 2 * LORA_RANK]
            lo = lo + (lo_prev - lo) * mu_lo_ref[d]
            lane = lax.broadcasted_iota(jnp.int32, lo.shape, 1)
            lo = jnp.where(lane < LORA_RANK, jnp.tanh(lo), lo)
            pre = _bdot(lo, wl_ref[d]) + b0_ref[d]
            w = jnp.exp(-RWKV_DECAY_SCALE * _sigmoid(pre[:, 0:MIX_W]))
            a = _sigmoid(pre[:, MIX_W:2 * MIX_W])
            kk = k * kk_ref[...]
            kk = kk / jnp.maximum(jnp.sqrt(_seg_sum(kk * kk, seg)), 1e-12)
            k = k * (1.0 + (a - 1.0) * ka_ref[...])
            bonus = _seg_sum(r * k * rk_ref[...], seg) * v
            placed = [_exact_select_dot(arr, prow_ref[d, b]) for arr in (-kk, w, kk * a, k, r)]
            placed += [_exact_select_dot(arr, pval_ref[d, b]) for arr in (v, bonus)]
            acc = [p if s is None else s + p for s, p in zip(acc, placed)]
    for j in range(5):
        z_ref[j] = acc[j]
    v_ref[...] = acc[5]
    bn_ref[...] = acc[6]


def _lat_tile_specs(n_seq, t_len, width):
    n_tiles = t_len // LAT_TILE
    per = LAT_TILE // SUBLANES
    return [pl.BlockSpec((n_seq, LAT_TILE, width), lambda i: (0, i, 0)),
            pl.BlockSpec((n_seq, LAT_TILE, width), lambda i: (0, n_tiles - 1 - i, 0)),
            pl.BlockSpec((n_seq, SUBLANES, width), lambda i: (0, jnp.maximum(i * per - 1, 0), 0)),
            pl.BlockSpec((n_seq, SUBLANES, width),
                         lambda i: (0, jnp.minimum((n_tiles - i) * per, n_tiles * per - 1), 0))]


def _lat_rwkv_pre(p_rw, lw, seg, perms, n_seq, t_len):
    prow, pval, _, rev = perms
    width = prow.shape[-1]
    p3 = p_rw.reshape(n_seq, t_len, RW_COLS)
    whole = lambda a: pl.BlockSpec(a.shape, lambda i: (0,) * a.ndim)
    params = [lw["mu_rkv"], lw["mu_lo"], lw["w_lora"], lw["b_lora"], lw["rw_kk"], lw["rw_ka"], lw["rw_rk"], seg]
    return pl.pallas_call(
        functools.partial(_lat_rwkv_pre_kernel, n_seq=n_seq),
        grid=(t_len // LAT_TILE,),
        in_specs=_lat_tile_specs(n_seq, t_len, RW_COLS) + [whole(a) for a in (rev, prow, pval, *params)],
        out_specs=[pl.BlockSpec((5, LAT_TILE, width), lambda i: (0, i, 0)),
                   pl.BlockSpec((LAT_TILE, width), lambda i: (i, 0)),
                   pl.BlockSpec((LAT_TILE, width), lambda i: (i, 0))],
        out_shape=[jax.ShapeDtypeStruct((5, t_len, width), F32),
                   jax.ShapeDtypeStruct((t_len, width), F32),
                   jax.ShapeDtypeStruct((t_len, width), F32)],
        compiler_params=_params("parallel"),
        name="lat_rwkv_pre",
    )(p3, p3, p3, p3, rev, prow, pval, *params)


def _unplace(yf_ref, yb_ref, rev_ref, pinv_ref, b):
    fwd = _exact_select_dot(yf_ref[...], pinv_ref[0, b])
    bwd = _exact_select_dot(yb_ref[...], pinv_ref[1, b])
    return fwd + _exact_row_select(rev_ref[...], bwd)


def _lat_rwkv_post_kernel(yf_ref, yb_ref, bf_ref, bb_ref, rev_ref, pinv_ref, gh_ref, g2_ref, lnw_ref, lnb_ref,
                          seg_ref, o_ref):
    b = pl.program_id(0)
    seg = seg_ref[...]
    y = _unplace(yf_ref, yb_ref, rev_ref, pinv_ref, b)
    mu = _seg_sum(y, seg) * (1.0 / HEAD_DIM)
    yc = y - mu
    var = _seg_sum(yc * yc, seg) * (1.0 / HEAD_DIM)
    y = yc * lax.rsqrt(var + RWKV_LN_EPS) * lnw_ref[...] + lnb_ref[...]
    y = y + _unplace(bf_ref, bb_ref, rev_ref, pinv_ref, b)
    o_ref[...] = y * _bdot(_sigmoid(gh_ref[...]), g2_ref[...])


def _lat_post_specs(t_len, width):
    n_tiles = t_len // LAT_TILE
    return [pl.BlockSpec((LAT_TILE, width), lambda b, i: (i, 0)),
            pl.BlockSpec((LAT_TILE, width), lambda b, i: (n_tiles - 1 - i, 0))]


def _lat_rwkv_post(y, bonus, p_rw, lw, seg, perms, n_seq, t_len):
    _, _, pinv, rev = perms
    width = y.shape[-1]
    n_tiles = t_len // LAT_TILE
    whole = lambda a: pl.BlockSpec(a.shape, lambda b, i: (0,) * a.ndim)
    params = [lw["rw_g2"], lw["rw_lnx_w"], lw["rw_lnx_b"], seg]
    return pl.pallas_call(
        _lat_rwkv_post_kernel,
        grid=(n_seq, n_tiles),
        in_specs=_lat_post_specs(t_len, width) * 2 + [whole(rev), whole(pinv),
                 pl.BlockSpec((LAT_TILE, GATE_RANK), lambda b, i: (b * n_tiles + i, 3 * MIX_W // GATE_RANK))]
        + [whole(a) for a in params],
        out_specs=pl.BlockSpec((LAT_TILE, MIX_W), lambda b, i: (b * n_tiles + i, 0)),
        out_shape=jax.ShapeDtypeStruct((n_seq * t_len, MIX_W), F32),
        compiler_params=_params("parallel", "parallel"),
        name="lat_rwkv_post",
    )(y, y, bonus, bonus, rev, pinv, p_rw, *params)


def _lat_hgrn_pre_kernel(pf_ref, pb_ref, rev_ref, prow_ref, pval_ref, lb_ref, z_ref, v_ref, *, n_seq):
    acc = [None] * 4
    for d in range(2):
        for b in range(n_seq):
            x = pf_ref[b] if d == 0 else _exact_row_select(rev_ref[...], pb_ref[b])
            q = x[:, 0:MIX_W]
            f_raw = x[:, (3 + d) * MIX_W:(4 + d) * MIX_W]
            lb = lb_ref[d]
            f = lb + (1.0 - lb) * _sigmoid(f_raw)
            kv = (1.0 - lb) * _sigmoid(-f_raw)
            placed = [_exact_select_dot(arr, prow_ref[d, b]) for arr in (f, kv, q * _sigmoid(q))]
            placed.append(_exact_select_dot(x[:, MIX_W:2 * MIX_W], pval_ref[d, b]))
            acc = [p if s is None else s + p for s, p in zip(acc, placed)]
    for j in range(3):
        z_ref[j] = acc[j]
    v_ref[...] = acc[3]


def _lat_hgrn_pre(p_hg, lb, perms, n_seq, t_len):
    prow, pval, _, rev = perms
    width = prow.shape[-1]
    p3 = p_hg.reshape(n_seq, t_len, HG_COLS)
    whole = lambda a: pl.BlockSpec(a.shape, lambda i: (0,) * a.ndim)
    return pl.pallas_call(
        functools.partial(_lat_hgrn_pre_kernel, n_seq=n_seq),
        grid=(t_len // LAT_TILE,),
        in_specs=_lat_tile_specs(n_seq, t_len, HG_COLS)[:2] + [whole(a) for a in (rev, prow, pval, lb)],
        out_specs=[pl.BlockSpec((3, LAT_TILE, width), lambda i: (0, i, 0)),
                   pl.BlockSpec((LAT_TILE, width), lambda i: (i, 0))],
        out_shape=[jax.ShapeDtypeStruct((3, t_len, width), F32),
                   jax.ShapeDtypeStruct((t_len, width), F32)],
        compiler_params=_params("parallel"),
        name="lat_hgrn_pre",
    )(p3, p3, rev, prow, pval, lb)


def _lat_hgrn_post_kernel(of_ref, ob_ref, rev_ref, pinv_ref, g_ref, nw_ref, seg_ref, out_ref):
    o = _unplace(of_ref, ob_ref, rev_ref, pinv_ref, pl.program_id(0))
    ms = _seg_sum(o * o, seg_ref[...]) * (1.0 / HEAD_DIM)
    out_ref[...] = o * lax.rsqrt(ms + NORM_EPS) * nw_ref[...] * _sigmoid(g_ref[...])


def _lat_hgrn_post(o, p_hg, hg_norm, seg, perms, n_seq, t_len):
    _, _, pinv, rev = perms
    width = o.shape[-1]
    n_tiles = t_len // LAT_TILE
    whole = lambda a: pl.BlockSpec(a.shape, lambda b, i: (0,) * a.ndim)
    return pl.pallas_call(
        _lat_hgrn_post_kernel,
        grid=(n_seq, n_tiles),
        in_specs=_lat_post_specs(t_len, width) + [whole(rev), whole(pinv),
                 pl.BlockSpec((LAT_TILE, MIX_W), lambda b, i: (b * n_tiles + i, 2)),
                 whole(hg_norm), whole(seg)],
        out_specs=pl.BlockSpec((LAT_TILE, MIX_W), lambda b, i: (b * n_tiles + i, 0)),
        out_shape=jax.ShapeDtypeStruct((n_seq * t_len, MIX_W), F32),
        compiler_params=_params("parallel", "parallel"),
        name="lat_hgrn_post",
    )(o, o, rev, pinv, p_hg, hg_norm, seg)


def _lat_recurrent_mixers(p_rw, p_hg, lw, seg, perms, n_seq, t_len, state_rw, state_hg):
    z, v, bonus = _lat_rwkv_pre(p_rw, lw, seg, perms, n_seq, t_len)
    y, _ = _scan(z[None], v[None], _state_to_scan(state_rw, True), delta=True, steps=LAT_SCAN_STEPS)
    ya = _lat_rwkv_post(y.reshape(v.shape), bonus, p_rw, lw, seg, perms, n_seq, t_len)
    z, v = _lat_hgrn_pre(p_hg, lw["hg_lb"], perms, n_seq, t_len)
    o, _ = _scan(z[None], v[None], _state_to_scan(state_hg, False), delta=False, steps=LAT_SCAN_STEPS)
    yc = _lat_hgrn_post(o.reshape(v.shape), p_hg, lw["hg_norm"], seg, perms, n_seq, t_len)
    return ya, yc


def kernel(x_prompt, x_sample, cache_nat_kv, cache_swa_kv, state_rwkv, state_hgrn, c, c_ctx, norm_g, mod_w, mod_b, w_in, w_out, rw_mu_rkv, rw_mu_lora, rw_w0, rw_w2, rw_a0, rw_a2, rw_g2, rw_kk, rw_ka, rw_rk, rw_lnx_w, rw_lnx_b, nat_rpb, hg_lb_logits, hg_norm, swa_sink, ffn_w1, ffn_w2):
    n_ctx, t_ctx, _ = x_prompt.shape
    n_lat, t_lat, _ = x_sample.shape
    past = cache_nat_kv.shape[3]

    cond = jnp.zeros((MOD_ROWS, D_MODEL), F32).at[0].set(c_ctx).at[1:1 + n_lat].set(c)
    mod = _modulation_all(cond, mod_w, mod_b)

    w_in_bf, w_out_bf = w_in.astype(BF16), w_out.astype(BF16)
    w1_bf, w2_bf = ffn_w1.astype(BF16), ffn_w2.astype(BF16)
    cuts = np.cumsum((0,) + SLABS)
    lat_slabs = tuple(w_in_bf[:, :, cuts[j]:cuts[j + 1]] for j in range(4))
    ctx_modes = ("fm", "seq", "fm", "seq")
    ctx_y_modes = ("tok", "seq", "tok", "seq")
    lat_modes = ("tok",) * 4
    ctx_slabs = tuple(w.transpose(0, 2, 1) if m == "fm" else w for w, m in zip(lat_slabs, ctx_modes))

    lb_sm = jax.nn.softmax(hg_lb_logits.astype(F32), axis=1)
    hg_lb = jnp.cumsum(lb_sm, axis=1) - lb_sm[:, :1]

    head_of = np.arange(MIX_W) // HEAD_DIM
    seg = jnp.asarray((head_of[:, None] == head_of[None, :]).astype(np.float32))
    rope = _rope_tables(t_lat)
    lat_perms = _latent_perms(n_lat)
    zeros_lora = jnp.zeros((LORA_RANK, MIX_W), F32)
    cache_nat = cache_nat_kv.reshape(n_lat, DEPTH, 2, past, MIX_W)
    cache_swa = cache_swa_kv.reshape(n_lat, DEPTH, 2, past, SWA_KV_HEADS * HEAD_DIM)

    xp = x_prompt.transpose(1, 0, 2).reshape(t_ctx * n_ctx, D_MODEL)
    xs = x_sample.reshape(n_lat * t_lat, D_MODEL)
    ctx_row, lat_row = _mod_row(t_ctx, 0), _mod_row(t_lat, 1)
    nat_out, swa_out, rw_out, hg_out = [], [], [], []
    for l in range(DEPTH):
        lw = {
            "mu_rkv": rw_mu_rkv[l].reshape(2, 1, 3 * MIX_W),
            "mu_lo": rw_mu_lora[l].reshape(2, 1, 2 * LORA_RANK),
            "w_lora": jnp.stack([jnp.concatenate(
                [jnp.concatenate([rw_w2[l, d], zeros_lora], axis=1),
                 jnp.concatenate([zeros_lora, rw_a2[l, d]], axis=1)], axis=0) for d in range(2)]),
            "b_lora": jnp.concatenate([rw_w0[l], rw_a0[l]], axis=-1).reshape(2, 1, 2 * MIX_W),
            "rw_kk": rw_kk[l].reshape(1, MIX_W), "rw_ka": rw_ka[l].reshape(1, MIX_W),
            "rw_rk": rw_rk[l].reshape(1, MIX_W), "rw_g2": rw_g2[l],
            "rw_lnx_w": rw_lnx_w[l].reshape(1, MIX_W), "rw_lnx_b": rw_lnx_b[l].reshape(1, MIX_W),
            "hg_lb": hg_lb[:, l].reshape(2, 1, MIX_W), "hg_norm": hg_norm[l].reshape(1, MIX_W),
            "mu_rkv_col": rw_mu_rkv[l].reshape(2, 3 * MIX_W, 1), "mu_lo_col": rw_mu_lora[l].reshape(2, 2 * LORA_RANK, 1),
            "w2t": rw_w2[l].transpose(0, 2, 1), "a2t": rw_a2[l].transpose(0, 2, 1),
            "w0_col": rw_w0[l].reshape(2, MIX_W, 1), "a0_col": rw_a0[l].reshape(2, MIX_W, 1),
            "kk_col": rw_kk[l].reshape(MIX_W, 1), "ka_col": rw_ka[l].reshape(MIX_W, 1),
            "rk_col": rw_rk[l].reshape(MIX_W, 1), "g2t": rw_g2[l].T,
            "lnw_col": rw_lnx_w[l].reshape(MIX_W, 1), "lnb_col": rw_lnx_b[l].reshape(MIX_W, 1),
            "hg_lb_col": hg_lb[:, l].reshape(2, MIX_W, 1), "hg_norm_col": hg_norm[l].reshape(MIX_W, 1),
        }
        sink = swa_sink[l]

        pt_rw, p_nat, pt_hg, p_swa = _in_projection(xp, norm_g, mod, ctx_slabs, l, ctx_row, ctx_modes, n_ctx)
        ya, yc, s_rw, s_hg = _ctx_recurrent_mixers(pt_rw, pt_hg, lw, n_ctx, t_ctx)
        yb, cnat = _ctx_attention(p_nat, sink, N_HEADS, False)
        yd, cswa = _ctx_attention(p_swa, sink, SWA_KV_HEADS, True)
        xp = _out_projection_ffn(xp, (ya, yb, yc, yd), norm_g, mod, w_out_bf, w1_bf, w2_bf, l, ctx_row,
                                 ctx_y_modes, n_ctx)
        nat_out.append(cnat.reshape(n_ctx, 2, t_ctx, N_HEADS, HEAD_DIM))
        swa_out.append(cswa.reshape(n_ctx, 2, t_ctx, SWA_KV_HEADS, HEAD_DIM))
        rw_out.append(_state_from_scan(s_rw, n_ctx, True))
        hg_out.append(_state_from_scan(s_hg, n_ctx, False))

        p_rw, p_nat, p_hg, p_swa = _in_projection(xs, norm_g, mod, lat_slabs, l, lat_row, lat_modes)
        ya, yc = _lat_recurrent_mixers(p_rw, p_hg, lw, seg, lat_perms, n_lat, t_lat,
                                       state_rwkv[:, l], state_hgrn[:, l])
        yb = _nat_latent(p_nat, cache_nat, _nat_bias_table(nat_rpb[l]), l, n_lat, t_lat, 0)
        yd = _swa_latent(p_swa, cache_swa, sink, rope, l, n_lat, t_lat, 0)
        xs = _out_projection_ffn(xs, (ya, yb, yc, yd), norm_g, mod, w_out_bf, w1_bf, w2_bf, l, lat_row,
                                 lat_modes)

    return (xp.reshape(t_ctx, n_ctx, D_MODEL).transpose(1, 0, 2), xs.reshape(x_sample.shape),
            jnp.stack(nat_out, axis=1), jnp.stack(swa_out, axis=1),
            jnp.stack(rw_out, axis=1), jnp.stack(hg_out, axis=1))
```
